```python
import jax, jax.numpy as jnp
from jax import lax
import numpy as np

D_MODEL = 1024
BATCH = 16
SEQ = 2048
DEPTH = 2
DEC_BATCH = 128
DEC_SEQ = 4
PAST_LEN = 16384
PAGE_SIZE = 128

N_A_LAYERS = DEPTH // 2
N_B_LAYERS = DEPTH - N_A_LAYERS
N_DENSE = (DEPTH + 1) // 2
N_MOE = DEPTH // 2
CONV_CH = D_MODEL
CONV_W = 31
HEAD_DIM = 64
N_HEADS = D_MODEL // HEAD_DIM
N_KV_HEADS = 4
GROUP = N_HEADS // N_KV_HEADS
WINDOW = 128
BLOCK = WINDOW
NUM_BUCKETS = 32
MAX_DISTANCE = 128
D_FF = 2816
N_EXPERTS = 8
TOP_K = 2
D_FF_EXPERT = 3584
EPS = 1e-6

kernel_name = 'yoco_conformer_swa_sink_moe_step'


def rms_norm(x, g):
    xf = x.astype(jnp.float32)
    y = xf * lax.rsqrt(jnp.mean(xf * xf, axis=-1, keepdims=True) + EPS)
    return (y * g.astype(jnp.float32)).astype(x.dtype)


def layer_norm(x, g, b):
    xf = x.astype(jnp.float32)
    mu = jnp.mean(xf, axis=-1, keepdims=True)
    xc = xf - mu
    var = jnp.mean(xc * xc, axis=-1, keepdims=True)
    y = xc * lax.rsqrt(var + EPS) * g.astype(jnp.float32) + b.astype(jnp.float32)
    return y.astype(x.dtype)


def t5_bucket(dist):
    n = np.maximum(dist, 0)
    max_exact = NUM_BUCKETS // 2
    large = max_exact + (np.log(np.maximum(n, 1) / max_exact) / np.log(MAX_DISTANCE / max_exact)
                         * (NUM_BUCKETS - max_exact)).astype(np.int32)
    large = np.minimum(large, NUM_BUCKETS - 1)
    return np.where(n < max_exact, n, large).astype(np.int32)


def rel_bias_heads(rel_bias, dist):
    q_len, s_len = dist.shape
    b = jnp.take(rel_bias, jnp.asarray(t5_bucket(dist)), axis=0).astype(jnp.float32)
    return jnp.transpose(b, (2, 0, 1)).reshape(N_KV_HEADS, GROUP, q_len, s_len)


def sink_attend(s, sinks, v, eq):
    sink = sinks.astype(jnp.float32).reshape(N_KV_HEADS, GROUP)[:, :, None, None]
    m = jnp.maximum(jnp.max(s, axis=-1, keepdims=True), sink)
    p = jnp.exp(s - m)
    denom = jnp.sum(p, axis=-1, keepdims=True) + jnp.exp(sink - m)
    return jnp.einsum(eq, (p / denom).astype(v.dtype), v)


def swa_prompt(q, k, v, sinks, rel_bias):
    B, T = q.shape[0], q.shape[1]
    nb = T // BLOCK
    qb = q.reshape(B, nb, BLOCK, N_KV_HEADS, GROUP, HEAD_DIM)

    def band(t):
        tb = t.reshape(B, nb, BLOCK, N_KV_HEADS, HEAD_DIM)
        prev = jnp.pad(tb, ((0, 0), (1, 0), (0, 0), (0, 0), (0, 0)))[:, :-1]
        return jnp.concatenate([prev, tb], axis=2)

    kb, vb = band(k), band(v)
    s = jnp.einsum('bnqkgd,bnskd->bnkgqs', qb, kb).astype(jnp.float32) * (HEAD_DIM ** -0.5)
    dist = np.arange(BLOCK)[:, None] + BLOCK - np.arange(2 * BLOCK)[None, :]
    valid = (dist >= 0) & (dist < WINDOW)
    first = (np.arange(nb)[:, None, None] == 0) & (np.arange(2 * BLOCK)[None, None, :] < BLOCK)
    valid = valid[None] & ~first
    s = s + rel_bias_heads(rel_bias, dist)
    s = jnp.where(valid[None, :, None, None], s, -jnp.inf)
    o = sink_attend(s, sinks, vb, 'bnkgqs,bnskd->bnqkgd')
    return o.reshape(B, T, N_HEADS * HEAD_DIM)


def swa_sample(q, kc, vc, sinks, rel_bias):
    Bd, S = q.shape[0], q.shape[1]
    s = jnp.einsum('bqkgd,bskd->bkgqs', q, kc).astype(jnp.float32) * (HEAD_DIM ** -0.5)
    dist = np.arange(S)[:, None] + WINDOW - np.arange(WINDOW + S)[None, :]
    valid = (dist >= 0) & (dist < WINDOW)
    s = s + rel_bias_heads(rel_bias, dist)
    s = jnp.where(valid, s, -jnp.inf)
    o = sink_attend(s, sinks, vc, 'bkgqs,bskd->bqkgd')
    return o.reshape(Bd, S, N_HEADS * HEAD_DIM)


def conv_module(x, prev, g_norm, w_pw1, b_pw1, w_dw, b_dw, g_ln, b_ln, w_pw2, b_pw2):
    h = rms_norm(x, g_norm)
    u = h @ w_pw1 + b_pw1
    a, gt = jnp.split(u, 2, axis=-1)
    u = a * jax.nn.sigmoid(gt)
    full = jnp.concatenate([prev, u], axis=1)
    c = lax.conv_general_dilated(full, w_dw[:, None, :], window_strides=(1,), padding='VALID',
                                 dimension_numbers=('NWC', 'WIO', 'NWC'),
                                 feature_group_count=CONV_CH) + b_dw
    c = jax.nn.silu(layer_norm(c, g_ln, b_ln))
    return x + c @ w_pw2 + b_pw2, full[:, -(CONV_W - 1):]


def dense_swiglu(x, g, w_gate, w_up, w_down):
    h = rms_norm(x, g)
    return x + (jax.nn.silu(h @ w_gate) * (h @ w_up)) @ w_down


def moe_swiglu(x, g, w_router, w_eg, w_eu, w_ed):
    h = rms_norm(x, g)
    hf = h.reshape(-1, D_MODEL)
    logits = (hf @ w_router).astype(jnp.float32)
    top_v, top_i = lax.top_k(logits, TOP_K)
    gates = jax.nn.softmax(top_v, axis=-1)
    combine = jnp.einsum('tk,tke->te', gates,
                         jax.nn.one_hot(top_i, N_EXPERTS, dtype=jnp.float32)).astype(h.dtype)
    y = jnp.zeros_like(hf)
    for e in range(N_EXPERTS):
        he = jax.nn.silu(hf @ w_eg[e]) * (hf @ w_eu[e])
        y = y + combine[:, e:e + 1] * (he @ w_ed[e])
    return x + y.reshape(x.shape)


def trunk(x, conv_prev, k_win, v_win, P):
    N, T = x.shape[0], x.shape[1]
    h = x
    conv_out = []
    k = v = kc = vc = None
    for l in range(DEPTH):
        if l < N_A_LAYERS:
            prev = (jnp.zeros((N, CONV_W - 1, CONV_CH), x.dtype) if conv_prev is None else conv_prev[l])
            h, st = conv_module(h, prev, P['g_conv_norm'][l], P['w_pw1'][l], P['b_pw1'][l],
                                P['w_dw'][l], P['b_dw'][l], P['g_ln'][l], P['b_ln'][l],
                                P['w_pw2'][l], P['b_pw2'][l])
            conv_out.append(st)
        else:
            j = l - N_A_LAYERS
            if j == 0:
                hk = rms_norm(h, P['g_kv_norm'])
                kv = (hk @ P['w_kv']).reshape(N, T, 2, N_KV_HEADS, HEAD_DIM)
                k = rms_norm(kv[:, :, 0], P['g_k_norm'])
                v = kv[:, :, 1]
                if k_win is not None:
                    kc = jnp.concatenate([k_win, k], axis=1)
                    vc = jnp.concatenate([v_win, v], axis=1)
            hq = rms_norm(h, P['g_attn_norm'][j])
            q = rms_norm((hq @ P['w_q'][j]).reshape(N, T, N_KV_HEADS, GROUP, HEAD_DIM), P['g_q_norm'][j])
            if k_win is None:
                o = swa_prompt(q, k, v, P['sinks'][j], P['rel_bias'])
            else:
                o = swa_sample(q, kc, vc, P['sinks'][j], P['rel_bias'])
            h = h + o @ P['w_o'][j]
        if l % 2 == 0:
            i = l // 2
            h = dense_swiglu(h, P['g_ffn_norm'][i], P['w_gate'][i], P['w_up'][i], P['w_down'][i])
        else:
            i = l // 2
            h = moe_swiglu(h, P['g_moe_norm'][i], P['w_router'][i], P['w_e_gate'][i],
                           P['w_e_up'][i], P['w_e_down'][i])
    if k_win is None:
        k_new, v_new = k[:, -WINDOW:], v[:, -WINDOW:]
    else:
        k_new, v_new = kc[:, -WINDOW:], vc[:, -WINDOW:]
    return h, jnp.stack(conv_out), k_new, v_new


def setup_inputs(seed: int = 0) -> dict:
    key = jax.random.key(seed)
    ks = iter(jax.random.split(key, 40))

    def nrm(shape, scale):
        return jax.random.normal(next(ks), shape, jnp.float32) * scale

    def gain(shape):
        return 1.0 + nrm(shape, 0.02)

    HD = N_HEADS * HEAD_DIM
    return {
        'x_prompt': nrm((BATCH, SEQ, D_MODEL), 1.0),
        'x_sample': nrm((DEC_BATCH, DEC_SEQ, D_MODEL), 1.0),
        'state_conv': nrm((N_A_LAYERS, DEC_BATCH, CONV_W - 1, CONV_CH), 0.5),
        'cache_k_win': nrm((DEC_BATCH, WINDOW, N_KV_HEADS, HEAD_DIM), 1.0),
        'cache_v_win': nrm((DEC_BATCH, WINDOW, N_KV_HEADS, HEAD_DIM), 1.0),
        'g_conv_norm': gain((N_A_LAYERS, D_MODEL)),
        'w_pw1': nrm((N_A_LAYERS, D_MODEL, 2 * CONV_CH), D_MODEL ** -0.5),
        'b_pw1': nrm((N_A_LAYERS, 2 * CONV_CH), 0.02),
        'w_dw': nrm((N_A_LAYERS, CONV_W, CONV_CH), CONV_W ** -0.5),
        'b_dw': nrm((N_A_LAYERS, CONV_CH), 0.02),
        'g_ln': gain((N_A_LAYERS, CONV_CH)),
        'b_ln': nrm((N_A_LAYERS, CONV_CH), 0.02),
        'w_pw2': nrm((N_A_LAYERS, CONV_CH, D_MODEL), CONV_CH ** -0.5),
        'b_pw2': nrm((N_A_LAYERS, D_MODEL), 0.02),
        'g_kv_norm': gain((D_MODEL,)),
        'w_kv': nrm((D_MODEL, 2 * N_KV_HEADS * HEAD_DIM), D_MODEL ** -0.5),
        'g_k_norm': gain((HEAD_DIM,)),
        'g_attn_norm': gain((N_B_LAYERS, D_MODEL)),
        'w_q': nrm((N_B_LAYERS, D_MODEL, HD), D_MODEL ** -0.5),
        'g_q_norm': gain((N_B_LAYERS, HEAD_DIM)),
        'sinks': nrm((N_B_LAYERS, N_HEADS), 1.0),
        'w_o': nrm((N_B_LAYERS, HD, D_MODEL), HD ** -0.5),
        'rel_bias': nrm((NUM_BUCKETS, N_HEADS), 0.5),
        'g_ffn_norm': gain((N_DENSE, D_MODEL)),
        'w_gate': nrm((N_DENSE, D_MODEL, D_FF), D_MODEL ** -0.5),
        'w_up': nrm((N_DENSE, D_MODEL, D_FF), D_MODEL ** -0.5),
        'w_down': nrm((N_DENSE, D_FF, D_MODEL), D_FF ** -0.5),
        'g_moe_norm': gain((N_MOE, D_MODEL)),
        'w_router': nrm((N_MOE, D_MODEL, N_EXPERTS), D_MODEL ** -0.5),
        'w_e_gate': nrm((N_MOE, N_EXPERTS, D_MODEL, D_FF_EXPERT), D_MODEL ** -0.5),
        'w_e_up': nrm((N_MOE, N_EXPERTS, D_MODEL, D_FF_EXPERT), D_MODEL ** -0.5),
        'w_e_down': nrm((N_MOE, N_EXPERTS, D_FF_EXPERT, D_MODEL), D_FF_EXPERT ** -0.5),
    }


def reference(x_prompt, x_sample, state_conv, cache_k_win, cache_v_win,
              g_conv_norm, w_pw1, b_pw1, w_dw, b_dw, g_ln, b_ln, w_pw2, b_pw2,
              g_kv_norm, w_kv, g_k_norm,
              g_attn_norm, w_q, g_q_norm, sinks, w_o, rel_bias,
              g_ffn_norm, w_gate, w_up, w_down,
              g_moe_norm, w_router, w_e_gate, w_e_up, w_e_down):
    P = dict(g_conv_norm=g_conv_norm, w_pw1=w_pw1, b_pw1=b_pw1, w_dw=w_dw, b_dw=b_dw,
             g_ln=g_ln, b_ln=b_ln, w_pw2=w_pw2, b_pw2=b_pw2,
             g_kv_norm=g_kv_norm, w_kv=w_kv, g_k_norm=g_k_norm,
             g_attn_norm=g_attn_norm, w_q=w_q, g_q_norm=g_q_norm, sinks=sinks, w_o=w_o,
             rel_bias=rel_bias,
             g_ffn_norm=g_ffn_norm, w_gate=w_gate, w_up=w_up, w_down=w_down,
             g_moe_norm=g_moe_norm, w_router=w_router, w_e_gate=w_e_gate,
             w_e_up=w_e_up, w_e_down=w_e_down)
    y_prompt, conv_p, k_p, v_p = trunk(x_prompt, None, None, None, P)
    y_sample, conv_s, k_s, v_s = trunk(x_sample, state_conv, cache_k_win, cache_v_win, P)
    return (y_prompt, y_sample, conv_p, conv_s, k_p, k_s, v_p, v_s)
```

```python
import functools

import numpy as np
import jax
import jax.numpy as jnp
from jax import lax
from jax.experimental import pallas as pl
from jax.experimental.pallas import tpu as pltpu

D = 1024
CONV_W = 31
HALO = 32
DH = 64
NH = 16
NKV = 4
G = NH // NKV
KVW = NKV * DH
WIN = 128
NB = 32
MAXD = 128
NE = 8
EPS = 1e-6
LANES = 128
NEG = float("-inf")

TT_CONV = 512
TM_FFN = 512
TQ_ATT = 256
TOK = 512
TM_MOE = 1024
TF_MOE = 512
SEQ_TILE = 8
SEQ_TILE_CONV = 32
FF_CHUNK = 1024
VMEM_LIMIT = 56 * 1024 * 1024

bf16 = jnp.bfloat16
f32 = jnp.float32


def _cp(*sem):
    return pltpu.CompilerParams(dimension_semantics=sem, vmem_limit_bytes=VMEM_LIMIT)


def _full(shape):
    n = len(shape)
    return pl.BlockSpec(shape, lambda *_: (0,) * n)


def _sigmoid(x):
    return 1.0 / (1.0 + jnp.exp(-x))


def _rms(x):
    return x * lax.rsqrt(jnp.mean(x * x, axis=-1, keepdims=True) + EPS)


def _dot(a, b):
    return jnp.dot(a, b, preferred_element_type=f32)


def _dot_nt(a, b):
    return lax.dot_general(a, b, (((1,), (1,)), ((), ())), preferred_element_type=f32)


def _head_rms(x, gain):
    lo = lax.broadcasted_iota(jnp.int32, (1, LANES), 1) < DH
    cols = []
    for c in range(x.shape[1] // LANES):
        col = x[:, c * LANES:(c + 1) * LANES]
        sq = col * col
        tot = jnp.sum(sq, axis=-1, keepdims=True)
        ev = jnp.sum(jnp.where(lo, sq, 0.0), axis=-1, keepdims=True)
        od = tot - ev
        r = jnp.where(lo, lax.rsqrt(ev * (1.0 / DH) + EPS), lax.rsqrt(od * (1.0 / DH) + EPS))
        cols.append(col * r)
    return jnp.concatenate(cols, axis=-1) * gain


def _kv_lane_mask(kv):
    lane = lax.broadcasted_iota(jnp.int32, (1, KVW), 1)
    return (lane >= kv * DH) & (lane < (kv + 1) * DH)


def _bucket_of(dist):
    n = np.maximum(dist, 0)
    max_exact = NB // 2
    large = max_exact + (np.log(np.maximum(n, 1) / max_exact) / np.log(MAXD / max_exact)
                         * (NB - max_exact)).astype(np.int32)
    large = np.minimum(large, NB - 1)
    return np.where(n < max_exact, n, large).astype(np.int32)


def _prompt_bucket_map():
    dist = np.arange(WIN)[:, None] + WIN - np.arange(2 * WIN)[None, :]
    valid = (dist >= 0) & (dist < WIN)
    bk = np.where(valid, _bucket_of(dist), -1).astype(np.int32)
    first = bk.copy()
    first[:, :WIN] = -1
    return np.stack([first, bk])


def _sample_bucket_maps(s_len, st):
    r_t = np.repeat(np.arange(s_len), st)
    r_b = np.tile(np.arange(st), s_len)
    c_b = np.repeat(np.arange(st), WIN)
    c_s = np.tile(np.arange(WIN), st)
    dist = r_t[:, None] + WIN - c_s[None, :]
    valid = (r_b[:, None] == c_b[None, :]) & (dist >= 0) & (dist < WIN)
    map_c = np.where(valid, _bucket_of(dist), -1).astype(np.int32)
    dist_n = r_t[:, None] - r_t[None, :]
    valid_n = (r_b[:, None] == r_b[None, :]) & (dist_n >= 0) & (dist_n < WIN)
    map_n = np.full((s_len * st, LANES), -1, np.int32)
    map_n[:, :s_len * st] = np.where(valid_n, _bucket_of(dist_n), -1)
    return map_c[None], map_n[None]


def _bias_kernel(rb_ref, map_ref, out_ref):
    h = pl.program_id(1)
    bk = map_ref[0]
    acc = jnp.full(bk.shape, NEG, f32)
    for b in range(NB):
        acc = jnp.where(bk == b, rb_ref[b, h], acc)
    out_ref[0, 0] = acc


def _bias_table(rel_bias, bucket_map):
    v, r, c = bucket_map.shape
    return pl.pallas_call(
        _bias_kernel,
        grid=(v, NH),
        in_specs=[pl.BlockSpec(memory_space=pltpu.SMEM),
                  pl.BlockSpec((1, r, c), lambda i, h: (i, 0, 0))],
        out_specs=pl.BlockSpec((1, 1, r, c), lambda i, h: (i, h, 0, 0)),
        out_shape=jax.ShapeDtypeStruct((v, NH, r, c), f32),
        compiler_params=_cp("arbitrary", "arbitrary"),
        name="bias_table",
    )(rel_bias, jnp.asarray(bucket_map))


def _glu_rows(x, gn_ref, w1_ref, b1_ref):
    h = (_rms(x) * gn_ref[...]).astype(bf16)
    u2 = _dot(h, w1_ref[...]) + b1_ref[...]
    return u2[:, :D] * _sigmoid(u2[:, D:])


def _conv_tail(x, c, gln_ref, bln_ref, w2_ref, b2_ref):
    mu = jnp.mean(c, axis=-1, keepdims=True)
    xc = c - mu
    var = jnp.mean(xc * xc, axis=-1, keepdims=True)
    y = xc * lax.rsqrt(var + EPS) * gln_ref[...] + bln_ref[...]
    y = y * _sigmoid(y)
    return x + _dot(y.astype(bf16), w2_ref[...]) + b2_ref[...]


def _conv_prompt_kernel(x_ref, gn_ref, w1_ref, b1_ref, wdw_ref, bdw_ref, gln_ref, bln_ref,
                        w2_ref, b2_ref, x1_ref, ulast_ref, ubuf):
    tt = x_ref.shape[1]
    t = pl.program_id(1)
    x = x_ref[0]
    u = _glu_rows(x, gn_ref, w1_ref, b1_ref)

    @pl.when(t == 0)
    def _():
        ubuf[0:HALO, :] = jnp.zeros((HALO, D), f32)

    @pl.when(t > 0)
    def _():
        ubuf[0:HALO, :] = ubuf[tt:tt + HALO, :]

    ubuf[HALO:HALO + tt, :] = u
    ulast_ref[0] = u[tt - HALO:, :]
    off = HALO - (CONV_W - 1)
    c = jnp.broadcast_to(bdw_ref[...], (tt, D))
    for k in range(CONV_W):
        c = c + wdw_ref[k:k + 1, :] * ubuf[off + k:off + k + tt, :]
    x1_ref[0] = _conv_tail(x, c, gln_ref, bln_ref, w2_ref, b2_ref)


def _conv_prompt(x, wts):
    b, t, _ = x.shape
    tt = min(TT_CONV, t)
    assert t % tt == 0 and tt >= HALO
    row = lambda n: _full((1, n))
    return pl.pallas_call(
        _conv_prompt_kernel,
        grid=(b, t // tt),
        in_specs=[pl.BlockSpec((1, tt, D), lambda i, j: (i, j, 0)),
                  row(D), _full((D, 2 * D)), row(2 * D), _full((HALO, D)), row(D), row(D), row(D),
                  _full((D, D)), row(D)],
        out_specs=[pl.BlockSpec((1, tt, D), lambda i, j: (i, j, 0)),
                   pl.BlockSpec((1, HALO, D), lambda i, j: (i, 0, 0))],
        out_shape=[jax.ShapeDtypeStruct((b, t, D), f32),
                   jax.ShapeDtypeStruct((b, HALO, D), f32)],
        scratch_shapes=[pltpu.VMEM((tt + HALO, D), f32)],
        compiler_params=_cp("arbitrary", "arbitrary"),
        name="conv_prompt",
    )(x, *wts)


def _conv_sample_kernel(x_ref, st_ref, gn_ref, w1_ref, b1_ref, wdw_ref, bdw_ref, gln_ref, bln_ref,
                        w2_ref, b2_ref, x1_ref, u_ref):
    s_len, bs, _ = x_ref.shape
    x = x_ref[...].reshape(s_len * bs, D)
    u = _glu_rows(x, gn_ref, w1_ref, b1_ref)
    u_ref[...] = u.reshape(s_len, bs, D)
    n_prev = CONV_W - 1
    cs = []
    for t in range(s_len):
        c = jnp.broadcast_to(bdw_ref[...], (bs, D))
        for j in range(t, n_prev):
            c = c + wdw_ref[j - t:j - t + 1, :] * st_ref[j]
        for i in range(t + 1):
            k = n_prev - t + i
            c = c + wdw_ref[k:k + 1, :] * u[i * bs:(i + 1) * bs, :]
        cs.append(c)
    c = jnp.concatenate(cs, axis=0)
    x1_ref[...] = _conv_tail(x, c, gln_ref, bln_ref, w2_ref, b2_ref).reshape(s_len, bs, D)


def _conv_sample(x_tb, state_t, wts):
    s_len, bd, _ = x_tb.shape
    bs = min(SEQ_TILE_CONV, bd)
    assert bd % bs == 0 and bs % 8 == 0 and s_len <= CONV_W - 1
    row = lambda n: _full((1, n))
    blk = pl.BlockSpec((s_len, bs, D), lambda i: (0, i, 0))
    return pl.pallas_call(
        _conv_sample_kernel,
        grid=(bd // bs,),
        in_specs=[blk, pl.BlockSpec((CONV_W - 1, bs, D), lambda i: (0, i, 0)),
                  row(D), _full((D, 2 * D)), row(2 * D), _full((HALO, D)), row(D), row(D), row(D),
                  _full((D, D)), row(D)],
        out_specs=[blk, blk],
        out_shape=[jax.ShapeDtypeStruct((s_len, bd, D), f32)] * 2,
        compiler_params=_cp("arbitrary"),
        name="conv_sample",
    )(x_tb, state_t, *wts)


def _ffn_kernel(x_ref, g_ref, wg_ref, wu_ref, wd_ref, o_ref):
    x = x_ref[...]
    h = (_rms(x) * g_ref[...]).astype(bf16)
    ff = wg_ref.shape[1]
    acc = x
    for c0 in range(0, ff, FF_CHUNK):
        c1 = min(c0 + FF_CHUNK, ff)
        a = _dot(h, wg_ref[:, c0:c1])
        a = a * _sigmoid(a) * _dot(h, wu_ref[:, c0:c1])
        acc = acc + _dot(a.astype(bf16), wd_ref[c0:c1, :])
    o_ref[...] = acc


def _ffn(x, g, wg, wu, wd):
    n = x.shape[0]
    tm = min(TM_FFN, n)
    assert n % tm == 0
    ff = wg.shape[1]
    return pl.pallas_call(
        _ffn_kernel,
        grid=(n // tm,),
        in_specs=[pl.BlockSpec((tm, D), lambda i: (i, 0)), _full((1, D)),
                  _full((D, ff)), _full((D, ff)), _full((ff, D))],
        out_specs=pl.BlockSpec((tm, D), lambda i: (i, 0)),
        out_shape=jax.ShapeDtypeStruct((n, D), f32),
        compiler_params=_cp("arbitrary"),
        name="dense_ffn",
    )(x, g, wg, wu, wd)


def _qkv(x, gkv_ref, wkv_ref, gk_ref, gq_attn_ref, wq_ref, gq_ref):
    xn = _rms(x)
    kv = _dot((xn * gkv_ref[...]).astype(bf16), wkv_ref[...])
    k = _head_rms(kv[:, :KVW], gk_ref[...])
    v = kv[:, KVW:]
    q = _dot((xn * gq_attn_ref[...]).astype(bf16), wq_ref[...])
    q = _head_rms(q, gq_ref[...]) * (DH ** -0.5)
    return q, k, v


def _sink_col(sinks_ref, kv, rows_per_head):
    return jnp.concatenate(
        [jnp.full((rows_per_head, 1), sinks_ref[kv * G + g], f32) for g in range(G)], axis=0)


def _moe_prep(x3, gmoe_ref, wr_ref, h_ref, route_ref, route_t_ref):
    h = _rms(x3) * gmoe_ref[...]
    h_ref[...] = h
    logits = _dot(h.astype(bf16), wr_ref[...])
    lane = lax.broadcasted_iota(jnp.int32, logits.shape, 1).astype(f32)
    logits = jnp.where(lane < NE, logits, NEG)
    m1 = jnp.max(logits, axis=-1, keepdims=True)
    i1 = jnp.min(jnp.where(logits == m1, lane, float(LANES)), axis=-1, keepdims=True)
    rest = jnp.where(lane == i1, NEG, logits)
    m2 = jnp.max(rest, axis=-1, keepdims=True)
    i2 = jnp.min(jnp.where(rest == m2, lane, float(LANES)), axis=-1, keepdims=True)
    e2 = jnp.exp(m2 - m1)
    den = 1.0 + e2
    g1 = 1.0 / den
    g2 = e2 / den
    route = jnp.where(lane == 0, i1,
                      jnp.where(lane == 1, i2,
                                jnp.where(lane == 2, g1, jnp.where(lane == 3, g2, 0.0))))
    route_ref[...] = route
    route_t_ref[...] = route.T[0:8, :]


def _attn_prompt_kernel(sinks_ref, x_ref, gkv_ref, wkv_ref, gk_ref, gqa_ref, wq_ref, gq_ref,
                        bias_ref, wo_ref, gmoe_ref, wr_ref,
                        x3_ref, h_ref, route_ref, route_t_ref, klast_ref, vlast_ref,
                        kbuf, vbuf, obuf):
    tq = x_ref.shape[1]
    t = pl.program_id(1)
    x = x_ref[0]
    q, k, v = _qkv(x, gkv_ref, wkv_ref, gk_ref, gqa_ref, wq_ref, gq_ref)
    klast_ref[0] = k[tq - WIN:, :]
    vlast_ref[0] = v[tq - WIN:, :]

    @pl.when(t == 0)
    def _():
        kbuf[:, 0:WIN, :] = jnp.zeros((NKV, WIN, KVW), bf16)
        vbuf[:, 0:WIN, :] = jnp.zeros((NKV, WIN, KVW), bf16)

    @pl.when(t > 0)
    def _():
        kbuf[:, 0:WIN, :] = kbuf[:, tq:tq + WIN, :]
        vbuf[:, 0:WIN, :] = vbuf[:, tq:tq + WIN, :]

    kb = k.astype(bf16)
    vb = v.astype(bf16)
    for kv in range(NKV):
        m = _kv_lane_mask(kv)
        kbuf[kv, WIN:WIN + tq, :] = jnp.where(m, kb, jnp.zeros_like(kb))
        vbuf[kv, WIN:WIN + tq, :] = jnp.where(m, vb, jnp.zeros_like(vb))

    qb = q.astype(bf16)
    for n in range(tq // WIN):
        r0 = n * WIN
        lhs = jnp.concatenate([qb[r0:r0 + WIN, g * KVW:(g + 1) * KVW] for g in range(G)], axis=0)
        ps = []
        for kv in range(NKV):
            s = _dot_nt(lhs, kbuf[kv, r0:r0 + 2 * WIN, :])
            if n == 0:
                s = s + jnp.where(t == 0, bias_ref[0, kv], bias_ref[1, kv])
            else:
                s = s + bias_ref[1, kv]
            sink = _sink_col(sinks_ref, kv, WIN)
            mx = jnp.maximum(jnp.max(s, axis=-1, keepdims=True), sink)
            p = jnp.exp(s - mx)
            den = jnp.sum(p, axis=-1, keepdims=True) + jnp.exp(sink - mx)
            ps.append((p * (1.0 / den)).astype(bf16))
        pcat = jnp.concatenate(ps, axis=-1)
        vcat = jnp.concatenate([vbuf[kv, r0:r0 + 2 * WIN, :] for kv in range(NKV)], axis=0)
        pv = _dot(pcat, vcat)
        obuf[r0:r0 + WIN, :] = jnp.concatenate(
            [pv[g * WIN:(g + 1) * WIN, :] for g in range(G)], axis=-1).astype(bf16)

    x3 = x + _dot(obuf[...], wo_ref[...])
    x3_ref[0] = x3
    _moe_prep(x3, gmoe_ref, wr_ref, h_ref, route_ref, route_t_ref)


def _attn_prompt(x2, sinks, bias, wts):
    b, t, _ = x2.shape
    tq = min(TQ_ATT, t)
    assert t % tq == 0 and tq % WIN == 0
    nt = t // tq
    gkv, wkv, gk, gqa, wq, gq, wo, gmoe, wr = wts
    row = lambda n: _full((1, n))
    tok = lambda w: pl.BlockSpec((tq, w), lambda i, j: (i * nt + j, 0))
    last = pl.BlockSpec((1, WIN, KVW), lambda i, j: (i, 0, 0))
    return pl.pallas_call(
        _attn_prompt_kernel,
        grid=(b, nt),
        in_specs=[pl.BlockSpec(memory_space=pltpu.SMEM),
                  pl.BlockSpec((1, tq, D), lambda i, j: (i, j, 0)),
                  row(D), _full((D, 2 * KVW)), row(KVW), row(D), _full((D, D)), row(D),
                  _full(bias.shape), _full((D, D)), row(D), _full((D, LANES))],
        out_specs=[pl.BlockSpec((1, tq, D), lambda i, j: (i, j, 0)), tok(D), tok(LANES),
                   pl.BlockSpec((8, tq), lambda i, j: (0, i * nt + j)), last, last],
        out_shape=[jax.ShapeDtypeStruct((b, t, D), f32),
                   jax.ShapeDtypeStruct((b * t, D), f32),
                   jax.ShapeDtypeStruct((b * t, LANES), f32),
                   jax.ShapeDtypeStruct((8, b * t), f32),
                   jax.ShapeDtypeStruct((b, WIN, KVW), f32),
                   jax.ShapeDtypeStruct((b, WIN, KVW), f32)],
        scratch_shapes=[pltpu.VMEM((NKV, tq + WIN, KVW), bf16),
                        pltpu.VMEM((NKV, tq + WIN, KVW), bf16),
                        pltpu.VMEM((tq, D), bf16)],
        compiler_params=_cp("arbitrary", "arbitrary"),
        name="attn_prompt",
    )(sinks, x2, gkv, wkv, gk, gqa, wq, gq, bias, wo, gmoe, wr)


def _attn_sample_kernel(sinks_ref, x_ref, kc_ref, vc_ref, gkv_ref, wkv_ref, gk_ref, gqa_ref, wq_ref,
                        gq_ref, bias_c_ref, bias_n_ref, wo_ref, gmoe_ref, wr_ref,
                        x3_ref, h_ref, route_ref, route_t_ref, knew_ref, vnew_ref,
                        qbuf, obuf, *, bd):
    st = kc_ref.shape[0]
    s_len = x_ref.shape[0] // bd
    rows = s_len * st
    i = pl.program_id(0)

    @pl.when(i == 0)
    def _():
        q, k, v = _qkv(x_ref[...], gkv_ref, wkv_ref, gk_ref, gqa_ref, wq_ref, gq_ref)
        knew_ref[...] = k
        vnew_ref[...] = v
        qbuf[...] = q

    def tile_rows(buf):
        return jnp.concatenate(
            [buf[pl.ds(pl.multiple_of(t * bd + i * st, 8), st), :] for t in range(s_len)],
            axis=0).astype(bf16)

    kbuf, vbuf = knew_ref, vnew_ref

    qt = tile_rows(qbuf)
    lhs = jnp.concatenate([qt[:, g * KVW:(g + 1) * KVW] for g in range(G)], axis=0)
    pad = jnp.zeros((LANES - rows, KVW), bf16)
    kn = jnp.concatenate([tile_rows(kbuf), pad], axis=0)
    vn = jnp.concatenate([tile_rows(vbuf), pad], axis=0)
    kc = kc_ref[...].reshape(st * WIN, KVW).astype(bf16)
    vc = vc_ref[...].reshape(st * WIN, KVW).astype(bf16)
    pv = jnp.zeros((G * rows, KVW), f32)
    for kv in range(NKV):
        m = _kv_lane_mask(kv)
        zc = jnp.zeros_like(kc)
        zn = jnp.zeros_like(kn)
        b0 = kv * G * rows
        s_c = _dot_nt(lhs, jnp.where(m, kc, zc)) + bias_c_ref[b0:b0 + G * rows, :]
        s_n = _dot_nt(lhs, jnp.where(m, kn, zn)) + bias_n_ref[b0:b0 + G * rows, :]
        sink = _sink_col(sinks_ref, kv, rows)
        mx = jnp.maximum(jnp.maximum(jnp.max(s_c, axis=-1, keepdims=True),
                                     jnp.max(s_n, axis=-1, keepdims=True)), sink)
        p_c = jnp.exp(s_c - mx)
        p_n = jnp.exp(s_n - mx)
        den = (jnp.sum(p_c, axis=-1, keepdims=True) + jnp.sum(p_n, axis=-1, keepdims=True)
               + jnp.exp(sink - mx))
        inv = 1.0 / den
        pv = pv + _dot((p_c * inv).astype(bf16), jnp.where(m, vc, zc))
        pv = pv + _dot((p_n * inv).astype(bf16), jnp.where(m, vn, zn))
    o = jnp.concatenate([pv[g * rows:(g + 1) * rows, :] for g in range(G)], axis=-1)
    for t in range(s_len):
        obuf[pl.ds(pl.multiple_of(t * bd + i * st, 8), st), :] = o[t * st:(t + 1) * st, :]

    @pl.when(i == pl.num_programs(0) - 1)
    def _():
        x3 = x_ref[...] + _dot(obuf[...].astype(bf16), wo_ref[...])
        x3_ref[...] = x3
        _moe_prep(x3, gmoe_ref, wr_ref, h_ref, route_ref, route_t_ref)


def _attn_sample(x2, kc, vc, sinks, bias_c, bias_n, wts):
    n_tok = x2.shape[0]
    bd = kc.shape[0]
    st = SEQ_TILE
    assert bd % st == 0 and n_tok % bd == 0 and (n_tok // bd) * st <= LANES
    gkv, wkv, gk, gqa, wq, gq, wo, gmoe, wr = wts
    row = lambda n: _full((1, n))
    cache = pl.BlockSpec((st, WIN, KVW), lambda i: (i, 0, 0))
    return pl.pallas_call(
        functools.partial(_attn_sample_kernel, bd=bd),
        grid=(bd // st,),
        in_specs=[pl.BlockSpec(memory_space=pltpu.SMEM), _full((n_tok, D)), cache, cache,
                  row(D), _full((D, 2 * KVW)), row(KVW), row(D), _full((D, D)), row(D),
                  _full(bias_c.shape), _full(bias_n.shape), _full((D, D)), row(D), _full((D, LANES))],
        out_specs=[_full((n_tok, D)), _full((n_tok, D)), _full((n_tok, LANES)), _full((8, n_tok)),
                   _full((n_tok, KVW)), _full((n_tok, KVW))],
        out_shape=[jax.ShapeDtypeStruct((n_tok, D), f32),
                   jax.ShapeDtypeStruct((n_tok, D), f32),
                   jax.ShapeDtypeStruct((n_tok, LANES), f32),
                   jax.ShapeDtypeStruct((8, n_tok), f32),
                   jax.ShapeDtypeStruct((n_tok, KVW), f32),
                   jax.ShapeDtypeStruct((n_tok, KVW), f32)],
        scratch_shapes=[pltpu.VMEM((n_tok, D), f32), pltpu.VMEM((n_tok, D), f32)],
        compiler_params=_cp("arbitrary"),
        name="attn_sample",
    )(sinks, x2, kc, vc, gkv, wkv, gk, gqa, wq, gq, bias_c, bias_n, wo, gmoe, wr)


def _positions_kernel(rt_ref, pos_ref, off_ref, counts, running, offs):
    ph = pl.program_id(0)
    i = pl.program_id(1)
    tok = rt_ref.shape[1]
    e0 = rt_ref[0:1, :]
    e1 = rt_ref[1:2, :]
    sub = lax.broadcasted_iota(jnp.int32, (NE, tok), 0).astype(f32)
    sel = ((sub == e0) | (sub == e1)).astype(f32)
    tile_cnt = jnp.broadcast_to(jnp.sum(sel, axis=-1, keepdims=True), (NE, LANES))

    @pl.when((ph == 0) & (i == 0))
    def _():
        counts[...] = jnp.zeros((NE, LANES), f32)

    @pl.when(ph == 0)
    def _():
        counts[...] += tile_cnt

    @pl.when((ph == 1) & (i == 0))
    def _():
        sub_l = lax.broadcasted_iota(jnp.int32, (NE, LANES), 0)
        acc = jnp.zeros((NE, LANES), f32)
        for e in range(NE - 1):
            acc = acc + jnp.where(sub_l > e, counts[e:e + 1, :], 0.0)
        offs[...] = acc
        running[...] = jnp.zeros((NE, LANES), f32)
        off_ref[...] = acc.astype(jnp.int32)

    @pl.when(ph == 1)
    def _():
        r_i = lax.broadcasted_iota(jnp.int32, (tok, tok), 0)
        c_i = lax.broadcasted_iota(jnp.int32, (tok, tok), 1)
        upper = (r_i < c_i).astype(bf16)
        cum = _dot(sel.astype(bf16), upper)
        base = offs[:, 0:1] + running[:, 0:1]
        tot = cum + base
        p0 = jnp.sum(jnp.where(sub == e0, tot, 0.0), axis=0, keepdims=True)
        p1 = jnp.sum(jnp.where(sub == e1, tot, 0.0), axis=0, keepdims=True)
        pos_ref[0] = jnp.concatenate([p0, p1], axis=0).astype(jnp.int32)
        running[...] += tile_cnt


def _positions(route_t):
    n = route_t.shape[1]
    assert n % TOK == 0
    nt = n // TOK
    return pl.pallas_call(
        _positions_kernel,
        grid=(2, nt),
        in_specs=[pl.BlockSpec((8, TOK), lambda p, i: (0, i))],
        out_specs=[pl.BlockSpec((1, 2, TOK), lambda p, i: (i * p, 0, 0)), _full((NE, LANES))],
        out_shape=[jax.ShapeDtypeStruct((nt, 2, TOK), jnp.int32),
                   jax.ShapeDtypeStruct((NE, LANES), jnp.int32)],
        scratch_shapes=[pltpu.VMEM((NE, LANES), f32)] * 3,
        compiler_params=_cp("arbitrary", "arbitrary"),
        name="moe_positions",
    )(route_t)


def _row_copy(src, src_row, dst, dst_row, sem):
    return pltpu.make_async_copy(src.at[pl.ds(src_row, 1)], dst.at[pl.ds(dst_row, 1)], sem)


def _dispatch_kernel(pos_ref, hp_ref, hs_ref, xs_ref, sem, *, n_p_tiles):
    i = pl.program_id(0)
    tok = hp_ref.shape[0]

    def scatter_rows(h_ref):
        def issue(r, c):
            _row_copy(h_ref, r, xs_ref, pos_ref[0, 0, r], sem).start()
            _row_copy(h_ref, r, xs_ref, pos_ref[0, 1, r], sem).start()
            return c

        lax.fori_loop(0, tok, issue, 0)

        def drain(r, c):
            _row_copy(h_ref, 0, xs_ref, 0, sem).wait()
            _row_copy(h_ref, 0, xs_ref, 0, sem).wait()
            return c

        lax.fori_loop(0, tok, drain, 0)

    @pl.when(i < n_p_tiles)
    def _():
        scatter_rows(hp_ref)

    @pl.when(i >= n_p_tiles)
    def _():
        scatter_rows(hs_ref)


def _dispatch(pos, hp, hs):
    n_p, n_s = hp.shape[0], hs.shape[0]
    assert n_p % TOK == 0 and n_s % TOK == 0
    npt, nst = n_p // TOK, n_s // TOK
    return pl.pallas_call(
        functools.partial(_dispatch_kernel, n_p_tiles=npt),
        grid=(npt + nst,),
        in_specs=[pl.BlockSpec((1, 2, TOK), lambda i: (i, 0, 0), memory_space=pltpu.SMEM),
                  pl.BlockSpec((TOK, D), lambda i: (jnp.minimum(i, npt - 1), 0)),
                  pl.BlockSpec((TOK, D), lambda i: (jnp.maximum(i - npt, 0), 0))],
        out_specs=pl.BlockSpec(memory_space=pl.ANY),
        out_shape=jax.ShapeDtypeStruct((2 * (n_p + n_s), D), f32),
        scratch_shapes=[pltpu.SemaphoreType.DMA(())],
        compiler_params=_cp("arbitrary"),
        name="moe_dispatch",
    )(pos, hp, hs)


def _experts_kernel(tile_ref, exp_ref, lo_ref, hi_ref, xs_ref, wg_ref, wu_ref, wd_ref, o_ref, xb, acc):
    del tile_ref, exp_ref
    k = pl.program_id(0)
    j = pl.program_id(1)
    lo = lo_ref[k]
    hi = hi_ref[k]

    @pl.when(hi > lo)
    def _():
        @pl.when(j == 0)
        def _():
            xb[...] = xs_ref[...].astype(bf16)

        x = xb[...]
        a = _dot(x, wg_ref[0])
        a = a * _sigmoid(a) * _dot(x, wu_ref[0])
        d = _dot(a.astype(bf16), wd_ref[0])

        @pl.when(j == 0)
        def _():
            acc[...] = d

        @pl.when(j > 0)
        def _():
            acc[...] += d

        @pl.when(j == pl.num_programs(1) - 1)
        def _():
            @pl.when(lo == 0)
            def _():
                o_ref[...] = acc[...]

            @pl.when(lo > 0)
            def _():
                row = lax.broadcasted_iota(jnp.int32, (o_ref.shape[0], 1), 0)
                o_ref[...] = jnp.where((row >= lo) & (row < hi), acc[...], o_ref[...])


def _experts(items, xs, wg, wu, wd):
    n_rows = xs.shape[0]
    ff = wg.shape[2]
    tm = TM_MOE
    assert n_rows % tm == 0 and ff % TF_MOE == 0
    n_items = items[0].shape[0]
    grid_spec = pltpu.PrefetchScalarGridSpec(
        num_scalar_prefetch=4,
        grid=(n_items, ff // TF_MOE),
        in_specs=[pl.BlockSpec((tm, D), lambda k, j, ti, ex, lo, hi: (ti[k], 0)),
                  pl.BlockSpec((1, D, TF_MOE), lambda k, j, ti, ex, lo, hi: (ex[k], 0, j)),
                  pl.BlockSpec((1, D, TF_MOE), lambda k, j, ti, ex, lo, hi: (ex[k], 0, j)),
                  pl.BlockSpec((1, TF_MOE, D), lambda k, j, ti, ex, lo, hi: (ex[k], j, 0))],
        out_specs=pl.BlockSpec((tm, D), lambda k, j, ti, ex, lo, hi: (ti[k], 0)),
        scratch_shapes=[pltpu.VMEM((tm, D), bf16), pltpu.VMEM((tm, D), f32)])
    return pl.pallas_call(
        _experts_kernel,
        grid_spec=grid_spec,
        out_shape=jax.ShapeDtypeStruct((n_rows, D), f32),
        compiler_params=_cp("arbitrary", "arbitrary"),
        name="moe_experts",
    )(*items, xs, wg, wu, wd)


def _work_items(off, n_rows):
    tm = TM_MOE
    n_tiles = n_rows // tm
    n_items = n_tiles + NE - 1
    start = off
    end = jnp.concatenate([off[1:], jnp.array([n_rows], jnp.int32)])
    cnt = end - start
    first = start // tm
    last = jnp.where(cnt > 0, (end - 1) // tm, first - 1)
    per = last - first + 1
    cum = jnp.cumsum(per)
    k = jnp.arange(n_items, dtype=jnp.int32)
    e = jnp.minimum(jnp.searchsorted(cum, k, side="right"), NE - 1).astype(jnp.int32)
    tile = first[e] + (k - (cum[e] - per[e]))
    real = k < cum[NE - 1]
    tile = jnp.where(real, tile, n_tiles - 1).astype(jnp.int32)
    lo = jnp.clip(start[e] - tile * tm, 0, tm)
    hi = jnp.clip(end[e] - tile * tm, 0, tm)
    lo = jnp.where(real, lo, 0).astype(jnp.int32)
    hi = jnp.where(real, hi, 0).astype(jnp.int32)
    return tile, e, lo, hi


def _combine_kernel(pos_ref, x3_ref, route_ref, o_hbm, y_ref, buf, sem):
    tok = x3_ref.shape[0]

    def issue(r, c):
        _row_copy(o_hbm, pos_ref[0, 0, r], buf.at[0], r, sem).start()
        _row_copy(o_hbm, pos_ref[0, 1, r], buf.at[1], r, sem).start()
        return c

    lax.fori_loop(0, tok, issue, 0)

    def drain(r, c):
        _row_copy(o_hbm, 0, buf.at[0], 0, sem).wait()
        _row_copy(o_hbm, 0, buf.at[1], 0, sem).wait()
        return c

    lax.fori_loop(0, tok, drain, 0)
    route = route_ref[...]
    y_ref[...] = x3_ref[...] + route[:, 2:3] * buf[0] + route[:, 3:4] * buf[1]


def _combine(pos, x3, route, o_sorted, tile0):
    n = x3.shape[0]
    assert n % TOK == 0
    return pl.pallas_call(
        _combine_kernel,
        grid=(n // TOK,),
        in_specs=[pl.BlockSpec((1, 2, TOK), lambda i: (i + tile0, 0, 0), memory_space=pltpu.SMEM),
                  pl.BlockSpec((TOK, D), lambda i: (i, 0)),
                  pl.BlockSpec((TOK, LANES), lambda i: (i, 0)),
                  pl.BlockSpec(memory_space=pl.ANY)],
        out_specs=pl.BlockSpec((TOK, D), lambda i: (i, 0)),
        out_shape=jax.ShapeDtypeStruct((n, D), f32),
        scratch_shapes=[pltpu.VMEM((2, TOK, D), f32), pltpu.SemaphoreType.DMA(())],
        compiler_params=_cp("arbitrary"),
        name="moe_combine",
    )(pos, x3, route, o_sorted)


def kernel(x_prompt, x_sample, state_conv, cache_k_win, cache_v_win, g_conv_norm, w_pw1, b_pw1, w_dw, b_dw, g_ln, b_ln, w_pw2, b_pw2, g_kv_norm, w_kv, g_k_norm, g_attn_norm, w_q, g_q_norm, sinks, w_o, rel_bias, g_ffn_norm, w_gate, w_up, w_down, g_moe_norm, w_router, w_e_gate, w_e_up, w_e_down):
    b, t, _ = x_prompt.shape
    bd, s_len, _ = x_sample.shape
    n_p = b * t
    n_s = bd * s_len
    row = lambda a: a.reshape(1, -1).astype(f32)

    conv_w = (row(g_conv_norm[0]), w_pw1[0].astype(bf16), row(b_pw1[0]),
              jnp.pad(w_dw[0], ((0, HALO - CONV_W), (0, 0))), row(b_dw[0]), row(g_ln[0]), row(b_ln[0]),
              w_pw2[0].astype(bf16), row(b_pw2[0]))
    ffn_w = (row(g_ffn_norm[0]), w_gate[0].astype(bf16), w_up[0].astype(bf16), w_down[0].astype(bf16))
    wq = w_q[0].reshape(D, NKV, G, DH).transpose(0, 2, 1, 3).reshape(D, NH * DH).astype(bf16)
    wo = w_o[0].reshape(NKV, G, DH, D).transpose(1, 0, 2, 3).reshape(NH * DH, D).astype(bf16)
    wr = jnp.pad(w_router[0], ((0, 0), (0, LANES - NE))).astype(bf16)
    attn_w = (row(g_kv_norm), w_kv.astype(bf16), row(jnp.tile(g_k_norm, NKV)), row(g_attn_norm[0]),
              wq, row(jnp.tile(g_q_norm[0], NH)), wo, row(g_moe_norm[0]), wr)
    weg, weu, wed = w_e_gate[0].astype(bf16), w_e_up[0].astype(bf16), w_e_down[0].astype(bf16)
    sink = sinks[0].astype(f32)

    bias_p = _bias_table(rel_bias, _prompt_bucket_map()).reshape(2, NKV, G * WIN, 2 * WIN)
    map_c, map_n = _sample_bucket_maps(s_len, SEQ_TILE)
    rows = s_len * SEQ_TILE
    bias_c = _bias_table(rel_bias, map_c).reshape(NH * rows, SEQ_TILE * WIN)
    bias_n = _bias_table(rel_bias, map_n).reshape(NH * rows, LANES)

    x1p, ulast = _conv_prompt(x_prompt, conv_w)
    x2p = _ffn(x1p.reshape(n_p, D), *ffn_w).reshape(b, t, D)
    x3p, hp, route_p, route_tp, klast, vlast = _attn_prompt(x2p, sink, bias_p, attn_w)

    xs_tb = x_sample.transpose(1, 0, 2)
    x1s, u_s = _conv_sample(xs_tb, state_conv[0].transpose(1, 0, 2), conv_w)
    x2s = _ffn(x1s.reshape(n_s, D), *ffn_w)
    kc = cache_k_win.reshape(bd, WIN, KVW)
    vc = cache_v_win.reshape(bd, WIN, KVW)
    x3s, hs, route_s, route_ts, knew, vnew = _attn_sample(x2s, kc, vc, sink, bias_c, bias_n, attn_w)

    pos, off = _positions(jnp.concatenate([route_tp, route_ts], axis=1))
    n_rows = 2 * (n_p + n_s)
    xs = _dispatch(pos, hp, hs)
    o_sorted = _experts(_work_items(off[:, 0], n_rows), xs, weg, weu, wed)
    y_p = _combine(pos, x3p.reshape(n_p, D), route_p, o_sorted, 0).reshape(b, t, D)
    y_s = _combine(pos, x3s, route_s, o_sorted, n_p // TOK).reshape(s_len, bd, D).transpose(1, 0, 2)

    n_prev = CONV_W - 1
    conv_p = ulast[:, HALO - n_prev:, :][None]
    conv_s = jnp.concatenate([state_conv[:, :, s_len:, :], u_s.transpose(1, 0, 2)[None]], axis=2)
    k_p = klast.reshape(b, WIN, NKV, DH)
    v_p = vlast.reshape(b, WIN, NKV, DH)
    k_new = knew.reshape(s_len, bd, NKV, DH).transpose(1, 0, 2, 3)
    v_new = vnew.reshape(s_len, bd, NKV, DH).transpose(1, 0, 2, 3)
    k_s = jnp.concatenate([cache_k_win[:, s_len:], k_new], axis=1)
    v_s = jnp.concatenate([cache_v_win[:, s_len:], v_new], axis=1)
    return (y_p, y_s, conv_p, conv_s, k_p, k_s, v_p, v_s)
```

```python
import functools

import numpy as np
import jax
import jax.numpy as jnp
from jax import lax
from jax.experimental import pallas as pl
from jax.experimental.pallas import tpu as pltpu

D = 1024
CONV_W = 31
HALO = 32
DH = 64
NH = 16
NKV = 4
G = NH // NKV
KVW = NKV * DH
WIN = 128
NB = 32
MAXD = 128
NE = 8
EPS = 1e-6
LANES = 128
NEG = float("-inf")

TT_CONV = 512
TM_FFN = 512
TQ_ATT = 512
TOK = 512
TM_MOE = 1024
TF_MOE = 512
MOE_SUB = 256
ROW_DMA_UNROLL = 8
SEQ_TILE = 8
SEQ_TILE_CONV = 32
FF_CHUNK = 1024
VMEM_LIMIT = 56 * 1024 * 1024

bf16 = jnp.bfloat16
f32 = jnp.float32


def _cp(*sem):
    return pltpu.CompilerParams(dimension_semantics=sem, vmem_limit_bytes=VMEM_LIMIT)


def _full(shape):
    n = len(shape)
    return pl.BlockSpec(shape, lambda *_: (0,) * n)


def _sigmoid(x):
    return 1.0 / (1.0 + jnp.exp(-x))


def _rms(x):
    return x * lax.rsqrt(jnp.mean(x * x, axis=-1, keepdims=True) + EPS)


def _dot(a, b):
    return jnp.dot(a, b, preferred_element_type=f32)


def _dot_nt(a, b):
    return lax.dot_general(a, b, (((1,), (1,)), ((), ())), preferred_element_type=f32)


def _head_rms(x, gain):
    lo = lax.broadcasted_iota(jnp.int32, (1, LANES), 1) < DH
    cols = []
    for c in range(x.shape[1] // LANES):
        col = x[:, c * LANES:(c + 1) * LANES]
        sq = col * col
        tot = jnp.sum(sq, axis=-1, keepdims=True)
        ev = jnp.sum(jnp.where(lo, sq, 0.0), axis=-1, keepdims=True)
        od = tot - ev
        r = jnp.where(lo, lax.rsqrt(ev * (1.0 / DH) + EPS), lax.rsqrt(od * (1.0 / DH) + EPS))
        cols.append(col * r)
    return jnp.concatenate(cols, axis=-1) * gain


def _kv_lane_mask(kv):
    lane = lax.broadcasted_iota(jnp.int32, (1, KVW), 1)
    return (lane >= kv * DH) & (lane < (kv + 1) * DH)


def _bucket_of(dist):
    n = np.maximum(dist, 0)
    max_exact = NB // 2
    large = max_exact + (np.log(np.maximum(n, 1) / max_exact) / np.log(MAXD / max_exact)
                         * (NB - max_exact)).astype(np.int32)
    large = np.minimum(large, NB - 1)
    return np.where(n < max_exact, n, large).astype(np.int32)


def _prompt_bucket_map():
    dist = np.arange(WIN)[:, None] + WIN - np.arange(2 * WIN)[None, :]
    valid = (dist >= 0) & (dist < WIN)
    bk = np.where(valid, _bucket_of(dist), -1).astype(np.int32)
    first = bk.copy()
    first[:, :WIN] = -1
    return np.stack([first, bk])


def _sample_bucket_maps(s_len, st):
    r_t = np.repeat(np.arange(s_len), st)
    r_b = np.tile(np.arange(st), s_len)
    c_b = np.repeat(np.arange(st), WIN)
    c_s = np.tile(np.arange(WIN), st)
    dist = r_t[:, None] + WIN - c_s[None, :]
    valid = (r_b[:, None] == c_b[None, :]) & (dist >= 0) & (dist < WIN)
    map_c = np.where(valid, _bucket_of(dist), -1).astype(np.int32)
    dist_n = r_t[:, None] - r_t[None, :]
    valid_n = (r_b[:, None] == r_b[None, :]) & (dist_n >= 0) & (dist_n < WIN)
    map_n = np.full((s_len * st, LANES), -1, np.int32)
    map_n[:, :s_len * st] = np.where(valid_n, _bucket_of(dist_n), -1)
    return map_c[None], map_n[None]


def _bias_kernel(rb_ref, map_ref, out_ref):
    h = pl.program_id(1)
    bk = map_ref[0]
    acc = jnp.full(bk.shape, NEG, f32)
    for b in range(NB):
        acc = jnp.where(bk == b, rb_ref[b, h], acc)
    out_ref[0, 0] = acc


def _bias_table(rel_bias, bucket_map):
    v, r, c = bucket_map.shape
    return pl.pallas_call(
        _bias_kernel,
        grid=(v, NH),
        in_specs=[pl.BlockSpec(memory_space=pltpu.SMEM),
                  pl.BlockSpec((1, r, c), lambda i, h: (i, 0, 0))],
        out_specs=pl.BlockSpec((1, 1, r, c), lambda i, h: (i, h, 0, 0)),
        out_shape=jax.ShapeDtypeStruct((v, NH, r, c), f32),
        compiler_params=_cp("arbitrary", "arbitrary"),
        name="bias_table",
    )(rel_bias, jnp.asarray(bucket_map))


def _glu_rows(x, gn_ref, w1_ref, b1_ref):
    h = (_rms(x) * gn_ref[...]).astype(bf16)
    u2 = _dot(h, w1_ref[...]) + b1_ref[...]
    return u2[:, :D] * _sigmoid(u2[:, D:])


def _conv_tail(x, c, gln_ref, bln_ref, w2_ref, b2_ref):
    mu = jnp.mean(c, axis=-1, keepdims=True)
    xc = c - mu
    var = jnp.mean(xc * xc, axis=-1, keepdims=True)
    y = xc * lax.rsqrt(var + EPS) * gln_ref[...] + bln_ref[...]
    y = y * _sigmoid(y)
    return x + _dot(y.astype(bf16), w2_ref[...]) + b2_ref[...]


def _conv_prompt_kernel(x_ref, gn_ref, w1_ref, b1_ref, wdw_ref, bdw_ref, gln_ref, bln_ref,
                        w2_ref, b2_ref, x1_ref, ulast_ref, ubuf, ush):
    tt = x_ref.shape[1]
    t = pl.program_id(1)
    x = x_ref[0]
    u = _glu_rows(x, gn_ref, w1_ref, b1_ref)

    @pl.when(t == 0)
    def _():
        ubuf[0:HALO, :] = jnp.zeros((HALO, D), f32)

    @pl.when(t > 0)
    def _():
        ubuf[0:HALO, :] = ubuf[tt:tt + HALO, :]

    ubuf[HALO:HALO + tt, :] = u
    ulast_ref[0] = u[tt - HALO:, :]
    n_sh = ush.shape[1]
    for r in range(1, 8):
        ush[r - 1] = ubuf[r:r + n_sh, :]
    off = HALO - (CONV_W - 1)
    c = jnp.broadcast_to(bdw_ref[...], (tt, D))
    for k in range(CONV_W):
        a, r = divmod(off + k, 8)
        src = ubuf[8 * a:8 * a + tt, :] if r == 0 else ush[r - 1, 8 * a:8 * a + tt, :]
        c = c + wdw_ref[k:k + 1, :] * src
    x1_ref[0] = _conv_tail(x, c, gln_ref, bln_ref, w2_ref, b2_ref)


def _conv_prompt(x, wts):
    b, t, _ = x.shape
    tt = min(TT_CONV, t)
    assert t % tt == 0 and tt >= HALO
    row = lambda n: _full((1, n))
    return pl.pallas_call(
        _conv_prompt_kernel,
        grid=(b, t // tt),
        in_specs=[pl.BlockSpec((1, tt, D), lambda i, j: (i, j, 0)),
                  row(D), _full((D, 2 * D)), row(2 * D), _full((HALO, D)), row(D), row(D), row(D),
                  _full((D, D)), row(D)],
        out_specs=[pl.BlockSpec((1, tt, D), lambda i, j: (i, j, 0)),
                   pl.BlockSpec((1, HALO, D), lambda i, j: (i, 0, 0))],
        out_shape=[jax.ShapeDtypeStruct((b, t, D), f32),
                   jax.ShapeDtypeStruct((b, HALO, D), f32)],
        scratch_shapes=[pltpu.VMEM((tt + HALO, D), f32),
                        pltpu.VMEM((7, tt + HALO - 8, D), f32)],
        compiler_params=_cp("arbitrary", "arbitrary"),
        name="conv_prompt",
    )(x, *wts)


def _conv_sample_kernel(x_ref, st_ref, gn_ref, w1_ref, b1_ref, wdw_ref, bdw_ref, gln_ref, bln_ref,
                        w2_ref, b2_ref, x1_ref, u_ref):
    s_len, bs, _ = x_ref.shape
    x = x_ref[...].reshape(s_len * bs, D)
    u = _glu_rows(x, gn_ref, w1_ref, b1_ref)
    u_ref[...] = u.reshape(s_len, bs, D)
    n_prev = CONV_W - 1
    cs = []
    for t in range(s_len):
        c = jnp.broadcast_to(bdw_ref[...], (bs, D))
        for j in range(t, n_prev):
            c = c + wdw_ref[j - t:j - t + 1, :] * st_ref[j]
        for i in range(t + 1):
            k = n_prev - t + i
            c = c + wdw_ref[k:k + 1, :] * u[i * bs:(i + 1) * bs, :]
        cs.append(c)
    c = jnp.concatenate(cs, axis=0)
    x1_ref[...] = _conv_tail(x, c, gln_ref, bln_ref, w2_ref, b2_ref).reshape(s_len, bs, D)


def _conv_sample(x_tb, state_t, wts):
    s_len, bd, _ = x_tb.shape
    bs = min(SEQ_TILE_CONV, bd)
    assert bd % bs == 0 and bs % 8 == 0 and s_len <= CONV_W - 1
    row = lambda n: _full((1, n))
    blk = pl.BlockSpec((s_len, bs, D), lambda i: (0, i, 0))
    return pl.pallas_call(
        _conv_sample_kernel,
        grid=(bd // bs,),
        in_specs=[blk, pl.BlockSpec((CONV_W - 1, bs, D), lambda i: (0, i, 0)),
                  row(D), _full((D, 2 * D)), row(2 * D), _full((HALO, D)), row(D), row(D), row(D),
                  _full((D, D)), row(D)],
        out_specs=[blk, blk],
        out_shape=[jax.ShapeDtypeStruct((s_len, bd, D), f32)] * 2,
        compiler_params=_cp("arbitrary"),
        name="conv_sample",
    )(x_tb, state_t, *wts)


def _ffn_kernel(x_ref, g_ref, wg_ref, wu_ref, wd_ref, o_ref):
    x = x_ref[...]
    h = (_rms(x) * g_ref[...]).astype(bf16)
    ff = wg_ref.shape[1]
    acc = x
    for c0 in range(0, ff, FF_CHUNK):
        c1 = min(c0 + FF_CHUNK, ff)
        a = _dot(h, wg_ref[:, c0:c1])
        a = a * _sigmoid(a) * _dot(h, wu_ref[:, c0:c1])
        acc = acc + _dot(a.astype(bf16), wd_ref[c0:c1, :])
    o_ref[...] = acc


def _ffn(x, g, wg, wu, wd):
    n = x.shape[0]
    tm = min(TM_FFN, n)
    assert n % tm == 0
    ff = wg.shape[1]
    return pl.pallas_call(
        _ffn_kernel,
        grid=(n // tm,),
        in_specs=[pl.BlockSpec((tm, D), lambda i: (i, 0)), _full((1, D)),
                  _full((D, ff)), _full((D, ff)), _full((ff, D))],
        out_specs=pl.BlockSpec((tm, D), lambda i: (i, 0)),
        out_shape=jax.ShapeDtypeStruct((n, D), f32),
        compiler_params=_cp("arbitrary"),
        name="dense_ffn",
    )(x, g, wg, wu, wd)


def _qkv(x, gkv_ref, wkv_ref, gk_ref, gq_attn_ref, wq_ref, gq_ref):
    xn = _rms(x)
    kv = _dot((xn * gkv_ref[...]).astype(bf16), wkv_ref[...])
    k = _head_rms(kv[:, :KVW], gk_ref[...])
    v = kv[:, KVW:]
    q = _dot((xn * gq_attn_ref[...]).astype(bf16), wq_ref[...])
    q = _head_rms(q, gq_ref[...]) * (DH ** -0.5)
    return q, k, v


def _sink_col(sinks_ref, kv, rows_per_head):
    return jnp.concatenate(
        [jnp.full((rows_per_head, 1), sinks_ref[kv * G + g], f32) for g in range(G)], axis=0)


def _moe_prep(x3, gmoe_ref, wr_ref, h_ref, route_ref, route_t_ref):
    h = _rms(x3) * gmoe_ref[...]
    h_ref[...] = h
    logits = _dot(h.astype(bf16), wr_ref[...])
    lane = lax.broadcasted_iota(jnp.int32, logits.shape, 1).astype(f32)
    logits = jnp.where(lane < NE, logits, NEG)
    m1 = jnp.max(logits, axis=-1, keepdims=True)
    i1 = jnp.min(jnp.where(logits == m1, lane, float(LANES)), axis=-1, keepdims=True)
    rest = jnp.where(lane == i1, NEG, logits)
    m2 = jnp.max(rest, axis=-1, keepdims=True)
    i2 = jnp.min(jnp.where(rest == m2, lane, float(LANES)), axis=-1, keepdims=True)
    e2 = jnp.exp(m2 - m1)
    den = 1.0 + e2
    g1 = 1.0 / den
    g2 = e2 / den
    route = jnp.where(lane == 0, i1,
                      jnp.where(lane == 1, i2,
                                jnp.where(lane == 2, g1, jnp.where(lane == 3, g2, 0.0))))
    route_ref[...] = route
    route_t_ref[...] = route.T[0:8, :]


def _attn_prompt_kernel(sinks_ref, x_ref, gkv_ref, wkv_ref, gk_ref, gqa_ref, wq_ref, gq_ref,
                        bias_ref, wo_ref, gmoe_ref, wr_ref,
                        x3_ref, h_ref, route_ref, route_t_ref, klast_ref, vlast_ref,
                        kbuf, vbuf, obuf):
    tq = x_ref.shape[1]
    t = pl.program_id(1)
    x = x_ref[0]
    q, k, v = _qkv(x, gkv_ref, wkv_ref, gk_ref, gqa_ref, wq_ref, gq_ref)
    klast_ref[0] = k[tq - WIN:, :]
    vlast_ref[0] = v[tq - WIN:, :]

    @pl.when(t == 0)
    def _():
        kbuf[:, 0:WIN, :] = jnp.zeros((NKV, WIN, KVW), bf16)
        vbuf[:, 0:WIN, :] = jnp.zeros((NKV, WIN, KVW), bf16)

    @pl.when(t > 0)
    def _():
        kbuf[:, 0:WIN, :] = kbuf[:, tq:tq + WIN, :]
        vbuf[:, 0:WIN, :] = vbuf[:, tq:tq + WIN, :]

    kb = k.astype(bf16)
    vb = v.astype(bf16)
    for kv in range(NKV):
        m = _kv_lane_mask(kv)
        kbuf[kv, WIN:WIN + tq, :] = jnp.where(m, kb, jnp.zeros_like(kb))
        vbuf[kv, WIN:WIN + tq, :] = jnp.where(m, vb, jnp.zeros_like(vb))

    qb = q.astype(bf16)
    for n in range(tq // WIN):
        r0 = n * WIN
        lhs = jnp.concatenate([qb[r0:r0 + WIN, g * KVW:(g + 1) * KVW] for g in range(G)], axis=0)
        ps = []
        for kv in range(NKV):
            s = _dot_nt(lhs, kbuf[kv, r0:r0 + 2 * WIN, :])
            if n == 0:
                s = s + jnp.where(t == 0, bias_ref[0, kv], bias_ref[1, kv])
            else:
                s = s + bias_ref[1, kv]
            sink = _sink_col(sinks_ref, kv, WIN)
            mx = jnp.maximum(jnp.max(s, axis=-1, keepdims=True), sink)
            p = jnp.exp(s - mx)
            den = jnp.sum(p, axis=-1, keepdims=True) + jnp.exp(sink - mx)
            ps.append((p * (1.0 / den)).astype(bf16))
        pcat = jnp.concatenate(ps, axis=-1)
        vcat = jnp.concatenate([vbuf[kv, r0:r0 + 2 * WIN, :] for kv in range(NKV)], axis=0)
        pv = _dot(pcat, vcat)
        obuf[r0:r0 + WIN, :] = jnp.concatenate(
            [pv[g * WIN:(g + 1) * WIN, :] for g in range(G)], axis=-1).astype(bf16)

    x3 = x + _dot(obuf[...], wo_ref[...])
    x3_ref[0] = x3
    _moe_prep(x3, gmoe_ref, wr_ref, h_ref, route_ref, route_t_ref)


def _attn_prompt(x2, sinks, bias, wts):
    b, t, _ = x2.shape
    tq = min(TQ_ATT, t)
    assert t % tq == 0 and tq % WIN == 0
    nt = t // tq
    gkv, wkv, gk, gqa, wq, gq, wo, gmoe, wr = wts
    row = lambda n: _full((1, n))
    tok = lambda w: pl.BlockSpec((tq, w), lambda i, j: (i * nt + j, 0))
    last = pl.BlockSpec((1, WIN, KVW), lambda i, j: (i, 0, 0))
    return pl.pallas_call(
        _attn_prompt_kernel,
        grid=(b, nt),
        in_specs=[pl.BlockSpec(memory_space=pltpu.SMEM),
                  pl.BlockSpec((1, tq, D), lambda i, j: (i, j, 0)),
                  row(D), _full((D, 2 * KVW)), row(KVW), row(D), _full((D, D)), row(D),
                  _full(bias.shape), _full((D, D)), row(D), _full((D, LANES))],
        out_specs=[pl.BlockSpec((1, tq, D), lambda i, j: (i, j, 0)), tok(D), tok(LANES),
                   pl.BlockSpec((8, tq), lambda i, j: (0, i * nt + j)), last, last],
        out_shape=[jax.ShapeDtypeStruct((b, t, D), f32),
                   jax.ShapeDtypeStruct((b * t, D), f32),
                   jax.ShapeDtypeStruct((b * t, LANES), f32),
                   jax.ShapeDtypeStruct((8, b * t), f32),
                   jax.ShapeDtypeStruct((b, WIN, KVW), f32),
                   jax.ShapeDtypeStruct((b, WIN, KVW), f32)],
        scratch_shapes=[pltpu.VMEM((NKV, tq + WIN, KVW), bf16),
                        pltpu.VMEM((NKV, tq + WIN, KVW), bf16),
                        pltpu.VMEM((tq, D), bf16)],
        compiler_params=_cp("arbitrary", "arbitrary"),
        name="attn_prompt",
    )(sinks, x2, gkv, wkv, gk, gqa, wq, gq, bias, wo, gmoe, wr)


def _attn_sample_kernel(sinks_ref, x_ref, kc_ref, vc_ref, gkv_ref, wkv_ref, gk_ref, gqa_ref, wq_ref,
                        gq_ref, bias_c_ref, bias_n_ref, wo_ref, gmoe_ref, wr_ref,
                        x3_ref, h_ref, route_ref, route_t_ref, knew_ref, vnew_ref,
                        qbuf, obuf, *, bd):
    st = kc_ref.shape[0]
    s_len = x_ref.shape[0] // bd
    rows = s_len * st
    i = pl.program_id(0)

    @pl.when(i == 0)
    def _():
        q, k, v = _qkv(x_ref[...], gkv_ref, wkv_ref, gk_ref, gqa_ref, wq_ref, gq_ref)
        knew_ref[...] = k
        vnew_ref[...] = v
        qbuf[...] = q

    def tile_rows(buf):
        return jnp.concatenate(
            [buf[pl.ds(pl.multiple_of(t * bd + i * st, 8), st), :] for t in range(s_len)],
            axis=0).astype(bf16)

    kbuf, vbuf = knew_ref, vnew_ref

    qt = tile_rows(qbuf)
    lhs = jnp.concatenate([qt[:, g * KVW:(g + 1) * KVW] for g in range(G)], axis=0)
    pad = jnp.zeros((LANES - rows, KVW), bf16)
    kn = jnp.concatenate([tile_rows(kbuf), pad], axis=0)
    vn = jnp.concatenate([tile_rows(vbuf), pad], axis=0)
    kc = kc_ref[...].reshape(st * WIN, KVW).astype(bf16)
    vc = vc_ref[...].reshape(st * WIN, KVW).astype(bf16)
    pv = jnp.zeros((G * rows, KVW), f32)
    for kv in range(NKV):
        m = _kv_lane_mask(kv)
        zc = jnp.zeros_like(kc)
        zn = jnp.zeros_like(kn)
        b0 = kv * G * rows
        s_c = _dot_nt(lhs, jnp.where(m, kc, zc)) + bias_c_ref[b0:b0 + G * rows, :]
        s_n = _dot_nt(lhs, jnp.where(m, kn, zn)) + bias_n_ref[b0:b0 + G * rows, :]
        sink = _sink_col(sinks_ref, kv, rows)
        mx = jnp.maximum(jnp.maximum(jnp.max(s_c, axis=-1, keepdims=True),
                                     jnp.max(s_n, axis=-1, keepdims=True)), sink)
        p_c = jnp.exp(s_c - mx)
        p_n = jnp.exp(s_n - mx)
        den = (jnp.sum(p_c, axis=-1, keepdims=True) + jnp.sum(p_n, axis=-1, keepdims=True)
               + jnp.exp(sink - mx))
        inv = 1.0 / den
        pv = pv + _dot((p_c * inv).astype(bf16), jnp.where(m, vc, zc))
        pv = pv + _dot((p_n * inv).astype(bf16), jnp.where(m, vn, zn))
    o = jnp.concatenate([pv[g * rows:(g + 1) * rows, :] for g in range(G)], axis=-1)
    for t in range(s_len):
        obuf[pl.ds(pl.multiple_of(t * bd + i * st, 8), st), :] = o[t * st:(t + 1) * st, :]

    @pl.when(i == pl.num_programs(0) - 1)
    def _():
        x3 = x_ref[...] + _dot(obuf[...].astype(bf16), wo_ref[...])
        x3_ref[...] = x3
        _moe_prep(x3, gmoe_ref, wr_ref, h_ref, route_ref, route_t_ref)


def _attn_sample(x2, kc, vc, sinks, bias_c, bias_n, wts):
    n_tok = x2.shape[0]
    bd = kc.shape[0]
    st = SEQ_TILE
    assert bd % st == 0 and n_tok % bd == 0 and (n_tok // bd) * st <= LANES
    gkv, wkv, gk, gqa, wq, gq, wo, gmoe, wr = wts
    row = lambda n: _full((1, n))
    cache = pl.BlockSpec((st, WIN, KVW), lambda i: (i, 0, 0))
    return pl.pallas_call(
        functools.partial(_attn_sample_kernel, bd=bd),
        grid=(bd // st,),
        in_specs=[pl.BlockSpec(memory_space=pltpu.SMEM), _full((n_tok, D)), cache, cache,
                  row(D), _full((D, 2 * KVW)), row(KVW), row(D), _full((D, D)), row(D),
                  _full(bias_c.shape), _full(bias_n.shape), _full((D, D)), row(D), _full((D, LANES))],
        out_specs=[_full((n_tok, D)), _full((n_tok, D)), _full((n_tok, LANES)), _full((8, n_tok)),
                   _full((n_tok, KVW)), _full((n_tok, KVW))],
        out_shape=[jax.ShapeDtypeStruct((n_tok, D), f32),
                   jax.ShapeDtypeStruct((n_tok, D), f32),
                   jax.ShapeDtypeStruct((n_tok, LANES), f32),
                   jax.ShapeDtypeStruct((8, n_tok), f32),
                   jax.ShapeDtypeStruct((n_tok, KVW), f32),
                   jax.ShapeDtypeStruct((n_tok, KVW), f32)],
        scratch_shapes=[pltpu.VMEM((n_tok, D), f32), pltpu.VMEM((n_tok, D), f32)],
        compiler_params=_cp("arbitrary"),
        name="attn_sample",
    )(sinks, x2, kc, vc, gkv, wkv, gk, gqa, wq, gq, bias_c, bias_n, wo, gmoe, wr)


def _positions_kernel(rt_ref, pos_ref, off_ref, counts, running, offs):
    ph = pl.program_id(0)
    i = pl.program_id(1)
    tok = rt_ref.shape[1]
    e0 = rt_ref[0:1, :]
    e1 = rt_ref[1:2, :]
    sub = lax.broadcasted_iota(jnp.int32, (NE, tok), 0).astype(f32)
    sel = ((sub == e0) | (sub == e1)).astype(f32)
    tile_cnt = jnp.broadcast_to(jnp.sum(sel, axis=-1, keepdims=True), (NE, LANES))

    @pl.when((ph == 0) & (i == 0))
    def _():
        counts[...] = jnp.zeros((NE, LANES), f32)

    @pl.when(ph == 0)
    def _():
        counts[...] += tile_cnt

    @pl.when((ph == 1) & (i == 0))
    def _():
        sub_l = lax.broadcasted_iota(jnp.int32, (NE, LANES), 0)
        acc = jnp.zeros((NE, LANES), f32)
        for e in range(NE - 1):
            acc = acc + jnp.where(sub_l > e, counts[e:e + 1, :], 0.0)
        offs[...] = acc
        running[...] = jnp.zeros((NE, LANES), f32)
        off_ref[...] = acc.astype(jnp.int32)

    @pl.when(ph == 1)
    def _():
        r_i = lax.broadcasted_iota(jnp.int32, (tok, tok), 0)
        c_i = lax.broadcasted_iota(jnp.int32, (tok, tok), 1)
        upper = (r_i < c_i).astype(bf16)
        cum = _dot(sel.astype(bf16), upper)
        base = offs[:, 0:1] + running[:, 0:1]
        tot = cum + base
        p0 = jnp.sum(jnp.where(sub == e0, tot, 0.0), axis=0, keepdims=True)
        p1 = jnp.sum(jnp.where(sub == e1, tot, 0.0), axis=0, keepdims=True)
        pos_ref[0] = jnp.concatenate([p0, p1], axis=0).astype(jnp.int32)
        running[...] += tile_cnt


def _positions(route_t):
    n = route_t.shape[1]
    assert n % TOK == 0
    nt = n // TOK
    return pl.pallas_call(
        _positions_kernel,
        grid=(2, nt),
        in_specs=[pl.BlockSpec((8, TOK), lambda p, i: (0, i))],
        out_specs=[pl.BlockSpec((1, 2, TOK), lambda p, i: (i * p, 0, 0)), _full((NE, LANES))],
        out_shape=[jax.ShapeDtypeStruct((nt, 2, TOK), jnp.int32),
                   jax.ShapeDtypeStruct((NE, LANES), jnp.int32)],
        scratch_shapes=[pltpu.VMEM((NE, LANES), f32)] * 3,
        compiler_params=_cp("arbitrary", "arbitrary"),
        name="moe_positions",
    )(route_t)


def _row_copy(src, src_row, dst, dst_row, sem):
    return pltpu.make_async_copy(src.at[pl.ds(src_row, 1)], dst.at[pl.ds(dst_row, 1)], sem)


def _dispatch_kernel(pos_ref, hp_ref, hs_ref, xs_ref, sem, *, n_p_tiles):
    i = pl.program_id(0)
    tok = hp_ref.shape[0]

    def scatter_rows(h_ref):
        def issue(r, c):
            _row_copy(h_ref, r, xs_ref, pos_ref[0, 0, r], sem).start()
            _row_copy(h_ref, r, xs_ref, pos_ref[0, 1, r], sem).start()
            return c

        lax.fori_loop(0, tok, issue, 0, unroll=ROW_DMA_UNROLL)
        for _ in range(2):
            pltpu.make_async_copy(h_ref, xs_ref.at[pl.ds(0, tok)], sem).wait()

    @pl.when(i < n_p_tiles)
    def _():
        scatter_rows(hp_ref)

    @pl.when(i >= n_p_tiles)
    def _():
        scatter_rows(hs_ref)


def _dispatch(pos, hp, hs):
    n_p, n_s = hp.shape[0], hs.shape[0]
    assert n_p % TOK == 0 and n_s % TOK == 0
    npt, nst = n_p // TOK, n_s // TOK
    return pl.pallas_call(
        functools.partial(_dispatch_kernel, n_p_tiles=npt),
        grid=(npt + nst,),
        in_specs=[pl.BlockSpec((1, 2, TOK), lambda i: (i, 0, 0), memory_space=pltpu.SMEM),
                  pl.BlockSpec((TOK, D), lambda i: (jnp.minimum(i, npt - 1), 0)),
                  pl.BlockSpec((TOK, D), lambda i: (jnp.maximum(i - npt, 0), 0))],
        out_specs=pl.BlockSpec(memory_space=pl.ANY),
        out_shape=jax.ShapeDtypeStruct((2 * (n_p + n_s), D), f32),
        scratch_shapes=[pltpu.SemaphoreType.DMA(())],
        compiler_params=_cp("arbitrary"),
        name="moe_dispatch",
    )(pos, hp, hs)


def _experts_kernel(tile_ref, exp_ref, lo_ref, hi_ref, xs_ref, wg_ref, wu_ref, wd_ref, o_ref, xb, acc):
    del tile_ref, exp_ref
    k = pl.program_id(0)
    j = pl.program_id(1)
    lo = lo_ref[k]
    hi = hi_ref[k]

    @pl.when(hi > lo)
    def _():
        @pl.when(j == 0)
        def _():
            xb[...] = xs_ref[...].astype(bf16)
            acc[...] = jnp.zeros(acc.shape, f32)

        for r0 in range(0, xb.shape[0], MOE_SUB):
            x = xb[r0:r0 + MOE_SUB, :]
            a = _dot(x, wg_ref[0])
            a = a * _sigmoid(a) * _dot(x, wu_ref[0])
            d = _dot(a.astype(bf16), wd_ref[0])
            acc[r0:r0 + MOE_SUB, :] += d

        @pl.when(j == pl.num_programs(1) - 1)
        def _():
            @pl.when(lo == 0)
            def _():
                o_ref[...] = acc[...]

            @pl.when(lo > 0)
            def _():
                row = lax.broadcasted_iota(jnp.int32, (o_ref.shape[0], 1), 0)
                o_ref[...] = jnp.where((row >= lo) & (row < hi), acc[...], o_ref[...])


def _experts(items, xs, wg, wu, wd):
    n_rows = xs.shape[0]
    ff = wg.shape[2]
    tm = TM_MOE
    assert n_rows % tm == 0 and ff % TF_MOE == 0
    n_items = items[0].shape[0]
    grid_spec = pltpu.PrefetchScalarGridSpec(
        num_scalar_prefetch=4,
        grid=(n_items, ff // TF_MOE),
        in_specs=[pl.BlockSpec((tm, D), lambda k, j, ti, ex, lo, hi: (ti[k], 0)),
                  pl.BlockSpec((1, D, TF_MOE), lambda k, j, ti, ex, lo, hi: (ex[k], 0, j)),
                  pl.BlockSpec((1, D, TF_MOE), lambda k, j, ti, ex, lo, hi: (ex[k], 0, j)),
                  pl.BlockSpec((1, TF_MOE, D), lambda k, j, ti, ex, lo, hi: (ex[k], j, 0))],
        out_specs=pl.BlockSpec((tm, D), lambda k, j, ti, ex, lo, hi: (ti[k], 0)),
        scratch_shapes=[pltpu.VMEM((tm, D), bf16), pltpu.VMEM((tm, D), f32)])
    return pl.pallas_call(
        _experts_kernel,
        grid_spec=grid_spec,
        out_shape=jax.ShapeDtypeStruct((n_rows, D), f32),
        compiler_params=_cp("arbitrary", "arbitrary"),
        name="moe_experts",
    )(*items, xs, wg, wu, wd)


def _work_items(off, n_rows):
    tm = TM_MOE
    n_tiles = n_rows // tm
    n_items = n_tiles + NE - 1
    start = off
    end = jnp.concatenate([off[1:], jnp.array([n_rows], jnp.int32)])
    cnt = end - start
    first = start // tm
    last = jnp.where(cnt > 0, (end - 1) // tm, first - 1)
    per = last - first + 1
    cum = jnp.cumsum(per)
    k = jnp.arange(n_items, dtype=jnp.int32)
    e = jnp.minimum(jnp.searchsorted(cum, k, side="right"), NE - 1).astype(jnp.int32)
    tile = first[e] + (k - (cum[e] - per[e]))
    real = k < cum[NE - 1]
    tile = jnp.where(real, tile, n_tiles - 1).astype(jnp.int32)
    lo = jnp.clip(start[e] - tile * tm, 0, tm)
    hi = jnp.clip(end[e] - tile * tm, 0, tm)
    lo = jnp.where(real, lo, 0).astype(jnp.int32)
    hi = jnp.where(real, hi, 0).astype(jnp.int32)
    return tile, e, lo, hi


def _combine_kernel(pos_ref, x3_ref, route_ref, o_hbm, y_ref, buf, sem):
    tok = x3_ref.shape[0]

    def issue(r, c):
        _row_copy(o_hbm, pos_ref[0, 0, r], buf.at[0], r, sem).start()
        _row_copy(o_hbm, pos_ref[0, 1, r], buf.at[1], r, sem).start()
        return c

    lax.fori_loop(0, tok, issue, 0, unroll=ROW_DMA_UNROLL)
    for s in range(2):
        pltpu.make_async_copy(o_hbm.at[pl.ds(0, tok)], buf.at[s], sem).wait()
    route = route_ref[...]
    y_ref[...] = x3_ref[...] + route[:, 2:3] * buf[0] + route[:, 3:4] * buf[1]


def _combine(pos, x3, route, o_sorted, tile0):
    n = x3.shape[0]
    assert n % TOK == 0
    return pl.pallas_call(
        _combine_kernel,
        grid=(n // TOK,),
        in_specs=[pl.BlockSpec((1, 2, TOK), lambda i: (i + tile0, 0, 0), memory_space=pltpu.SMEM),
                  pl.BlockSpec((TOK, D), lambda i: (i, 0)),
                  pl.BlockSpec((TOK, LANES), lambda i: (i, 0)),
                  pl.BlockSpec(memory_space=pl.ANY)],
        out_specs=pl.BlockSpec((TOK, D), lambda i: (i, 0)),
        out_shape=jax.ShapeDtypeStruct((n, D), f32),
        scratch_shapes=[pltpu.VMEM((2, TOK, D), f32), pltpu.SemaphoreType.DMA(())],
        compiler_params=_cp("arbitrary"),
        name="moe_combine",
    )(pos, x3, route, o_sorted)


def kernel(x_prompt, x_sample, state_conv, cache_k_win, cache_v_win, g_conv_norm, w_pw1, b_pw1, w_dw, b_dw, g_ln, b_ln, w_pw2, b_pw2, g_kv_norm, w_kv, g_k_norm, g_attn_norm, w_q, g_q_norm, sinks, w_o, rel_bias, g_ffn_norm, w_gate, w_up, w_down, g_moe_norm, w_router, w_e_gate, w_e_up, w_e_down):
    b, t, _ = x_prompt.shape
    bd, s_len, _ = x_sample.shape
    n_p = b * t
    n_s = bd * s_len
    row = lambda a: a.reshape(1, -1).astype(f32)

    conv_w = (row(g_conv_norm[0]), w_pw1[0].astype(bf16), row(b_pw1[0]),
              jnp.pad(w_dw[0], ((0, HALO - CONV_W), (0, 0))), row(b_dw[0]), row(g_ln[0]), row(b_ln[0]),
              w_pw2[0].astype(bf16), row(b_pw2[0]))
    ffn_w = (row(g_ffn_norm[0]), w_gate[0].astype(bf16), w_up[0].astype(bf16), w_down[0].astype(bf16))
    wq = w_q[0].reshape(D, NKV, G, DH).transpose(0, 2, 1, 3).reshape(D, NH * DH).astype(bf16)
    wo = w_o[0].reshape(NKV, G, DH, D).transpose(1, 0, 2, 3).reshape(NH * DH, D).astype(bf16)
    wr = jnp.pad(w_router[0], ((0, 0), (0, LANES - NE))).astype(bf16)
    attn_w = (row(g_kv_norm), w_kv.astype(bf16), row(jnp.tile(g_k_norm, NKV)), row(g_attn_norm[0]),
              wq, row(jnp.tile(g_q_norm[0], NH)), wo, row(g_moe_norm[0]), wr)
    weg, weu, wed = w_e_gate[0].astype(bf16), w_e_up[0].astype(bf16), w_e_down[0].astype(bf16)
    sink = sinks[0].astype(f32)

    bias_p = _bias_table(rel_bias, _prompt_bucket_map()).reshape(2, NKV, G * WIN, 2 * WIN)
    map_c, map_n = _sample_bucket_maps(s_len, SEQ_TILE)
    rows = s_len * SEQ_TILE
    bias_c = _bias_table(rel_bias, map_c).reshape(NH * rows, SEQ_TILE * WIN)
    bias_n = _bias_table(rel_bias, map_n).reshape(NH * rows, LANES)

    x1p, ulast = _conv_prompt(x_prompt, conv_w)
    x2p = _ffn(x1p.reshape(n_p, D), *ffn_w).reshape(b, t, D)
    x3p, hp, route_p, route_tp, klast, vlast = _attn_prompt(x2p, sink, bias_p, attn_w)

    xs_tb = x_sample.transpose(1, 0, 2)
    x1s, u_s = _conv_sample(xs_tb, state_conv[0].transpose(1, 0, 2), conv_w)
    x2s = _ffn(x1s.reshape(n_s, D), *ffn_w)
    kc = cache_k_win.reshape(bd, WIN, KVW)
    vc = cache_v_win.reshape(bd, WIN, KVW)
    x3s, hs, route_s, route_ts, knew, vnew = _attn_sample(x2s, kc, vc, sink, bias_c, bias_n, attn_w)

    pos, off = _positions(jnp.concatenate([route_tp, route_ts], axis=1))
    n_rows = 2 * (n_p + n_s)
    xs = _dispatch(pos, hp, hs)
    o_sorted = _experts(_work_items(off[:, 0], n_rows), xs, weg, weu, wed)
    y_p = _combine(pos, x3p.reshape(n_p, D), route_p, o_sorted, 0).reshape(b, t, D)
    y_s = _combine(pos, x3s, route_s, o_sorted, n_p // TOK).reshape(s_len, bd, D).transpose(1, 0, 2)

    n_prev = CONV_W - 1
    conv_p = ulast[:, HALO - n_prev:, :][None]
    conv_s = jnp.concatenate([state_conv[:, :, s_len:, :], u_s.transpose(1, 0, 2)[None]], axis=2)
    k_p = klast.reshape(b, WIN, NKV, DH)
    v_p = vlast.reshape(b, WIN, NKV, DH)
    k_new = knew.reshape(s_len, bd, NKV, DH).transpose(1, 0, 2, 3)
    v_new = vnew.reshape(s_len, bd, NKV, DH).transpose(1, 0, 2, 3)
    k_s = jnp.concatenate([cache_k_win[:, s_len:], k_new], axis=1)
    v_s = jnp.concatenate([cache_v_win[:, s_len:], v_new], axis=1)
    return (y_p, y_s, conv_p, conv_s, k_p, k_s, v_p, v_s)
```

```python
import functools

import numpy as np
import jax
import jax.numpy as jnp
from jax import lax
from jax.experimental import pallas as pl
from jax.experimental.pallas import tpu as pltpu

D = 1024
CONV_W = 31
HALO = 32
DH = 64
NH = 16
NKV = 4
G = NH // NKV
KVW = NKV * DH
WIN = 128
NB = 32
MAXD = 128
NE = 8
EPS = 1e-6
LANES = 128
NEG = float("-inf")

TT_CONV = 512
TM_FFN = 512
TQ_ATT = 512
TOK = 512
TM_MOE = 512
ROW_DMA_UNROLL = 8
SEQ_TILE = 8
SEQ_TILE_CONV = 32
FF_CHUNK = 1024
VMEM_LIMIT = 56 * 1024 * 1024

bf16 = jnp.bfloat16
f32 = jnp.float32


def _cp(*sem):
    return pltpu.CompilerParams(dimension_semantics=sem, vmem_limit_bytes=VMEM_LIMIT)


def _full(shape, once=False):
    n = len(shape)
    return pl.BlockSpec(shape, lambda *_: (0,) * n, pipeline_mode=pl.Buffered(1) if once else None)


def _sigmoid(x):
    return 1.0 / (1.0 + jnp.exp(-x))


def _rms(x):
    return x * lax.rsqrt(jnp.mean(x * x, axis=-1, keepdims=True) + EPS)


def _dot(a, b):
    return jnp.dot(a, b, preferred_element_type=f32)


def _dot_nt(a, b):
    return lax.dot_general(a, b, (((1,), (1,)), ((), ())), preferred_element_type=f32)


def _split_dot(x, w):
    hi = x.astype(bf16)
    lo = (x - hi.astype(f32)).astype(bf16)
    return _dot(hi, w) + _dot(lo, w)


def _head_rms(x, gain, ind_ref, spread_ref):
    c = x.shape[1]
    ss = _split_dot(x * x, ind_ref[0:c, :])
    r = lax.rsqrt(ss * (1.0 / DH) + EPS)
    return x * _split_dot(r, spread_ref[:, 0:c]) * gain


def _kv_lane_mask(kv):
    lane = lax.broadcasted_iota(jnp.int32, (1, KVW), 1)
    return (lane >= kv * DH) & (lane < (kv + 1) * DH)


def _bucket_of(dist):
    n = np.maximum(dist, 0)
    max_exact = NB // 2
    large = max_exact + (np.log(np.maximum(n, 1) / max_exact) / np.log(MAXD / max_exact)
                         * (NB - max_exact)).astype(np.int32)
    large = np.minimum(large, NB - 1)
    return np.where(n < max_exact, n, large).astype(np.int32)


def _prompt_bucket_map():
    dist = np.arange(WIN)[:, None] + WIN - np.arange(2 * WIN)[None, :]
    valid = (dist >= 0) & (dist < WIN)
    bk = np.where(valid, _bucket_of(dist), -1).astype(np.int32)
    first = bk.copy()
    first[:, :WIN] = -1
    return np.stack([first, bk])


def _sample_bucket_maps(s_len, st):
    r_t = np.repeat(np.arange(s_len), st)
    r_b = np.tile(np.arange(st), s_len)
    c_b = np.repeat(np.arange(st), WIN)
    c_s = np.tile(np.arange(WIN), st)
    dist = r_t[:, None] + WIN - c_s[None, :]
    valid = (r_b[:, None] == c_b[None, :]) & (dist >= 0) & (dist < WIN)
    map_c = np.where(valid, _bucket_of(dist), -1).astype(np.int32)
    dist_n = r_t[:, None] - r_t[None, :]
    valid_n = (r_b[:, None] == r_b[None, :]) & (dist_n >= 0) & (dist_n < WIN)
    map_n = np.full((s_len * st, LANES), -1, np.int32)
    map_n[:, :s_len * st] = np.where(valid_n, _bucket_of(dist_n), -1)
    return map_c[None], map_n[None]


def _bias_kernel(rb_ref, map_ref, out_ref):
    h = pl.program_id(1)
    bk = map_ref[0]
    acc = jnp.full(bk.shape, NEG, f32)
    for b in range(NB):
        acc = jnp.where(bk == b, rb_ref[b, h], acc)
    out_ref[0, 0] = acc


def _bias_table(rel_bias, bucket_map):
    v, r, c = bucket_map.shape
    return pl.pallas_call(
        _bias_kernel,
        grid=(v, NH),
        in_specs=[pl.BlockSpec(memory_space=pltpu.SMEM),
                  pl.BlockSpec((1, r, c), lambda i, h: (i, 0, 0))],
        out_specs=pl.BlockSpec((1, 1, r, c), lambda i, h: (i, h, 0, 0)),
        out_shape=jax.ShapeDtypeStruct((v, NH, r, c), f32),
        compiler_params=_cp("arbitrary", "arbitrary"),
        name="bias_table",
    )(rel_bias, jnp.asarray(bucket_map))


def _glu_rows(x, gn_ref, w1_ref, b1_ref):
    h = (_rms(x) * gn_ref[...]).astype(bf16)
    u2 = _dot(h, w1_ref[...]) + b1_ref[...]
    return u2[:, :D] * _sigmoid(u2[:, D:])


def _conv_tail(x, c, gln_ref, bln_ref, w2_ref, b2_ref):
    mu = jnp.mean(c, axis=-1, keepdims=True)
    xc = c - mu
    var = jnp.mean(xc * xc, axis=-1, keepdims=True)
    y = xc * lax.rsqrt(var + EPS) * gln_ref[...] + bln_ref[...]
    y = y * _sigmoid(y)
    return x + _dot(y.astype(bf16), w2_ref[...]) + b2_ref[...]


def _swiglu_rows(x, g_ref, wg_ref, wu_ref, wd_ref):
    h = (_rms(x) * g_ref[...]).astype(bf16)
    ff = wg_ref.shape[1]
    acc = x
    for c0 in range(0, ff, FF_CHUNK):
        c1 = min(c0 + FF_CHUNK, ff)
        a = _dot(h, wg_ref[:, c0:c1])
        a = a * _sigmoid(a) * _dot(h, wu_ref[:, c0:c1])
        acc = acc + _dot(a.astype(bf16), wd_ref[c0:c1, :])
    return acc


def _conv_prompt_kernel(x_ref, gn_ref, w1_ref, b1_ref, wdw_ref, bdw_ref, gln_ref, bln_ref,
                        w2_ref, b2_ref, x1_ref, ulast_ref, ubuf, ush):
    tt = x_ref.shape[1]
    t = pl.program_id(1)
    x = x_ref[0]
    u = _glu_rows(x, gn_ref, w1_ref, b1_ref)

    @pl.when(t == 0)
    def _():
        ubuf[0:HALO, :] = jnp.zeros((HALO, D), f32)

    @pl.when(t > 0)
    def _():
        ubuf[0:HALO, :] = ubuf[tt:tt + HALO, :]

    ubuf[HALO:HALO + tt, :] = u
    ulast_ref[0] = u[tt - HALO:, :]
    n_sh = ush.shape[1]
    for r in range(1, 8):
        ush[r - 1] = ubuf[r:r + n_sh, :]
    off = HALO - (CONV_W - 1)
    c = jnp.broadcast_to(bdw_ref[...], (tt, D))
    for k in range(CONV_W):
        a, r = divmod(off + k, 8)
        src = ubuf[8 * a:8 * a + tt, :] if r == 0 else ush[r - 1, 8 * a:8 * a + tt, :]
        c = c + wdw_ref[k:k + 1, :] * src
    x1_ref[0] = _conv_tail(x, c, gln_ref, bln_ref, w2_ref, b2_ref)


def _conv_prompt(x, wts):
    b, t, _ = x.shape
    tt = min(TT_CONV, t)
    assert t % tt == 0 and tt >= HALO
    row = lambda n: _full((1, n))
    return pl.pallas_call(
        _conv_prompt_kernel,
        grid=(b, t // tt),
        in_specs=[pl.BlockSpec((1, tt, D), lambda i, j: (i, j, 0)),
                  row(D), _full((D, 2 * D)), row(2 * D), _full((HALO, D)), row(D), row(D), row(D),
                  _full((D, D)), row(D)],
        out_specs=[pl.BlockSpec((1, tt, D), lambda i, j: (i, j, 0)),
                   pl.BlockSpec((1, HALO, D), lambda i, j: (i, 0, 0))],
        out_shape=[jax.ShapeDtypeStruct((b, t, D), f32),
                   jax.ShapeDtypeStruct((b, HALO, D), f32)],
        scratch_shapes=[pltpu.VMEM((tt + HALO, D), f32),
                        pltpu.VMEM((7, tt + HALO - 8, D), f32)],
        compiler_params=_cp("arbitrary", "arbitrary"),
        name="conv_prompt",
    )(x, *wts)


def _conv_sample_kernel(x_ref, st_ref, gn_ref, w1_ref, b1_ref, wdw_ref, bdw_ref, gln_ref, bln_ref,
                        w2_ref, b2_ref, x1_ref, u_ref):
    s_len, bs, _ = x_ref.shape
    x = x_ref[...].reshape(s_len * bs, D)
    u = _glu_rows(x, gn_ref, w1_ref, b1_ref)
    u_ref[...] = u.reshape(s_len, bs, D)
    n_prev = CONV_W - 1
    cs = []
    for t in range(s_len):
        c = jnp.broadcast_to(bdw_ref[...], (bs, D))
        for j in range(t, n_prev):
            c = c + wdw_ref[j - t:j - t + 1, :] * st_ref[j]
        for i in range(t + 1):
            k = n_prev - t + i
            c = c + wdw_ref[k:k + 1, :] * u[i * bs:(i + 1) * bs, :]
        cs.append(c)
    c = jnp.concatenate(cs, axis=0)
    x1_ref[...] = _conv_tail(x, c, gln_ref, bln_ref, w2_ref, b2_ref).reshape(s_len, bs, D)


def _conv_sample(x_tb, state_t, wts):
    s_len, bd, _ = x_tb.shape
    bs = min(SEQ_TILE_CONV, bd)
    assert bd % bs == 0 and bs % 8 == 0 and s_len <= CONV_W - 1
    row = lambda n: _full((1, n))
    blk = pl.BlockSpec((s_len, bs, D), lambda i: (0, i, 0))
    return pl.pallas_call(
        _conv_sample_kernel,
        grid=(bd // bs,),
        in_specs=[blk, pl.BlockSpec((CONV_W - 1, bs, D), lambda i: (0, i, 0)),
                  row(D), _full((D, 2 * D)), row(2 * D), _full((HALO, D)), row(D), row(D), row(D),
                  _full((D, D)), row(D)],
        out_specs=[blk, blk],
        out_shape=[jax.ShapeDtypeStruct((s_len, bd, D), f32)] * 2,
        compiler_params=_cp("arbitrary"),
        name="conv_sample",
    )(x_tb, state_t, *wts)


def _ffn_kernel(x_ref, g_ref, wg_ref, wu_ref, wd_ref, o_ref):
    o_ref[...] = _swiglu_rows(x_ref[...], g_ref, wg_ref, wu_ref, wd_ref)


def _ffn(x, g, wg, wu, wd):
    n = x.shape[0]
    tm = min(TM_FFN, n)
    assert n % tm == 0
    ff = wg.shape[1]
    return pl.pallas_call(
        _ffn_kernel,
        grid=(n // tm,),
        in_specs=[pl.BlockSpec((tm, D), lambda i: (i, 0)), _full((1, D)),
                  _full((D, ff)), _full((D, ff)), _full((ff, D))],
        out_specs=pl.BlockSpec((tm, D), lambda i: (i, 0)),
        out_shape=jax.ShapeDtypeStruct((n, D), f32),
        compiler_params=_cp("arbitrary"),
        name="dense_ffn",
    )(x, g, wg, wu, wd)


def _qkv(x, gkv_ref, wkv_ref, gk_ref, gq_attn_ref, wq_ref, gq_ref, ind_ref, spread_ref):
    xn = _rms(x)
    kv = _dot((xn * gkv_ref[...]).astype(bf16), wkv_ref[...])
    k = _head_rms(kv[:, :KVW], gk_ref[...], ind_ref, spread_ref)
    v = kv[:, KVW:]
    q = _dot((xn * gq_attn_ref[...]).astype(bf16), wq_ref[...])
    q = _head_rms(q, gq_ref[...], ind_ref, spread_ref) * (DH ** -0.5)
    return q, k, v


def _sink_col(sinks_ref, kv, rows_per_head):
    return jnp.concatenate(
        [jnp.full((rows_per_head, 1), sinks_ref[kv * G + g], f32) for g in range(G)], axis=0)


def _moe_prep(x3, gmoe_ref, wr_ref, h_ref, route_ref, route_t_ref):
    h = _rms(x3) * gmoe_ref[...]
    h_ref[...] = h
    logits = _dot(h.astype(bf16), wr_ref[...])
    lane = lax.broadcasted_iota(jnp.int32, logits.shape, 1).astype(f32)
    logits = jnp.where(lane < NE, logits, NEG)
    m1 = jnp.max(logits, axis=-1, keepdims=True)
    i1 = jnp.min(jnp.where(logits == m1, lane, float(LANES)), axis=-1, keepdims=True)
    rest = jnp.where(lane == i1, NEG, logits)
    m2 = jnp.max(rest, axis=-1, keepdims=True)
    i2 = jnp.min(jnp.where(rest == m2, lane, float(LANES)), axis=-1, keepdims=True)
    e2 = jnp.exp(m2 - m1)
    den = 1.0 + e2
    g1 = 1.0 / den
    g2 = e2 / den
    route = jnp.where(lane == 0, i1,
                      jnp.where(lane == 1, i2,
                                jnp.where(lane == 2, g1, jnp.where(lane == 3, g2, 0.0))))
    route_ref[...] = route
    route_t_ref[...] = route.T[0:8, :]


def _attn_prompt_kernel(sinks_ref, x_ref, gkv_ref, wkv_ref, gk_ref, gqa_ref, wq_ref, gq_ref,
                        ind_ref, spread_ref, ones_ref, bias_ref, wo_ref, gmoe_ref, wr_ref,
                        x3_ref, h_ref, route_ref, route_t_ref, klast_ref, vlast_ref,
                        kbuf, vbuf, obuf):
    tq = x_ref.shape[1]
    t = pl.program_id(1)
    x = x_ref[0]
    q, k, v = _qkv(x, gkv_ref, wkv_ref, gk_ref, gqa_ref, wq_ref, gq_ref, ind_ref, spread_ref)
    klast_ref[0] = k[tq - WIN:, :]
    vlast_ref[0] = v[tq - WIN:, :]

    @pl.when(t == 0)
    def _():
        kbuf[:, 0:WIN, :] = jnp.zeros((NKV, WIN, KVW), bf16)
        vbuf[:, 0:WIN, :] = jnp.zeros((NKV, WIN, KVW), bf16)

    @pl.when(t > 0)
    def _():
        kbuf[:, 0:WIN, :] = kbuf[:, tq:tq + WIN, :]
        vbuf[:, 0:WIN, :] = vbuf[:, tq:tq + WIN, :]

    kb = k.astype(bf16)
    vb = v.astype(bf16)
    for kv in range(NKV):
        m = _kv_lane_mask(kv)
        kbuf[kv, WIN:WIN + tq, :] = jnp.where(m, kb, jnp.zeros_like(kb))
        vbuf[kv, WIN:WIN + tq, :] = jnp.where(m, vb, jnp.zeros_like(vb))

    qb = q.astype(bf16)
    for n in range(tq // WIN):
        r0 = n * WIN
        lhs = jnp.concatenate([qb[r0:r0 + WIN, g * KVW:(g + 1) * KVW] for g in range(G)], axis=0)
        ps = []
        lane = lax.broadcasted_iota(jnp.int32, (1, LANES), 1)
        sink_term = jnp.zeros((G * WIN, LANES), f32)
        for kv in range(NKV):
            s = _dot_nt(lhs, kbuf[kv, r0:r0 + 2 * WIN, :])
            if n == 0:
                s = s + jnp.where(t == 0, bias_ref[0, kv], bias_ref[1, kv])
            else:
                s = s + bias_ref[1, kv]
            sink = _sink_col(sinks_ref, kv, WIN)
            mx = jnp.maximum(jnp.max(s, axis=-1, keepdims=True), sink)
            ps.append(jnp.exp(s - mx).astype(bf16))
            sink_term = jnp.where(lane == kv, jnp.exp(sink - mx), sink_term)
        pcat = jnp.concatenate(ps, axis=-1)
        vcat = jnp.concatenate([vbuf[kv, r0:r0 + 2 * WIN, :] for kv in range(NKV)], axis=0)
        den = _dot(pcat, ones_ref[...]) + sink_term
        inv = 1.0 / jnp.where(lane < NKV, den, 1.0)
        pv = _dot(pcat, vcat) * _split_dot(inv, spread_ref[:, 0:KVW])
        obuf[r0:r0 + WIN, :] = jnp.concatenate(
            [pv[g * WIN:(g + 1) * WIN, :] for g in range(G)], axis=-1).astype(bf16)

    x3 = x + _dot(obuf[...], wo_ref[...])
    x3_ref[0] = x3
    _moe_prep(x3, gmoe_ref, wr_ref, h_ref, route_ref, route_t_ref)


def _attn_prompt(x2, sinks, bias, wts):
    b, t, _ = x2.shape
    tq = min(TQ_ATT, t)
    assert t % tq == 0 and tq % WIN == 0
    nt = t // tq
    gkv, wkv, gk, gqa, wq, gq, ind, spread, wo, gmoe, wr = wts
    ones = jnp.asarray(np.repeat(np.eye(NKV, LANES), 2 * WIN, axis=0), bf16)
    row = lambda n: _full((1, n))
    tok = lambda w: pl.BlockSpec((tq, w), lambda i, j: (i * nt + j, 0))
    last = pl.BlockSpec((1, WIN, KVW), lambda i, j: (i, 0, 0))
    return pl.pallas_call(
        _attn_prompt_kernel,
        grid=(b, nt),
        in_specs=[pl.BlockSpec(memory_space=pltpu.SMEM),
                  pl.BlockSpec((1, tq, D), lambda i, j: (i, j, 0)),
                  row(D), _full((D, 2 * KVW)), row(KVW), row(D), _full((D, D)), row(D),
                  _full(ind.shape), _full(spread.shape), _full(ones.shape),
                  _full(bias.shape), _full((D, D)), row(D), _full((D, LANES))],
        out_specs=[pl.BlockSpec((1, tq, D), lambda i, j: (i, j, 0)), tok(D), tok(LANES),
                   pl.BlockSpec((8, tq), lambda i, j: (0, i * nt + j)), last, last],
        out_shape=[jax.ShapeDtypeStruct((b, t, D), f32),
                   jax.ShapeDtypeStruct((b * t, D), f32),
                   jax.ShapeDtypeStruct((b * t, LANES), f32),
                   jax.ShapeDtypeStruct((8, b * t), f32),
                   jax.ShapeDtypeStruct((b, WIN, KVW), f32),
                   jax.ShapeDtypeStruct((b, WIN, KVW), f32)],
        scratch_shapes=[pltpu.VMEM((NKV, tq + WIN, KVW), bf16),
                        pltpu.VMEM((NKV, tq + WIN, KVW), bf16),
                        pltpu.VMEM((tq, D), bf16)],
        compiler_params=_cp("arbitrary", "arbitrary"),
        name="attn_prompt",
    )(sinks, x2, gkv, wkv, gk, gqa, wq, gq, ind, spread, ones, bias, wo, gmoe, wr)


def _attn_sample_kernel(sinks_ref, x_ref, kc_ref, vc_ref, gkv_ref, wkv_ref, gk_ref, gqa_ref, wq_ref,
                        gq_ref, ind_ref, spread_ref, bias_c_ref, bias_n_ref, wo_ref, gmoe_ref, wr_ref,
                        x3_ref, h_ref, route_ref, route_t_ref, knew_ref, vnew_ref,
                        qbuf, obuf, *, bd):
    st = kc_ref.shape[0]
    s_len = x_ref.shape[0] // bd
    rows = s_len * st
    i = pl.program_id(0)

    @pl.when(i == 0)
    def _():
        q, k, v = _qkv(x_ref[...], gkv_ref, wkv_ref, gk_ref, gqa_ref, wq_ref, gq_ref, ind_ref, spread_ref)
        knew_ref[...] = k
        vnew_ref[...] = v
        qbuf[...] = q

    def tile_rows(buf):
        return jnp.concatenate(
            [buf[pl.ds(pl.multiple_of(t * bd + i * st, 8), st), :] for t in range(s_len)],
            axis=0).astype(bf16)

    kbuf, vbuf = knew_ref, vnew_ref

    qt = tile_rows(qbuf)
    lhs = jnp.concatenate([qt[:, g * KVW:(g + 1) * KVW] for g in range(G)], axis=0)
    pad = jnp.zeros((LANES - rows, KVW), bf16)
    kn = jnp.concatenate([tile_rows(kbuf), pad], axis=0)
    vn = jnp.concatenate([tile_rows(vbuf), pad], axis=0)
    kc = kc_ref[...].reshape(st * WIN, KVW).astype(bf16)
    vc = vc_ref[...].reshape(st * WIN, KVW).astype(bf16)
    pv = jnp.zeros((G * rows, KVW), f32)
    for kv in range(NKV):
        m = _kv_lane_mask(kv)
        zc = jnp.zeros_like(kc)
        zn = jnp.zeros_like(kn)
        b0 = kv * G * rows
        s_c = _dot_nt(lhs, jnp.where(m, kc, zc)) + bias_c_ref[b0:b0 + G * rows, :]
        s_n = _dot_nt(lhs, jnp.where(m, kn, zn)) + bias_n_ref[b0:b0 + G * rows, :]
        sink = _sink_col(sinks_ref, kv, rows)
        mx = jnp.maximum(jnp.maximum(jnp.max(s_c, axis=-1, keepdims=True),
                                     jnp.max(s_n, axis=-1, keepdims=True)), sink)
        p_c = jnp.exp(s_c - mx)
        p_n = jnp.exp(s_n - mx)
        den = (jnp.sum(p_c, axis=-1, keepdims=True) + jnp.sum(p_n, axis=-1, keepdims=True)
               + jnp.exp(sink - mx))
        inv = 1.0 / den
        pv = pv + _dot((p_c * inv).astype(bf16), jnp.where(m, vc, zc))
        pv = pv + _dot((p_n * inv).astype(bf16), jnp.where(m, vn, zn))
    o = jnp.concatenate([pv[g * rows:(g + 1) * rows, :] for g in range(G)], axis=-1)
    for t in range(s_len):
        obuf[pl.ds(pl.multiple_of(t * bd + i * st, 8), st), :] = o[t * st:(t + 1) * st, :]

    @pl.when(i == pl.num_programs(0) - 1)
    def _():
        x3 = x_ref[...] + _dot(obuf[...].astype(bf16), wo_ref[...])
        x3_ref[...] = x3
        _moe_prep(x3, gmoe_ref, wr_ref, h_ref, route_ref, route_t_ref)


def _attn_sample(x2, kc, vc, sinks, bias_c, bias_n, wts):
    n_tok = x2.shape[0]
    bd = kc.shape[0]
    st = SEQ_TILE
    assert bd % st == 0 and n_tok % bd == 0 and (n_tok // bd) * st <= LANES
    gkv, wkv, gk, gqa, wq, gq, ind, spread, wo, gmoe, wr = wts
    row = lambda n: _full((1, n))
    cache = pl.BlockSpec((st, WIN, KVW), lambda i: (i, 0, 0))
    return pl.pallas_call(
        functools.partial(_attn_sample_kernel, bd=bd),
        grid=(bd // st,),
        in_specs=[pl.BlockSpec(memory_space=pltpu.SMEM), _full((n_tok, D)), cache, cache,
                  row(D), _full((D, 2 * KVW)), row(KVW), row(D), _full((D, D)), row(D),
                  _full(ind.shape), _full(spread.shape), _full(bias_c.shape), _full(bias_n.shape), _full((D, D)), row(D), _full((D, LANES))],
        out_specs=[_full((n_tok, D)), _full((n_tok, D)), _full((n_tok, LANES)), _full((8, n_tok)),
                   _full((n_tok, KVW)), _full((n_tok, KVW))],
        out_shape=[jax.ShapeDtypeStruct((n_tok, D), f32),
                   jax.ShapeDtypeStruct((n_tok, D), f32),
                   jax.ShapeDtypeStruct((n_tok, LANES), f32),
                   jax.ShapeDtypeStruct((8, n_tok), f32),
                   jax.ShapeDtypeStruct((n_tok, KVW), f32),
                   jax.ShapeDtypeStruct((n_tok, KVW), f32)],
        scratch_shapes=[pltpu.VMEM((n_tok, D), f32), pltpu.VMEM((n_tok, D), f32)],
        compiler_params=_cp("arbitrary"),
        name="attn_sample",
    )(sinks, x2, kc, vc, gkv, wkv, gk, gqa, wq, gq, ind, spread, bias_c, bias_n, wo, gmoe, wr)


def _positions_kernel(rt_ref, pos_ref, off_ref, counts, running, offs):
    ph = pl.program_id(0)
    i = pl.program_id(1)
    tok = rt_ref.shape[1]
    e0 = rt_ref[0:1, :]
    e1 = rt_ref[1:2, :]
    sub = lax.broadcasted_iota(jnp.int32, (NE, tok), 0).astype(f32)
    sel = ((sub == e0) | (sub == e1)).astype(f32)
    tile_cnt = jnp.broadcast_to(jnp.sum(sel, axis=-1, keepdims=True), (NE, LANES))

    @pl.when((ph == 0) & (i == 0))
    def _():
        counts[...] = jnp.zeros((NE, LANES), f32)

    @pl.when(ph == 0)
    def _():
        counts[...] += tile_cnt

    @pl.when((ph == 1) & (i == 0))
    def _():
        sub_l = lax.broadcasted_iota(jnp.int32, (NE, LANES), 0)
        acc = jnp.zeros((NE, LANES), f32)
        for e in range(NE - 1):
            acc = acc + jnp.where(sub_l > e, counts[e:e + 1, :], 0.0)
        offs[...] = acc
        running[...] = jnp.zeros((NE, LANES), f32)
        off_ref[...] = acc.astype(jnp.int32)

    @pl.when(ph == 1)
    def _():
        r_i = lax.broadcasted_iota(jnp.int32, (tok, tok), 0)
        c_i = lax.broadcasted_iota(jnp.int32, (tok, tok), 1)
        upper = (r_i < c_i).astype(bf16)
        cum = _dot(sel.astype(bf16), upper)
        base = offs[:, 0:1] + running[:, 0:1]
        tot = cum + base
        p0 = jnp.sum(jnp.where(sub == e0, tot, 0.0), axis=0, keepdims=True)
        p1 = jnp.sum(jnp.where(sub == e1, tot, 0.0), axis=0, keepdims=True)
        pos_ref[0] = jnp.concatenate([p0, p1], axis=0).astype(jnp.int32)
        running[...] += tile_cnt


def _positions(route_t):
    n = route_t.shape[1]
    assert n % TOK == 0
    nt = n // TOK
    return pl.pallas_call(
        _positions_kernel,
        grid=(2, nt),
        in_specs=[pl.BlockSpec((8, TOK), lambda p, i: (0, i))],
        out_specs=[pl.BlockSpec((1, 2, TOK), lambda p, i: (i * p, 0, 0)), _full((NE, LANES))],
        out_shape=[jax.ShapeDtypeStruct((nt, 2, TOK), jnp.int32),
                   jax.ShapeDtypeStruct((NE, LANES), jnp.int32)],
        scratch_shapes=[pltpu.VMEM((NE, LANES), f32)] * 3,
        compiler_params=_cp("arbitrary", "arbitrary"),
        name="moe_positions",
    )(route_t)


def _row_copy(src, src_row, dst, dst_row, sem):
    return pltpu.make_async_copy(src.at[pl.ds(src_row, 1)], dst.at[pl.ds(dst_row, 1)], sem)


def _dispatch_kernel(pos_ref, hp_ref, hs_ref, xs_ref, sem, *, n_p_tiles):
    i = pl.program_id(0)
    tok = hp_ref.shape[0]

    def scatter_rows(h_ref):
        def issue(r, c):
            _row_copy(h_ref, r, xs_ref, pos_ref[0, 0, r], sem).start()
            _row_copy(h_ref, r, xs_ref, pos_ref[0, 1, r], sem).start()
            return c

        lax.fori_loop(0, tok, issue, 0, unroll=ROW_DMA_UNROLL)
        for _ in range(2):
            pltpu.make_async_copy(h_ref, xs_ref.at[pl.ds(0, tok)], sem).wait()

    @pl.when(i < n_p_tiles)
    def _():
        scatter_rows(hp_ref)

    @pl.when(i >= n_p_tiles)
    def _():
        scatter_rows(hs_ref)


def _dispatch(pos, hp, hs):
    n_p, n_s = hp.shape[0], hs.shape[0]
    assert n_p % TOK == 0 and n_s % TOK == 0
    npt, nst = n_p // TOK, n_s // TOK
    return pl.pallas_call(
        functools.partial(_dispatch_kernel, n_p_tiles=npt),
        grid=(npt + nst,),
        in_specs=[pl.BlockSpec((1, 2, TOK), lambda i: (i, 0, 0), memory_space=pltpu.SMEM),
                  pl.BlockSpec((TOK, D), lambda i: (jnp.minimum(i, npt - 1), 0)),
                  pl.BlockSpec((TOK, D), lambda i: (jnp.maximum(i - npt, 0), 0))],
        out_specs=pl.BlockSpec(memory_space=pl.ANY),
        out_shape=jax.ShapeDtypeStruct((2 * (n_p + n_s), D), f32),
        scratch_shapes=[pltpu.SemaphoreType.DMA(())],
        compiler_params=_cp("arbitrary"),
        name="moe_dispatch",
    )(pos, hp, hs)


def _experts_kernel(tile_ref, exp_ref, lo_ref, hi_ref, xs_ref, wg_ref, wu_ref, wd_ref, o_ref):
    del tile_ref, exp_ref
    k = pl.program_id(0)
    lo = lo_ref[k]
    hi = hi_ref[k]

    @pl.when(hi > lo)
    def _():
        x = xs_ref[...].astype(bf16)
        ff = wg_ref.shape[2]
        acc = jnp.zeros(o_ref.shape, f32)
        for c0 in range(0, ff, FF_CHUNK):
            c1 = min(c0 + FF_CHUNK, ff)
            a = _dot(x, wg_ref[0, :, c0:c1])
            a = a * _sigmoid(a) * _dot(x, wu_ref[0, :, c0:c1])
            acc = acc + _dot(a.astype(bf16), wd_ref[0, c0:c1, :])
        @pl.when(lo == 0)
        def _():
            o_ref[...] = acc

        @pl.when(lo > 0)
        def _():
            row = lax.broadcasted_iota(jnp.int32, (o_ref.shape[0], 1), 0)
            o_ref[...] = jnp.where((row >= lo) & (row < hi), acc, o_ref[...])


def _experts(items, xs, wg, wu, wd):
    n_rows = xs.shape[0]
    ff = wg.shape[2]
    tm = TM_MOE
    assert n_rows % tm == 0
    n_items = items[0].shape[0]
    once = pl.Buffered(1)
    grid_spec = pltpu.PrefetchScalarGridSpec(
        num_scalar_prefetch=4,
        grid=(n_items,),
        in_specs=[pl.BlockSpec((tm, D), lambda k, ti, ex, lo, hi: (ti[k], 0)),
                  pl.BlockSpec((1, D, ff), lambda k, ti, ex, lo, hi: (ex[k], 0, 0), pipeline_mode=once),
                  pl.BlockSpec((1, D, ff), lambda k, ti, ex, lo, hi: (ex[k], 0, 0), pipeline_mode=once),
                  pl.BlockSpec((1, ff, D), lambda k, ti, ex, lo, hi: (ex[k], 0, 0), pipeline_mode=once)],
        out_specs=pl.BlockSpec((tm, D), lambda k, ti, ex, lo, hi: (ti[k], 0)))
    return pl.pallas_call(
        _experts_kernel,
        grid_spec=grid_spec,
        out_shape=jax.ShapeDtypeStruct((n_rows, D), f32),
        compiler_params=_cp("arbitrary"),
        name="moe_experts",
    )(*items, xs, wg, wu, wd)


def _work_items(off, n_rows):
    tm = TM_MOE
    n_tiles = n_rows // tm
    n_items = n_tiles + NE - 1
    start = off
    end = jnp.concatenate([off[1:], jnp.array([n_rows], jnp.int32)])
    cnt = end - start
    first = start // tm
    last = jnp.where(cnt > 0, (end - 1) // tm, first - 1)
    per = last - first + 1
    cum = jnp.cumsum(per)
    k = jnp.arange(n_items, dtype=jnp.int32)
    e = jnp.minimum(jnp.searchsorted(cum, k, side="right"), NE - 1).astype(jnp.int32)
    tile = first[e] + (k - (cum[e] - per[e]))
    real = k < cum[NE - 1]
    e = jnp.where(real, e, jnp.max(jnp.where(real, e, 0)))
    tile = jnp.where(real, tile, n_tiles - 1).astype(jnp.int32)
    lo = jnp.clip(start[e] - tile * tm, 0, tm)
    hi = jnp.clip(end[e] - tile * tm, 0, tm)
    lo = jnp.where(real, lo, 0).astype(jnp.int32)
    hi = jnp.where(real, hi, 0).astype(jnp.int32)
    return tile, e, lo, hi


def _combine_kernel(pos_ref, pos_next_ref, x3_ref, route_ref, o_hbm, y_ref, buf, sem, *, n_tiles):
    i = pl.program_id(0)
    tok = x3_ref.shape[0]
    slot = i % 2

    def gather(p_ref, sl):
        def issue(r, c):
            _row_copy(o_hbm, p_ref[0, 0, r], buf.at[sl, 0], r, sem.at[sl]).start()
            _row_copy(o_hbm, p_ref[0, 1, r], buf.at[sl, 1], r, sem.at[sl]).start()
            return c

        lax.fori_loop(0, tok, issue, 0, unroll=ROW_DMA_UNROLL)

    @pl.when(i == 0)
    def _():
        gather(pos_ref, slot)

    if n_tiles > 1:
        @pl.when(i + 1 < n_tiles)
        def _():
            gather(pos_next_ref, 1 - slot)

    for s in range(2):
        pltpu.make_async_copy(o_hbm.at[pl.ds(0, tok)], buf.at[slot, s], sem.at[slot]).wait()
    route = route_ref[...]
    y_ref[...] = x3_ref[...] + route[:, 2:3] * buf[slot, 0] + route[:, 3:4] * buf[slot, 1]


def _combine(pos, x3, route, o_sorted, tile0):
    n = x3.shape[0]
    assert n % TOK == 0
    nt = n // TOK
    pos_spec = lambda d: pl.BlockSpec((1, 2, TOK), lambda i: (jnp.minimum(i + d, nt - 1) + tile0, 0, 0),
                                      memory_space=pltpu.SMEM)
    return pl.pallas_call(
        functools.partial(_combine_kernel, n_tiles=nt),
        grid=(nt,),
        in_specs=[pos_spec(0), pos_spec(1),
                  pl.BlockSpec((TOK, D), lambda i: (i, 0)),
                  pl.BlockSpec((TOK, LANES), lambda i: (i, 0)),
                  pl.BlockSpec(memory_space=pl.ANY)],
        out_specs=pl.BlockSpec((TOK, D), lambda i: (i, 0)),
        out_shape=jax.ShapeDtypeStruct((n, D), f32),
        scratch_shapes=[pltpu.VMEM((2, 2, TOK, D), f32), pltpu.SemaphoreType.DMA((2,))],
        compiler_params=_cp("arbitrary"),
        name="moe_combine",
    )(pos, pos, x3, route, o_sorted)


def kernel(x_prompt, x_sample, state_conv, cache_k_win, cache_v_win, g_conv_norm, w_pw1, b_pw1, w_dw, b_dw, g_ln, b_ln, w_pw2, b_pw2, g_kv_norm, w_kv, g_k_norm, g_attn_norm, w_q, g_q_norm, sinks, w_o, rel_bias, g_ffn_norm, w_gate, w_up, w_down, g_moe_norm, w_router, w_e_gate, w_e_up, w_e_down):
    b, t, _ = x_prompt.shape
    bd, s_len, _ = x_sample.shape
    n_p = b * t
    n_s = bd * s_len
    row = lambda a: a.reshape(1, -1).astype(f32)

    conv_w = (row(g_conv_norm[0]), w_pw1[0].astype(bf16), row(b_pw1[0]),
              jnp.pad(w_dw[0], ((0, HALO - CONV_W), (0, 0))), row(b_dw[0]), row(g_ln[0]), row(b_ln[0]),
              w_pw2[0].astype(bf16), row(b_pw2[0]))
    ffn_w = (row(g_ffn_norm[0]), w_gate[0].astype(bf16), w_up[0].astype(bf16), w_down[0].astype(bf16))
    wq = w_q[0].reshape(D, NKV, G, DH).transpose(0, 2, 1, 3).reshape(D, NH * DH).astype(bf16)
    wo = w_o[0].reshape(NKV, G, DH, D).transpose(1, 0, 2, 3).reshape(NH * DH, D).astype(bf16)
    wr = jnp.pad(w_router[0], ((0, 0), (0, LANES - NE))).astype(bf16)
    head_of_lane = np.repeat(np.eye(NH, LANES), DH, axis=0)
    attn_w = (row(g_kv_norm), w_kv.astype(bf16), row(jnp.tile(g_k_norm, NKV)), row(g_attn_norm[0]),
              wq, row(jnp.tile(g_q_norm[0], NH)), jnp.asarray(head_of_lane, bf16),
              jnp.asarray(head_of_lane.T, bf16), wo, row(g_moe_norm[0]), wr)
    weg, weu, wed = w_e_gate[0].astype(bf16), w_e_up[0].astype(bf16), w_e_down[0].astype(bf16)
    sink = sinks[0].astype(f32)

    bias_p = _bias_table(rel_bias, _prompt_bucket_map()).reshape(2, NKV, G * WIN, 2 * WIN)
    map_c, map_n = _sample_bucket_maps(s_len, SEQ_TILE)
    rows = s_len * SEQ_TILE
    bias_c = _bias_table(rel_bias, map_c).reshape(NH * rows, SEQ_TILE * WIN)
    bias_n = _bias_table(rel_bias, map_n).reshape(NH * rows, LANES)

    x1p, ulast = _conv_prompt(x_prompt, conv_w)
    x2p = _ffn(x1p.reshape(n_p, D), *ffn_w).reshape(b, t, D)
    x3p, hp, route_p, route_tp, klast, vlast = _attn_prompt(x2p, sink, bias_p, attn_w)

    xs_tb = x_sample.transpose(1, 0, 2)
    x1s, u_s = _conv_sample(xs_tb, state_conv[0].transpose(1, 0, 2), conv_w)
    x2s = _ffn(x1s.reshape(n_s, D), *ffn_w)
    kc = cache_k_win.reshape(bd, WIN, KVW)
    vc = cache_v_win.reshape(bd, WIN, KVW)
    x3s, hs, route_s, route_ts, knew, vnew = _attn_sample(x2s, kc, vc, sink, bias_c, bias_n, attn_w)

    pos, off = _positions(jnp.concatenate([route_tp, route_ts], axis=1))
    n_rows = 2 * (n_p + n_s)
    xs = _dispatch(pos, hp, hs)
    o_sorted = _experts(_work_items(off[:, 0], n_rows), xs, weg, weu, wed)
    y_p = _combine(pos, x3p.reshape(n_p, D), route_p, o_sorted, 0).reshape(b, t, D)
    y_s = _combine(pos, x3s, route_s, o_sorted, n_p // TOK).reshape(s_len, bd, D).transpose(1, 0, 2)

    n_prev = CONV_W - 1
    conv_p = ulast[:, HALO - n_prev:, :][None]
    conv_s = jnp.concatenate([state_conv[:, :, s_len:, :], u_s.transpose(1, 0, 2)[None]], axis=2)
    k_p = klast.reshape(b, WIN, NKV, DH)
    v_p = vlast.reshape(b, WIN, NKV, DH)
    k_new = knew.reshape(s_len, bd, NKV, DH).transpose(1, 0, 2, 3)
    v_new = vnew.reshape(s_len, bd, NKV, DH).transpose(1, 0, 2, 3)
    k_s = jnp.concatenate([cache_k_win[:, s_len:], k_new], axis=1)
    v_s = jnp.concatenate([cache_v_win[:, s_len:], v_new], axis=1)
    return (y_p, y_s, conv_p, conv_s, k_p, k_s, v_p, v_s)
```

```python
import functools

import numpy as np
import jax
import jax.numpy as jnp
from jax import lax
from jax.experimental import pallas as pl
from jax.experimental.pallas import tpu as pltpu

D = 1024
CONV_W = 31
HALO = 32
DH = 64
NH = 16
NKV = 4
G = NH // NKV
KVW = NKV * DH
WIN = 128
NB = 32
MAXD = 128
NE = 8
EPS = 1e-6
LANES = 128
NEG = float("-inf")

TT_CONV = 512
TM_FFN = 512
TQ_ATT = 512
TOK = 512
TM_MOE = 512
ROW_DMA_UNROLL = 8
SEQ_TILE = 8
SEQ_TILE_CONV = 32
FF_CHUNK = 1024
VMEM_LIMIT = 56 * 1024 * 1024

bf16 = jnp.bfloat16
f32 = jnp.float32


def _cp(*sem):
    return pltpu.CompilerParams(dimension_semantics=sem, vmem_limit_bytes=VMEM_LIMIT)


def _full(shape, once=False):
    n = len(shape)
    return pl.BlockSpec(shape, lambda *_: (0,) * n, pipeline_mode=pl.Buffered(1) if once else None)


def _sigmoid(x):
    return 1.0 / (1.0 + jnp.exp(-x))


def _rms(x):
    return x * lax.rsqrt(jnp.mean(x * x, axis=-1, keepdims=True) + EPS)


def _dot(a, b):
    return jnp.dot(a, b, preferred_element_type=f32)


def _dot_nt(a, b):
    return lax.dot_general(a, b, (((1,), (1,)), ((), ())), preferred_element_type=f32)


def _split_dot(x, w):
    hi = x.astype(bf16)
    lo = (x - hi.astype(f32)).astype(bf16)
    return _dot(hi, w) + _dot(lo, w)


def _head_rms(x, gain, ind_ref, spread_ref):
    c = x.shape[1]
    ss = _split_dot(x * x, ind_ref[0:c, :])
    r = lax.rsqrt(ss * (1.0 / DH) + EPS)
    return x * _split_dot(r, spread_ref[:, 0:c]) * gain


def _kv_lane_mask(kv):
    lane = lax.broadcasted_iota(jnp.int32, (1, KVW), 1)
    return (lane >= kv * DH) & (lane < (kv + 1) * DH)


def _bucket_of(dist):
    n = np.maximum(dist, 0)
    max_exact = NB // 2
    large = max_exact + (np.log(np.maximum(n, 1) / max_exact) / np.log(MAXD / max_exact)
                         * (NB - max_exact)).astype(np.int32)
    large = np.minimum(large, NB - 1)
    return np.where(n < max_exact, n, large).astype(np.int32)


def _prompt_bucket_map():
    dist = np.arange(WIN)[:, None] + WIN - np.arange(2 * WIN)[None, :]
    valid = (dist >= 0) & (dist < WIN)
    bk = np.where(valid, _bucket_of(dist), -1).astype(np.int32)
    first = bk.copy()
    first[:, :WIN] = -1
    return np.stack([first, bk])


def _sample_bucket_maps(s_len, st):
    r_t = np.repeat(np.arange(s_len), st)
    r_b = np.tile(np.arange(st), s_len)
    c_b = np.repeat(np.arange(st), WIN)
    c_s = np.tile(np.arange(WIN), st)
    dist = r_t[:, None] + WIN - c_s[None, :]
    valid = (r_b[:, None] == c_b[None, :]) & (dist >= 0) & (dist < WIN)
    map_c = np.where(valid, _bucket_of(dist), -1).astype(np.int32)
    dist_n = r_t[:, None] - r_t[None, :]
    valid_n = (r_b[:, None] == r_b[None, :]) & (dist_n >= 0) & (dist_n < WIN)
    map_n = np.full((s_len * st, LANES), -1, np.int32)
    map_n[:, :s_len * st] = np.where(valid_n, _bucket_of(dist_n), -1)
    return map_c[None], map_n[None]


def _bias_kernel(rb_ref, map_ref, out_ref):
    h = pl.program_id(1)
    bk = map_ref[0]
    acc = jnp.full(bk.shape, NEG, f32)
    for b in range(NB):
        acc = jnp.where(bk == b, rb_ref[b, h], acc)
    out_ref[0, 0] = acc


def _bias_table(rel_bias, bucket_map):
    v, r, c = bucket_map.shape
    return pl.pallas_call(
        _bias_kernel,
        grid=(v, NH),
        in_specs=[pl.BlockSpec(memory_space=pltpu.SMEM),
                  pl.BlockSpec((1, r, c), lambda i, h: (i, 0, 0))],
        out_specs=pl.BlockSpec((1, 1, r, c), lambda i, h: (i, h, 0, 0)),
        out_shape=jax.ShapeDtypeStruct((v, NH, r, c), f32),
        compiler_params=_cp("arbitrary", "arbitrary"),
        name="bias_table",
    )(rel_bias, jnp.asarray(bucket_map))


def _glu_rows(x, gn_ref, w1_ref, b1_ref):
    h = (_rms(x) * gn_ref[...]).astype(bf16)
    u2 = _dot(h, w1_ref[...]) + b1_ref[...]
    return u2[:, :D] * _sigmoid(u2[:, D:])


def _conv_tail(x, c, gln_ref, bln_ref, w2_ref, b2_ref):
    mu = jnp.mean(c, axis=-1, keepdims=True)
    xc = c - mu
    var = jnp.mean(xc * xc, axis=-1, keepdims=True)
    y = xc * lax.rsqrt(var + EPS) * gln_ref[...] + bln_ref[...]
    y = y * _sigmoid(y)
    return x + _dot(y.astype(bf16), w2_ref[...]) + b2_ref[...]


def _swiglu_rows(x, g_ref, wg_ref, wu_ref, wd_ref):
    h = (_rms(x) * g_ref[...]).astype(bf16)
    ff = wg_ref.shape[1]
    acc = x
    for c0 in range(0, ff, FF_CHUNK):
        c1 = min(c0 + FF_CHUNK, ff)
        a = _dot(h, wg_ref[:, c0:c1])
        a = a * _sigmoid(a) * _dot(h, wu_ref[:, c0:c1])
        acc = acc + _dot(a.astype(bf16), wd_ref[c0:c1, :])
    return acc


def _conv_prompt_kernel(x_ref, gn_ref, w1_ref, b1_ref, wdw_ref, bdw_ref, gln_ref, bln_ref,
                        w2_ref, b2_ref, x1_ref, ulast_ref, ubuf, ush):
    tt = x_ref.shape[1]
    t = pl.program_id(1)
    x = x_ref[0]
    u = _glu_rows(x, gn_ref, w1_ref, b1_ref)

    @pl.when(t == 0)
    def _():
        ubuf[0:HALO, :] = jnp.zeros((HALO, D), f32)

    @pl.when(t > 0)
    def _():
        ubuf[0:HALO, :] = ubuf[tt:tt + HALO, :]

    ubuf[HALO:HALO + tt, :] = u
    ulast_ref[0] = u[tt - HALO:, :]
    n_sh = ush.shape[1]
    for r in range(1, 8):
        ush[r - 1] = ubuf[r:r + n_sh, :]
    off = HALO - (CONV_W - 1)
    c = jnp.broadcast_to(bdw_ref[...], (tt, D))
    for k in range(CONV_W):
        a, r = divmod(off + k, 8)
        src = ubuf[8 * a:8 * a + tt, :] if r == 0 else ush[r - 1, 8 * a:8 * a + tt, :]
        c = c + wdw_ref[k:k + 1, :] * src
    x1_ref[0] = _conv_tail(x, c, gln_ref, bln_ref, w2_ref, b2_ref)


def _conv_prompt(x, wts):
    b, t, _ = x.shape
    tt = min(TT_CONV, t)
    assert t % tt == 0 and tt >= HALO
    row = lambda n: _full((1, n))
    return pl.pallas_call(
        _conv_prompt_kernel,
        grid=(b, t // tt),
        in_specs=[pl.BlockSpec((1, tt, D), lambda i, j: (i, j, 0)),
                  row(D), _full((D, 2 * D)), row(2 * D), _full((HALO, D)), row(D), row(D), row(D),
                  _full((D, D)), row(D)],
        out_specs=[pl.BlockSpec((1, tt, D), lambda i, j: (i, j, 0)),
                   pl.BlockSpec((1, HALO, D), lambda i, j: (i, 0, 0))],
        out_shape=[jax.ShapeDtypeStruct((b, t, D), f32),
                   jax.ShapeDtypeStruct((b, HALO, D), f32)],
        scratch_shapes=[pltpu.VMEM((tt + HALO, D), f32),
                        pltpu.VMEM((7, tt + HALO - 8, D), f32)],
        compiler_params=_cp("arbitrary", "arbitrary"),
        name="conv_prompt",
    )(x, *wts)


def _conv_sample_kernel(x_ref, st_ref, gn_ref, w1_ref, b1_ref, wdw_ref, bdw_ref, gln_ref, bln_ref,
                        w2_ref, b2_ref, x1_ref, u_ref):
    s_len, bs, _ = x_ref.shape
    x = x_ref[...].reshape(s_len * bs, D)
    u = _glu_rows(x, gn_ref, w1_ref, b1_ref)
    u_ref[...] = u.reshape(s_len, bs, D)
    n_prev = CONV_W - 1
    cs = []
    for t in range(s_len):
        c = jnp.broadcast_to(bdw_ref[...], (bs, D))
        for j in range(t, n_prev):
            c = c + wdw_ref[j - t:j - t + 1, :] * st_ref[j]
        for i in range(t + 1):
            k = n_prev - t + i
            c = c + wdw_ref[k:k + 1, :] * u[i * bs:(i + 1) * bs, :]
        cs.append(c)
    c = jnp.concatenate(cs, axis=0)
    x1_ref[...] = _conv_tail(x, c, gln_ref, bln_ref, w2_ref, b2_ref).reshape(s_len, bs, D)


def _conv_sample(x_tb, state_t, wts):
    s_len, bd, _ = x_tb.shape
    bs = min(SEQ_TILE_CONV, bd)
    assert bd % bs == 0 and bs % 8 == 0 and s_len <= CONV_W - 1
    row = lambda n: _full((1, n))
    blk = pl.BlockSpec((s_len, bs, D), lambda i: (0, i, 0))
    return pl.pallas_call(
        _conv_sample_kernel,
        grid=(bd // bs,),
        in_specs=[blk, pl.BlockSpec((CONV_W - 1, bs, D), lambda i: (0, i, 0)),
                  row(D), _full((D, 2 * D)), row(2 * D), _full((HALO, D)), row(D), row(D), row(D),
                  _full((D, D)), row(D)],
        out_specs=[blk, blk],
        out_shape=[jax.ShapeDtypeStruct((s_len, bd, D), f32)] * 2,
        compiler_params=_cp("arbitrary"),
        name="conv_sample",
    )(x_tb, state_t, *wts)


def _ffn_kernel(x_ref, g_ref, wg_ref, wu_ref, wd_ref, o_ref):
    o_ref[...] = _swiglu_rows(x_ref[...], g_ref, wg_ref, wu_ref, wd_ref)


def _ffn(x, g, wg, wu, wd):
    n = x.shape[0]
    tm = min(TM_FFN, n)
    assert n % tm == 0
    ff = wg.shape[1]
    return pl.pallas_call(
        _ffn_kernel,
        grid=(n // tm,),
        in_specs=[pl.BlockSpec((tm, D), lambda i: (i, 0)), _full((1, D)),
                  _full((D, ff)), _full((D, ff)), _full((ff, D))],
        out_specs=pl.BlockSpec((tm, D), lambda i: (i, 0)),
        out_shape=jax.ShapeDtypeStruct((n, D), f32),
        compiler_params=_cp("arbitrary"),
        name="dense_ffn",
    )(x, g, wg, wu, wd)


def _qkv(x, gkv_ref, wkv_ref, gk_ref, gq_attn_ref, wq_ref, gq_ref, ind_ref, spread_ref):
    xn = _rms(x)
    kv = _dot((xn * gkv_ref[...]).astype(bf16), wkv_ref[...])
    k = _head_rms(kv[:, :KVW], gk_ref[...], ind_ref, spread_ref)
    v = kv[:, KVW:]
    q = _dot((xn * gq_attn_ref[...]).astype(bf16), wq_ref[...])
    q = _head_rms(q, gq_ref[...], ind_ref, spread_ref) * (DH ** -0.5)
    return q, k, v


def _sink_col(sinks_ref, kv, rows_per_head):
    return jnp.concatenate(
        [jnp.full((rows_per_head, 1), sinks_ref[kv * G + g], f32) for g in range(G)], axis=0)


def _moe_prep(x3, gmoe_ref, wr_ref, h_ref, route_ref, route_t_ref):
    h = _rms(x3) * gmoe_ref[...]
    h_ref[...] = h
    logits = _dot(h.astype(bf16), wr_ref[...])
    lane = lax.broadcasted_iota(jnp.int32, logits.shape, 1).astype(f32)
    logits = jnp.where(lane < NE, logits, NEG)
    m1 = jnp.max(logits, axis=-1, keepdims=True)
    i1 = jnp.min(jnp.where(logits == m1, lane, float(LANES)), axis=-1, keepdims=True)
    rest = jnp.where(lane == i1, NEG, logits)
    m2 = jnp.max(rest, axis=-1, keepdims=True)
    i2 = jnp.min(jnp.where(rest == m2, lane, float(LANES)), axis=-1, keepdims=True)
    e2 = jnp.exp(m2 - m1)
    den = 1.0 + e2
    g1 = 1.0 / den
    g2 = e2 / den
    route = jnp.where(lane == 0, i1,
                      jnp.where(lane == 1, i2,
                                jnp.where(lane == 2, g1, jnp.where(lane == 3, g2, 0.0))))
    route_ref[...] = route
    route_t_ref[...] = route.T[0:8, :]


def _attn_prompt_kernel(sinks_ref, x_ref, gkv_ref, wkv_ref, gk_ref, gqa_ref, wq_ref, gq_ref,
                        ind_ref, spread_ref, ones_ref, bias_ref, wo_ref, gmoe_ref, wr_ref,
                        x3_ref, h_ref, route_ref, route_t_ref, klast_ref, vlast_ref,
                        kbuf, vbuf, obuf):
    tq = x_ref.shape[1]
    t = pl.program_id(1)
    x = x_ref[0]
    q, k, v = _qkv(x, gkv_ref, wkv_ref, gk_ref, gqa_ref, wq_ref, gq_ref, ind_ref, spread_ref)
    klast_ref[0] = k[tq - WIN:, :]
    vlast_ref[0] = v[tq - WIN:, :]

    @pl.when(t == 0)
    def _():
        kbuf[:, 0:WIN, :] = jnp.zeros((NKV, WIN, KVW), bf16)
        vbuf[:, 0:WIN, :] = jnp.zeros((NKV, WIN, KVW), bf16)

    @pl.when(t > 0)
    def _():
        kbuf[:, 0:WIN, :] = kbuf[:, tq:tq + WIN, :]
        vbuf[:, 0:WIN, :] = vbuf[:, tq:tq + WIN, :]

    kb = k.astype(bf16)
    vb = v.astype(bf16)
    for kv in range(NKV):
        m = _kv_lane_mask(kv)
        kbuf[kv, WIN:WIN + tq, :] = jnp.where(m, kb, jnp.zeros_like(kb))
        vbuf[kv, WIN:WIN + tq, :] = jnp.where(m, vb, jnp.zeros_like(vb))

    qb = q.astype(bf16)
    for n in range(tq // WIN):
        r0 = n * WIN
        lhs = jnp.concatenate([qb[r0:r0 + WIN, g * KVW:(g + 1) * KVW] for g in range(G)], axis=0)
        ps = []
        lane = lax.broadcasted_iota(jnp.int32, (1, LANES), 1)
        sink_term = jnp.zeros((G * WIN, LANES), f32)
        for kv in range(NKV):
            s = _dot_nt(lhs, kbuf[kv, r0:r0 + 2 * WIN, :])
            if n == 0:
                s = s + jnp.where(t == 0, bias_ref[0, kv], bias_ref[1, kv])
            else:
                s = s + bias_ref[1, kv]
            sink = _sink_col(sinks_ref, kv, WIN)
            mx = jnp.maximum(jnp.max(s, axis=-1, keepdims=True), sink)
            ps.append(jnp.exp(s - mx).astype(bf16))
            sink_term = jnp.where(lane == kv, jnp.exp(sink - mx), sink_term)
        pcat = jnp.concatenate(ps, axis=-1)
        vcat = jnp.concatenate([vbuf[kv, r0:r0 + 2 * WIN, :] for kv in range(NKV)], axis=0)
        den = _dot(pcat, ones_ref[...]) + sink_term
        inv = 1.0 / jnp.where(lane < NKV, den, 1.0)
        pv = _dot(pcat, vcat) * _split_dot(inv, spread_ref[:, 0:KVW])
        obuf[r0:r0 + WIN, :] = jnp.concatenate(
            [pv[g * WIN:(g + 1) * WIN, :] for g in range(G)], axis=-1).astype(bf16)

    x3 = x + _dot(obuf[...], wo_ref[...])
    x3_ref[0] = x3
    _moe_prep(x3, gmoe_ref, wr_ref, h_ref, route_ref, route_t_ref)


def _attn_prompt(x2, sinks, bias, wts):
    b, t, _ = x2.shape
    tq = min(TQ_ATT, t)
    assert t % tq == 0 and tq % WIN == 0
    nt = t // tq
    gkv, wkv, gk, gqa, wq, gq, ind, spread, wo, gmoe, wr = wts
    ones = jnp.asarray(np.repeat(np.eye(NKV, LANES), 2 * WIN, axis=0), bf16)
    row = lambda n: _full((1, n))
    tok = lambda w: pl.BlockSpec((tq, w), lambda i, j: (i * nt + j, 0))
    last = pl.BlockSpec((1, WIN, KVW), lambda i, j: (i, 0, 0))
    return pl.pallas_call(
        _attn_prompt_kernel,
        grid=(b, nt),
        in_specs=[pl.BlockSpec(memory_space=pltpu.SMEM),
                  pl.BlockSpec((1, tq, D), lambda i, j: (i, j, 0)),
                  row(D), _full((D, 2 * KVW)), row(KVW), row(D), _full((D, D)), row(D),
                  _full(ind.shape), _full(spread.shape), _full(ones.shape),
                  _full(bias.shape), _full((D, D)), row(D), _full((D, LANES))],
        out_specs=[pl.BlockSpec((1, tq, D), lambda i, j: (i, j, 0)), tok(D), tok(LANES),
                   pl.BlockSpec((8, tq), lambda i, j: (0, i * nt + j)), last, last],
        out_shape=[jax.ShapeDtypeStruct((b, t, D), f32),
                   jax.ShapeDtypeStruct((b * t, D), f32),
                   jax.ShapeDtypeStruct((b * t, LANES), f32),
                   jax.ShapeDtypeStruct((8, b * t), f32),
                   jax.ShapeDtypeStruct((b, WIN, KVW), f32),
                   jax.ShapeDtypeStruct((b, WIN, KVW), f32)],
        scratch_shapes=[pltpu.VMEM((NKV, tq + WIN, KVW), bf16),
                        pltpu.VMEM((NKV, tq + WIN, KVW), bf16),
                        pltpu.VMEM((tq, D), bf16)],
        compiler_params=_cp("arbitrary", "arbitrary"),
        name="attn_prompt",
    )(sinks, x2, gkv, wkv, gk, gqa, wq, gq, ind, spread, ones, bias, wo, gmoe, wr)


def _attn_sample_kernel(sinks_ref, x_ref, kc_ref, vc_ref, gkv_ref, wkv_ref, gk_ref, gqa_ref, wq_ref,
                        gq_ref, ind_ref, spread_ref, bias_c_ref, bias_n_ref, wo_ref, gmoe_ref, wr_ref,
                        x3_ref, h_ref, route_ref, route_t_ref, knew_ref, vnew_ref,
                        qbuf, obuf, *, bd):
    st = kc_ref.shape[0]
    s_len = x_ref.shape[0] // bd
    rows = s_len * st
    i = pl.program_id(0)

    @pl.when(i == 0)
    def _():
        q, k, v = _qkv(x_ref[...], gkv_ref, wkv_ref, gk_ref, gqa_ref, wq_ref, gq_ref, ind_ref, spread_ref)
        knew_ref[...] = k
        vnew_ref[...] = v
        qbuf[...] = q

    def tile_rows(buf):
        return jnp.concatenate(
            [buf[pl.ds(pl.multiple_of(t * bd + i * st, 8), st), :] for t in range(s_len)],
            axis=0).astype(bf16)

    kbuf, vbuf = knew_ref, vnew_ref

    qt = tile_rows(qbuf)
    lhs = jnp.concatenate([qt[:, g * KVW:(g + 1) * KVW] for g in range(G)], axis=0)
    pad = jnp.zeros((LANES - rows, KVW), bf16)
    kn = jnp.concatenate([tile_rows(kbuf), pad], axis=0)
    vn = jnp.concatenate([tile_rows(vbuf), pad], axis=0)
    kc = kc_ref[...].reshape(st * WIN, KVW).astype(bf16)
    vc = vc_ref[...].reshape(st * WIN, KVW).astype(bf16)
    pv = jnp.zeros((G * rows, KVW), f32)
    for kv in range(NKV):
        m = _kv_lane_mask(kv)
        zc = jnp.zeros_like(kc)
        zn = jnp.zeros_like(kn)
        b0 = kv * G * rows
        s_c = _dot_nt(lhs, jnp.where(m, kc, zc)) + bias_c_ref[b0:b0 + G * rows, :]
        s_n = _dot_nt(lhs, jnp.where(m, kn, zn)) + bias_n_ref[b0:b0 + G * rows, :]
        sink = _sink_col(sinks_ref, kv, rows)
        mx = jnp.maximum(jnp.maximum(jnp.max(s_c, axis=-1, keepdims=True),
                                     jnp.max(s_n, axis=-1, keepdims=True)), sink)
        p_c = jnp.exp(s_c - mx)
        p_n = jnp.exp(s_n - mx)
        den = (jnp.sum(p_c, axis=-1, keepdims=True) + jnp.sum(p_n, axis=-1, keepdims=True)
               + jnp.exp(sink - mx))
        inv = 1.0 / den
        pv = pv + _dot((p_c * inv).astype(bf16), jnp.where(m, vc, zc))
        pv = pv + _dot((p_n * inv).astype(bf16), jnp.where(m, vn, zn))
    o = jnp.concatenate([pv[g * rows:(g + 1) * rows, :] for g in range(G)], axis=-1)
    for t in range(s_len):
        obuf[pl.ds(pl.multiple_of(t * bd + i * st, 8), st), :] = o[t * st:(t + 1) * st, :]

    @pl.when(i == pl.num_programs(0) - 1)
    def _():
        x3 = x_ref[...] + _dot(obuf[...].astype(bf16), wo_ref[...])
        x3_ref[...] = x3
        _moe_prep(x3, gmoe_ref, wr_ref, h_ref, route_ref, route_t_ref)


def _attn_sample(x2, kc, vc, sinks, bias_c, bias_n, wts):
    n_tok = x2.shape[0]
    bd = kc.shape[0]
    st = SEQ_TILE
    assert bd % st == 0 and n_tok % bd == 0 and (n_tok // bd) * st <= LANES
    gkv, wkv, gk, gqa, wq, gq, ind, spread, wo, gmoe, wr = wts
    row = lambda n: _full((1, n))
    cache = pl.BlockSpec((st, WIN, KVW), lambda i: (i, 0, 0))
    return pl.pallas_call(
        functools.partial(_attn_sample_kernel, bd=bd),
        grid=(bd // st,),
        in_specs=[pl.BlockSpec(memory_space=pltpu.SMEM), _full((n_tok, D)), cache, cache,
                  row(D), _full((D, 2 * KVW)), row(KVW), row(D), _full((D, D)), row(D),
                  _full(ind.shape), _full(spread.shape), _full(bias_c.shape), _full(bias_n.shape), _full((D, D)), row(D), _full((D, LANES))],
        out_specs=[_full((n_tok, D)), _full((n_tok, D)), _full((n_tok, LANES)), _full((8, n_tok)),
                   _full((n_tok, KVW)), _full((n_tok, KVW))],
        out_shape=[jax.ShapeDtypeStruct((n_tok, D), f32),
                   jax.ShapeDtypeStruct((n_tok, D), f32),
                   jax.ShapeDtypeStruct((n_tok, LANES), f32),
                   jax.ShapeDtypeStruct((8, n_tok), f32),
                   jax.ShapeDtypeStruct((n_tok, KVW), f32),
                   jax.ShapeDtypeStruct((n_tok, KVW), f32)],
        scratch_shapes=[pltpu.VMEM((n_tok, D), f32), pltpu.VMEM((n_tok, D), f32)],
        compiler_params=_cp("arbitrary"),
        name="attn_sample",
    )(sinks, x2, kc, vc, gkv, wkv, gk, gqa, wq, gq, ind, spread, bias_c, bias_n, wo, gmoe, wr)


def _positions_kernel(rt_ref, pos_ref, off_ref, counts, running, offs):
    ph = pl.program_id(0)
    i = pl.program_id(1)
    n_sub, _, tok = pos_ref.shape
    sub = lax.broadcasted_iota(jnp.int32, (NE, tok), 0).astype(f32)

    def tile(s):
        e0 = rt_ref[0:1, s * tok:(s + 1) * tok]
        e1 = rt_ref[1:2, s * tok:(s + 1) * tok]
        sel = ((sub == e0) | (sub == e1)).astype(f32)
        return e0, e1, sel, jnp.broadcast_to(jnp.sum(sel, axis=-1, keepdims=True), (NE, LANES))

    @pl.when((ph == 0) & (i == 0))
    def _():
        counts[...] = jnp.zeros((NE, LANES), f32)

    @pl.when(ph == 0)
    def _():
        for s in range(n_sub):
            counts[...] += tile(s)[3]

    @pl.when((ph == 1) & (i == 0))
    def _():
        sub_l = lax.broadcasted_iota(jnp.int32, (NE, LANES), 0)
        acc = jnp.zeros((NE, LANES), f32)
        for e in range(NE - 1):
            acc = acc + jnp.where(sub_l > e, counts[e:e + 1, :], 0.0)
        offs[...] = acc
        running[...] = jnp.zeros((NE, LANES), f32)
        off_ref[...] = acc.astype(jnp.int32)

    @pl.when(ph == 1)
    def _():
        r_i = lax.broadcasted_iota(jnp.int32, (tok, tok), 0)
        c_i = lax.broadcasted_iota(jnp.int32, (tok, tok), 1)
        upper = (r_i < c_i).astype(bf16)
        for s in range(n_sub):
            e0, e1, sel, tile_cnt = tile(s)
            cum = _dot(sel.astype(bf16), upper)
            tot = cum + (offs[:, 0:1] + running[:, 0:1])
            p0 = jnp.sum(jnp.where(sub == e0, tot, 0.0), axis=0, keepdims=True)
            p1 = jnp.sum(jnp.where(sub == e1, tot, 0.0), axis=0, keepdims=True)
            pos_ref[s] = jnp.concatenate([p0, p1], axis=0).astype(jnp.int32)
            running[...] += tile_cnt


def _positions(route_t):
    n = route_t.shape[1]
    assert n % TOK == 0
    nt = n // TOK
    n_sub = max(d for d in range(1, 9) if nt % d == 0)
    return pl.pallas_call(
        _positions_kernel,
        grid=(2, nt // n_sub),
        in_specs=[pl.BlockSpec((8, n_sub * TOK), lambda p, i: (0, i))],
        out_specs=[pl.BlockSpec((n_sub, 2, TOK), lambda p, i: (i * p, 0, 0)), _full((NE, LANES))],
        out_shape=[jax.ShapeDtypeStruct((nt, 2, TOK), jnp.int32),
                   jax.ShapeDtypeStruct((NE, LANES), jnp.int32)],
        scratch_shapes=[pltpu.VMEM((NE, LANES), f32)] * 3,
        compiler_params=_cp("arbitrary", "arbitrary"),
        name="moe_positions",
    )(route_t)


def _row_copy(src, src_row, dst, dst_row, sem):
    return pltpu.make_async_copy(src.at[pl.ds(src_row, 1)], dst.at[pl.ds(dst_row, 1)], sem)


def _dispatch_kernel(pos_ref, hp_ref, hs_ref, xs_ref, sem, *, n_p_tiles):
    i = pl.program_id(0)
    tok = hp_ref.shape[0]

    def scatter_rows(h_ref):
        def issue(r, c):
            _row_copy(h_ref, r, xs_ref, pos_ref[r], sem).start(priority=0)
            _row_copy(h_ref, r, xs_ref, pos_ref[tok + r], sem).start(priority=1)
            return c

        lax.fori_loop(0, tok, issue, 0, unroll=ROW_DMA_UNROLL)
        for _ in range(2):
            pltpu.make_async_copy(h_ref, xs_ref.at[pl.ds(0, tok)], sem).wait()

    @pl.when(i < n_p_tiles)
    def _():
        scatter_rows(hp_ref)

    @pl.when(i >= n_p_tiles)
    def _():
        scatter_rows(hs_ref)


def _dispatch(pos, hp, hs):
    n_p, n_s = hp.shape[0], hs.shape[0]
    assert n_p % TOK == 0 and n_s % TOK == 0
    npt, nst = n_p // TOK, n_s // TOK
    return pl.pallas_call(
        functools.partial(_dispatch_kernel, n_p_tiles=npt),
        grid=(npt + nst,),
        in_specs=[pl.BlockSpec((2 * TOK,), lambda i: (i,), memory_space=pltpu.SMEM),
                  pl.BlockSpec((TOK, D), lambda i: (jnp.minimum(i, npt - 1), 0)),
                  pl.BlockSpec((TOK, D), lambda i: (jnp.maximum(i - npt, 0), 0))],
        out_specs=pl.BlockSpec(memory_space=pl.ANY),
        out_shape=jax.ShapeDtypeStruct((2 * (n_p + n_s), D), f32),
        scratch_shapes=[pltpu.SemaphoreType.DMA(())],
        compiler_params=_cp("arbitrary"),
        name="moe_dispatch",
    )(pos, hp, hs)


def _experts_kernel(tile_ref, exp_ref, lo_ref, hi_ref, xs_ref, wg_ref, wu_ref, wd_ref, o_ref):
    del tile_ref, exp_ref
    k = pl.program_id(0)
    lo = lo_ref[k]
    hi = hi_ref[k]

    @pl.when(hi > lo)
    def _():
        x = xs_ref[...].astype(bf16)
        ff = wg_ref.shape[2]
        acc = jnp.zeros(o_ref.shape, f32)
        for c0 in range(0, ff, FF_CHUNK):
            c1 = min(c0 + FF_CHUNK, ff)
            a = _dot(x, wg_ref[0, :, c0:c1])
            a = a * _sigmoid(a) * _dot(x, wu_ref[0, :, c0:c1])
            acc = acc + _dot(a.astype(bf16), wd_ref[0, c0:c1, :])
        @pl.when(lo == 0)
        def _():
            o_ref[...] = acc

        @pl.when(lo > 0)
        def _():
            row = lax.broadcasted_iota(jnp.int32, (o_ref.shape[0], 1), 0)
            o_ref[...] = jnp.where((row >= lo) & (row < hi), acc, o_ref[...])


def _experts(items, xs, wg, wu, wd):
    n_rows = xs.shape[0]
    ff = wg.shape[2]
    tm = TM_MOE
    assert n_rows % tm == 0
    n_items = items[0].shape[0]
    once = pl.Buffered(1)
    grid_spec = pltpu.PrefetchScalarGridSpec(
        num_scalar_prefetch=4,
        grid=(n_items,),
        in_specs=[pl.BlockSpec((tm, D), lambda k, ti, ex, lo, hi: (ti[k], 0)),
                  pl.BlockSpec((1, D, ff), lambda k, ti, ex, lo, hi: (ex[k], 0, 0), pipeline_mode=once),
                  pl.BlockSpec((1, D, ff), lambda k, ti, ex, lo, hi: (ex[k], 0, 0), pipeline_mode=once),
                  pl.BlockSpec((1, ff, D), lambda k, ti, ex, lo, hi: (ex[k], 0, 0), pipeline_mode=once)],
        out_specs=pl.BlockSpec((tm, D), lambda k, ti, ex, lo, hi: (ti[k], 0)))
    return pl.pallas_call(
        _experts_kernel,
        grid_spec=grid_spec,
        out_shape=jax.ShapeDtypeStruct((n_rows, D), f32),
        compiler_params=_cp("arbitrary"),
        name="moe_experts",
    )(*items, xs, wg, wu, wd)


def _work_items(off, n_rows):
    tm = TM_MOE
    n_tiles = n_rows // tm
    n_items = n_tiles + NE - 1
    start = off
    end = jnp.concatenate([off[1:], jnp.array([n_rows], jnp.int32)])
    cnt = end - start
    first = start // tm
    last = jnp.where(cnt > 0, (end - 1) // tm, first - 1)
    per = last - first + 1
    cum = jnp.cumsum(per)
    k = jnp.arange(n_items, dtype=jnp.int32)
    e = jnp.minimum(jnp.sum(k[:, None] >= cum[None, :], axis=1), NE - 1).astype(jnp.int32)
    tile = first[e] + (k - (cum[e] - per[e]))
    real = k < cum[NE - 1]
    e = jnp.where(real, e, jnp.max(jnp.where(real, e, 0)))
    tile = jnp.where(real, tile, n_tiles - 1).astype(jnp.int32)
    lo = jnp.clip(start[e] - tile * tm, 0, tm)
    hi = jnp.clip(end[e] - tile * tm, 0, tm)
    lo = jnp.where(real, lo, 0).astype(jnp.int32)
    hi = jnp.where(real, hi, 0).astype(jnp.int32)
    return tile, e, lo, hi


def _combine_kernel(pos_ref, pos_next_ref, x3_ref, route_ref, o_hbm, y_ref, buf, sem, *, n_tiles):
    i = pl.program_id(0)
    tok = x3_ref.shape[0]
    slot = i % 2

    def gather(p_ref, sl):
        def issue(r, c):
            _row_copy(o_hbm, p_ref[r], buf.at[sl, 0], r, sem.at[sl]).start(priority=0)
            _row_copy(o_hbm, p_ref[tok + r], buf.at[sl, 1], r, sem.at[sl]).start(priority=1)
            return c

        lax.fori_loop(0, tok, issue, 0, unroll=ROW_DMA_UNROLL)

    @pl.when(i == 0)
    def _():
        gather(pos_ref, slot)

    if n_tiles > 1:
        @pl.when(i + 1 < n_tiles)
        def _():
            gather(pos_next_ref, 1 - slot)

    for s in range(2):
        pltpu.make_async_copy(o_hbm.at[pl.ds(0, tok)], buf.at[slot, s], sem.at[slot]).wait()
    route = route_ref[...]
    y_ref[...] = x3_ref[...] + route[:, 2:3] * buf[slot, 0] + route[:, 3:4] * buf[slot, 1]


def _combine(pos, x3, route, o_sorted, tile0):
    n = x3.shape[0]
    assert n % TOK == 0
    nt = n // TOK
    pos_spec = lambda d: pl.BlockSpec((2 * TOK,), lambda i: (jnp.minimum(i + d, nt - 1) + tile0,),
                                      memory_space=pltpu.SMEM)
    return pl.pallas_call(
        functools.partial(_combine_kernel, n_tiles=nt),
        grid=(nt,),
        in_specs=[pos_spec(0), pos_spec(1),
                  pl.BlockSpec((TOK, D), lambda i: (i, 0)),
                  pl.BlockSpec((TOK, LANES), lambda i: (i, 0)),
                  pl.BlockSpec(memory_space=pl.ANY)],
        out_specs=pl.BlockSpec((TOK, D), lambda i: (i, 0)),
        out_shape=jax.ShapeDtypeStruct((n, D), f32),
        scratch_shapes=[pltpu.VMEM((2, 2, TOK, D), f32), pltpu.SemaphoreType.DMA((2,))],
        compiler_params=_cp("arbitrary"),
        name="moe_combine",
    )(pos, pos, x3, route, o_sorted)


def kernel(x_prompt, x_sample, state_conv, cache_k_win, cache_v_win, g_conv_norm, w_pw1, b_pw1, w_dw, b_dw, g_ln, b_ln, w_pw2, b_pw2, g_kv_norm, w_kv, g_k_norm, g_attn_norm, w_q, g_q_norm, sinks, w_o, rel_bias, g_ffn_norm, w_gate, w_up, w_down, g_moe_norm, w_router, w_e_gate, w_e_up, w_e_down):
    b, t, _ = x_prompt.shape
    bd, s_len, _ = x_sample.shape
    n_p = b * t
    n_s = bd * s_len
    row = lambda a: a.reshape(1, -1).astype(f32)

    conv_w = (row(g_conv_norm[0]), w_pw1[0].astype(bf16), row(b_pw1[0]),
              jnp.pad(w_dw[0], ((0, HALO - CONV_W), (0, 0))), row(b_dw[0]), row(g_ln[0]), row(b_ln[0]),
              w_pw2[0].astype(bf16), row(b_pw2[0]))
    ffn_w = (row(g_ffn_norm[0]), w_gate[0].astype(bf16), w_up[0].astype(bf16), w_down[0].astype(bf16))
    wq = w_q[0].reshape(D, NKV, G, DH).transpose(0, 2, 1, 3).reshape(D, NH * DH).astype(bf16)
    wo = w_o[0].reshape(NKV, G, DH, D).transpose(1, 0, 2, 3).reshape(NH * DH, D).astype(bf16)
    wr = jnp.pad(w_router[0], ((0, 0), (0, LANES - NE))).astype(bf16)
    head_of_lane = np.repeat(np.eye(NH, LANES), DH, axis=0)
    attn_w = (row(g_kv_norm), w_kv.astype(bf16), row(jnp.tile(g_k_norm, NKV)), row(g_attn_norm[0]),
              wq, row(jnp.tile(g_q_norm[0], NH)), jnp.asarray(head_of_lane, bf16),
              jnp.asarray(head_of_lane.T, bf16), wo, row(g_moe_norm[0]), wr)
    weg, weu, wed = w_e_gate[0].astype(bf16), w_e_up[0].astype(bf16), w_e_down[0].astype(bf16)
    sink = sinks[0].astype(f32)

    bias_p = _bias_table(rel_bias, _prompt_bucket_map()).reshape(2, NKV, G * WIN, 2 * WIN)
    map_c, map_n = _sample_bucket_maps(s_len, SEQ_TILE)
    rows = s_len * SEQ_TILE
    bias_c = _bias_table(rel_bias, map_c).reshape(NH * rows, SEQ_TILE * WIN)
    bias_n = _bias_table(rel_bias, map_n).reshape(NH * rows, LANES)

    x1p, ulast = _conv_prompt(x_prompt, conv_w)
    x2p = _ffn(x1p.reshape(n_p, D), *ffn_w).reshape(b, t, D)
    x3p, hp, route_p, route_tp, klast, vlast = _attn_prompt(x2p, sink, bias_p, attn_w)

    xs_tb = x_sample.transpose(1, 0, 2)
    x1s, u_s = _conv_sample(xs_tb, state_conv[0].transpose(1, 0, 2), conv_w)
    x2s = _ffn(x1s.reshape(n_s, D), *ffn_w)
    kc = cache_k_win.reshape(bd, WIN, KVW)
    vc = cache_v_win.reshape(bd, WIN, KVW)
    x3s, hs, route_s, route_ts, knew, vnew = _attn_sample(x2s, kc, vc, sink, bias_c, bias_n, attn_w)

    pos, off = _positions(jnp.concatenate([route_tp, route_ts], axis=1))
    pos = pos.reshape(-1)
    n_rows = 2 * (n_p + n_s)
    xs = _dispatch(pos, hp, hs)
    o_sorted = _experts(_work_items(off[:, 0], n_rows), xs, weg, weu, wed)
    y_p = _combine(pos, x3p.reshape(n_p, D), route_p, o_sorted, 0).reshape(b, t, D)
    y_s = _combine(pos, x3s, route_s, o_sorted, n_p // TOK).reshape(s_len, bd, D).transpose(1, 0, 2)

    n_prev = CONV_W - 1
    conv_p = ulast[:, HALO - n_prev:, :][None]
    conv_s = jnp.concatenate([state_conv[:, :, s_len:, :], u_s.transpose(1, 0, 2)[None]], axis=2)
    k_p = klast.reshape(b, WIN, NKV, DH)
    v_p = vlast.reshape(b, WIN, NKV, DH)
    k_new = knew.reshape(s_len, bd, NKV, DH).transpose(1, 0, 2, 3)
    v_new = vnew.reshape(s_len, bd, NKV, DH).transpose(1, 0, 2, 3)
    k_s = jnp.concatenate([cache_k_win[:, s_len:], k_new], axis=1)
    v_s = jnp.concatenate([cache_v_win[:, s_len:], v_new], axis=1)
    return (y_p, y_s, conv_p, conv_s, k_p, k_s, v_p, v_s)
```

```python
import functools

import numpy as np
import jax
import jax.numpy as jnp
from jax import lax
from jax.experimental import pallas as pl
from jax.experimental.pallas import tpu as pltpu

D = 1024
CONV_W = 31
HALO = 32
DH = 64
NH = 16
NKV = 4
G = NH // NKV
KVW = NKV * DH
WIN = 128
NB = 32
MAXD = 128
NE = 8
EPS = 1e-6
LANES = 128
NEG = float("-inf")

TT_CONV = 512
TM_FFN = 1024
TQ_ATT = 512
TOK = 512
TM_MOE = 512
ROW_DMA_UNROLL = 16
SEQ_TILE = 8
SEQ_TILE_CONV = 32
FF_CHUNK = 1024
VMEM_LIMIT = 56 * 1024 * 1024

bf16 = jnp.bfloat16
f32 = jnp.float32


def _cp(*sem):
    return pltpu.CompilerParams(dimension_semantics=sem, vmem_limit_bytes=VMEM_LIMIT)


def _full(shape, once=False):
    n = len(shape)
    return pl.BlockSpec(shape, lambda *_: (0,) * n, pipeline_mode=pl.Buffered(1) if once else None)


def _sigmoid(x):
    return 1.0 / (1.0 + jnp.exp(-x))


def _rms(x):
    return x * lax.rsqrt(jnp.mean(x * x, axis=-1, keepdims=True) + EPS)


def _dot(a, b):
    return jnp.dot(a, b, preferred_element_type=f32)


def _dot_nt(a, b):
    return lax.dot_general(a, b, (((1,), (1,)), ((), ())), preferred_element_type=f32)


def _split_dot(x, w):
    hi = x.astype(bf16)
    lo = (x - hi.astype(f32)).astype(bf16)
    return _dot(hi, w) + _dot(lo, w)


def _head_rms(x, gain, ind_ref, spread_ref):
    c = x.shape[1]
    ss = _dot((x * x).astype(bf16), ind_ref[0:c, :])
    r = lax.rsqrt(ss * (1.0 / DH) + EPS)
    return x * _split_dot(r, spread_ref[:, 0:c]) * gain


def _kv_lane_mask(kv):
    lane = lax.broadcasted_iota(jnp.int32, (1, KVW), 1)
    return (lane >= kv * DH) & (lane < (kv + 1) * DH)


def _bucket_of(dist):
    n = np.maximum(dist, 0)
    max_exact = NB // 2
    large = max_exact + (np.log(np.maximum(n, 1) / max_exact) / np.log(MAXD / max_exact)
                         * (NB - max_exact)).astype(np.int32)
    large = np.minimum(large, NB - 1)
    return np.where(n < max_exact, n, large).astype(np.int32)


def _prompt_bucket_map():
    dist = np.arange(WIN)[:, None] + WIN - np.arange(2 * WIN)[None, :]
    valid = (dist >= 0) & (dist < WIN)
    bk = np.where(valid, _bucket_of(dist), -1).astype(np.int32)
    first = bk.copy()
    first[:, :WIN] = -1
    return np.stack([first, bk])


def _sample_bucket_maps(s_len, st):
    r_t = np.repeat(np.arange(s_len), st)
    r_b = np.tile(np.arange(st), s_len)
    c_b = np.repeat(np.arange(st), WIN)
    c_s = np.tile(np.arange(WIN), st)
    dist = r_t[:, None] + WIN - c_s[None, :]
    valid = (r_b[:, None] == c_b[None, :]) & (dist >= 0) & (dist < WIN)
    map_c = np.where(valid, _bucket_of(dist), -1).astype(np.int32)
    dist_n = r_t[:, None] - r_t[None, :]
    valid_n = (r_b[:, None] == r_b[None, :]) & (dist_n >= 0) & (dist_n < WIN)
    map_n = np.full((s_len * st, LANES), -1, np.int32)
    map_n[:, :s_len * st] = np.where(valid_n, _bucket_of(dist_n), -1)
    return map_c[None], map_n[None]


def _bias_kernel(rb_ref, map_ref, out_ref):
    h = pl.program_id(1)
    bk = map_ref[0]
    acc = jnp.full(bk.shape, NEG, f32)
    for b in range(NB):
        acc = jnp.where(bk == b, rb_ref[b, h], acc)
    out_ref[0, 0] = acc


def _bias_table(rel_bias, bucket_map):
    v, r, c = bucket_map.shape
    return pl.pallas_call(
        _bias_kernel,
        grid=(v, NH),
        in_specs=[pl.BlockSpec(memory_space=pltpu.SMEM),
                  pl.BlockSpec((1, r, c), lambda i, h: (i, 0, 0))],
        out_specs=pl.BlockSpec((1, 1, r, c), lambda i, h: (i, h, 0, 0)),
        out_shape=jax.ShapeDtypeStruct((v, NH, r, c), f32),
        compiler_params=_cp("arbitrary", "arbitrary"),
        name="bias_table",
    )(rel_bias, jnp.asarray(bucket_map))


def _glu_rows(x, gn_ref, w1_ref, b1_ref):
    h = (_rms(x) * gn_ref[...]).astype(bf16)
    u2 = _dot(h, w1_ref[...]) + b1_ref[...]
    return u2[:, :D] * _sigmoid(u2[:, D:])


def _conv_tail(x, c, gln_ref, bln_ref, w2_ref, b2_ref):
    mu = jnp.mean(c, axis=-1, keepdims=True)
    xc = c - mu
    var = jnp.mean(xc * xc, axis=-1, keepdims=True)
    y = xc * lax.rsqrt(var + EPS) * gln_ref[...] + bln_ref[...]
    y = y * _sigmoid(y)
    return x + _dot(y.astype(bf16), w2_ref[...]) + b2_ref[...]


def _swiglu_rows(x, g_ref, wg_ref, wu_ref, wd_ref):
    h = (_rms(x) * g_ref[...]).astype(bf16)
    ff = wg_ref.shape[1]
    acc = x
    for c0 in range(0, ff, FF_CHUNK):
        c1 = min(c0 + FF_CHUNK, ff)
        a = _dot(h, wg_ref[:, c0:c1])
        a = a * _sigmoid(a) * _dot(h, wu_ref[:, c0:c1])
        acc = acc + _dot(a.astype(bf16), wd_ref[c0:c1, :])
    return acc


def _conv_prompt_kernel(x_ref, gn_ref, w1_ref, b1_ref, wdw_ref, bdw_ref, gln_ref, bln_ref,
                        w2_ref, b2_ref, x1_ref, ulast_ref, ubuf, ush):
    tt = x_ref.shape[1]
    t = pl.program_id(1)
    x = x_ref[0]
    u = _glu_rows(x, gn_ref, w1_ref, b1_ref)

    @pl.when(t == 0)
    def _():
        ubuf[0:HALO, :] = jnp.zeros((HALO, D), f32)

    @pl.when(t > 0)
    def _():
        ubuf[0:HALO, :] = ubuf[tt:tt + HALO, :]

    ubuf[HALO:HALO + tt, :] = u
    ulast_ref[0] = u[tt - HALO:, :]
    n_sh = ush.shape[1]
    for r in range(1, 8):
        ush[r - 1] = ubuf[r:r + n_sh, :]
    off = HALO - (CONV_W - 1)
    c = jnp.broadcast_to(bdw_ref[...], (tt, D))
    for k in range(CONV_W):
        a, r = divmod(off + k, 8)
        src = ubuf[8 * a:8 * a + tt, :] if r == 0 else ush[r - 1, 8 * a:8 * a + tt, :]
        c = c + wdw_ref[k:k + 1, :] * src
    x1_ref[0] = _conv_tail(x, c, gln_ref, bln_ref, w2_ref, b2_ref)


def _conv_prompt(x, wts):
    b, t, _ = x.shape
    tt = min(TT_CONV, t)
    assert t % tt == 0 and tt >= HALO
    row = lambda n: _full((1, n))
    return pl.pallas_call(
        _conv_prompt_kernel,
        grid=(b, t // tt),
        in_specs=[pl.BlockSpec((1, tt, D), lambda i, j: (i, j, 0)),
                  row(D), _full((D, 2 * D)), row(2 * D), _full((HALO, D)), row(D), row(D), row(D),
                  _full((D, D)), row(D)],
        out_specs=[pl.BlockSpec((1, tt, D), lambda i, j: (i, j, 0)),
                   pl.BlockSpec((1, HALO, D), lambda i, j: (i, 0, 0))],
        out_shape=[jax.ShapeDtypeStruct((b, t, D), f32),
                   jax.ShapeDtypeStruct((b, HALO, D), f32)],
        scratch_shapes=[pltpu.VMEM((tt + HALO, D), f32),
                        pltpu.VMEM((7, tt + HALO - 8, D), f32)],
        compiler_params=_cp("arbitrary", "arbitrary"),
        name="conv_prompt",
    )(x, *wts)


def _conv_sample_kernel(x_ref, st_ref, gn_ref, w1_ref, b1_ref, wdw_ref, bdw_ref, gln_ref, bln_ref,
                        w2_ref, b2_ref, x1_ref, u_ref):
    s_len, bs, _ = x_ref.shape
    x = x_ref[...].reshape(s_len * bs, D)
    u = _glu_rows(x, gn_ref, w1_ref, b1_ref)
    u_ref[...] = u.reshape(s_len, bs, D)
    n_prev = CONV_W - 1
    cs = []
    for t in range(s_len):
        c = jnp.broadcast_to(bdw_ref[...], (bs, D))
        for j in range(t, n_prev):
            c = c + wdw_ref[j - t:j - t + 1, :] * st_ref[j]
        for i in range(t + 1):
            k = n_prev - t + i
            c = c + wdw_ref[k:k + 1, :] * u[i * bs:(i + 1) * bs, :]
        cs.append(c)
    c = jnp.concatenate(cs, axis=0)
    x1_ref[...] = _conv_tail(x, c, gln_ref, bln_ref, w2_ref, b2_ref).reshape(s_len, bs, D)


def _conv_sample(x_tb, state_t, wts):
    s_len, bd, _ = x_tb.shape
    bs = min(SEQ_TILE_CONV, bd)
    assert bd % bs == 0 and bs % 8 == 0 and s_len <= CONV_W - 1
    row = lambda n: _full((1, n))
    blk = pl.BlockSpec((s_len, bs, D), lambda i: (0, i, 0))
    return pl.pallas_call(
        _conv_sample_kernel,
        grid=(bd // bs,),
        in_specs=[blk, pl.BlockSpec((CONV_W - 1, bs, D), lambda i: (0, i, 0)),
                  row(D), _full((D, 2 * D)), row(2 * D), _full((HALO, D)), row(D), row(D), row(D),
                  _full((D, D)), row(D)],
        out_specs=[blk, blk],
        out_shape=[jax.ShapeDtypeStruct((s_len, bd, D), f32)] * 2,
        compiler_params=_cp("arbitrary"),
        name="conv_sample",
    )(x_tb, state_t, *wts)


def _ffn_kernel(x_ref, g_ref, wg_ref, wu_ref, wd_ref, o_ref):
    o_ref[...] = _swiglu_rows(x_ref[...], g_ref, wg_ref, wu_ref, wd_ref)


def _ffn(x, g, wg, wu, wd):
    n = x.shape[0]
    tm = min(TM_FFN, n)
    assert n % tm == 0
    ff = wg.shape[1]
    return pl.pallas_call(
        _ffn_kernel,
        grid=(n // tm,),
        in_specs=[pl.BlockSpec((tm, D), lambda i: (i, 0)), _full((1, D)),
                  _full((D, ff)), _full((D, ff)), _full((ff, D))],
        out_specs=pl.BlockSpec((tm, D), lambda i: (i, 0)),
        out_shape=jax.ShapeDtypeStruct((n, D), f32),
        compiler_params=_cp("arbitrary"),
        name="dense_ffn",
    )(x, g, wg, wu, wd)


def _qkv(x, gkv_ref, wkv_ref, gk_ref, gq_attn_ref, wq_ref, gq_ref, ind_ref, spread_ref):
    xn = _rms(x)
    kv = _dot((xn * gkv_ref[...]).astype(bf16), wkv_ref[...])
    k = _head_rms(kv[:, :KVW], gk_ref[...], ind_ref, spread_ref)
    v = kv[:, KVW:]
    q = _dot((xn * gq_attn_ref[...]).astype(bf16), wq_ref[...])
    q = _head_rms(q, gq_ref[...], ind_ref, spread_ref) * (DH ** -0.5)
    return q, k, v


def _sink_col(sinks_ref, kv, rows_per_head):
    return jnp.concatenate(
        [jnp.full((rows_per_head, 1), sinks_ref[kv * G + g], f32) for g in range(G)], axis=0)


def _moe_prep(x3, gmoe_ref, wr_ref, h_ref, route_ref, route_t_ref):
    h = _rms(x3) * gmoe_ref[...]
    h_ref[...] = h
    logits = _dot(h.astype(bf16), wr_ref[...])
    lane = lax.broadcasted_iota(jnp.int32, logits.shape, 1).astype(f32)
    logits = jnp.where(lane < NE, logits, NEG)
    m1 = jnp.max(logits, axis=-1, keepdims=True)
    i1 = jnp.min(jnp.where(logits == m1, lane, float(LANES)), axis=-1, keepdims=True)
    rest = jnp.where(lane == i1, NEG, logits)
    m2 = jnp.max(rest, axis=-1, keepdims=True)
    i2 = jnp.min(jnp.where(rest == m2, lane, float(LANES)), axis=-1, keepdims=True)
    e2 = jnp.exp(m2 - m1)
    den = 1.0 + e2
    g1 = 1.0 / den
    g2 = e2 / den
    route = jnp.where(lane == 0, i1,
                      jnp.where(lane == 1, i2,
                                jnp.where(lane == 2, g1, jnp.where(lane == 3, g2, 0.0))))
    route_ref[...] = route
    route_t_ref[...] = route.T[0:8, :]


def _attn_prompt_kernel(sinks_ref, x_ref, gkv_ref, wkv_ref, gk_ref, gqa_ref, wq_ref, gq_ref,
                        ind_ref, spread_ref, ones_ref, bias_ref, wo_ref, gmoe_ref, wr_ref,
                        x3_ref, h_ref, route_ref, route_t_ref, klast_ref, vlast_ref,
                        kbuf, vbuf, obuf):
    tq = x_ref.shape[1]
    t = pl.program_id(1)
    x = x_ref[0]
    q, k, v = _qkv(x, gkv_ref, wkv_ref, gk_ref, gqa_ref, wq_ref, gq_ref, ind_ref, spread_ref)
    klast_ref[0] = k[tq - WIN:, :]
    vlast_ref[0] = v[tq - WIN:, :]

    @pl.when(t == 0)
    def _():
        kbuf[:, 0:WIN, :] = jnp.zeros((NKV, WIN, KVW), bf16)
        vbuf[:, 0:WIN, :] = jnp.zeros((NKV, WIN, KVW), bf16)

    @pl.when(t > 0)
    def _():
        kbuf[:, 0:WIN, :] = kbuf[:, tq:tq + WIN, :]
        vbuf[:, 0:WIN, :] = vbuf[:, tq:tq + WIN, :]

    kb = k.astype(bf16)
    vb = v.astype(bf16)
    for kv in range(NKV):
        m = _kv_lane_mask(kv)
        kbuf[kv, WIN:WIN + tq, :] = jnp.where(m, kb, jnp.zeros_like(kb))
        vbuf[kv, WIN:WIN + tq, :] = jnp.where(m, vb, jnp.zeros_like(vb))

    qb = q.astype(bf16)
    for n in range(tq // WIN):
        r0 = n * WIN
        lhs = jnp.concatenate([qb[r0:r0 + WIN, g * KVW:(g + 1) * KVW] for g in range(G)], axis=0)
        ps = []
        lane = lax.broadcasted_iota(jnp.int32, (1, LANES), 1)
        sink_term = jnp.zeros((G * WIN, LANES), f32)
        for kv in range(NKV):
            s = _dot_nt(lhs, kbuf[kv, r0:r0 + 2 * WIN, :])
            s = s + bias_ref[jnp.where(t == 0, 0, 1) if n == 0 else 1, kv]
            sink = _sink_col(sinks_ref, kv, WIN)
            mx = jnp.maximum(jnp.max(s, axis=-1, keepdims=True), sink)
            ps.append(jnp.exp(s - mx).astype(bf16))
            sink_term = jnp.where(lane == kv, jnp.exp(sink - mx), sink_term)
        pcat = jnp.concatenate(ps, axis=-1)
        vcat = jnp.concatenate([vbuf[kv, r0:r0 + 2 * WIN, :] for kv in range(NKV)], axis=0)
        den = _dot(pcat, ones_ref[...]) + sink_term
        inv = 1.0 / jnp.where(lane < NKV, den, 1.0)
        pv = _dot(pcat, vcat) * _split_dot(inv, spread_ref[:, 0:KVW])
        obuf[r0:r0 + WIN, :] = jnp.concatenate(
            [pv[g * WIN:(g + 1) * WIN, :] for g in range(G)], axis=-1).astype(bf16)

    x3 = x + _dot(obuf[...], wo_ref[...])
    x3_ref[0] = x3
    _moe_prep(x3, gmoe_ref, wr_ref, h_ref, route_ref, route_t_ref)


def _attn_prompt(x2, sinks, bias, wts):
    b, t, _ = x2.shape
    tq = min(TQ_ATT, t)
    assert t % tq == 0 and tq % WIN == 0
    nt = t // tq
    gkv, wkv, gk, gqa, wq, gq, ind, spread, wo, gmoe, wr = wts
    ones = jnp.asarray(np.repeat(np.eye(NKV, LANES), 2 * WIN, axis=0), bf16)
    row = lambda n: _full((1, n))
    tok = lambda w: pl.BlockSpec((tq, w), lambda i, j: (i * nt + j, 0))
    last = pl.BlockSpec((1, WIN, KVW), lambda i, j: (i, 0, 0))
    return pl.pallas_call(
        _attn_prompt_kernel,
        grid=(b, nt),
        in_specs=[pl.BlockSpec(memory_space=pltpu.SMEM),
                  pl.BlockSpec((1, tq, D), lambda i, j: (i, j, 0)),
                  row(D), _full((D, 2 * KVW)), row(KVW), row(D), _full((D, D)), row(D),
                  _full(ind.shape), _full(spread.shape), _full(ones.shape),
                  _full(bias.shape), _full((D, D)), row(D), _full((D, LANES))],
        out_specs=[pl.BlockSpec((1, tq, D), lambda i, j: (i, j, 0)), tok(D), tok(LANES),
                   pl.BlockSpec((8, tq), lambda i, j: (0, i * nt + j)), last, last],
        out_shape=[jax.ShapeDtypeStruct((b, t, D), f32),
                   jax.ShapeDtypeStruct((b * t, D), f32),
                   jax.ShapeDtypeStruct((b * t, LANES), f32),
                   jax.ShapeDtypeStruct((8, b * t), f32),
                   jax.ShapeDtypeStruct((b, WIN, KVW), f32),
                   jax.ShapeDtypeStruct((b, WIN, KVW), f32)],
        scratch_shapes=[pltpu.VMEM((NKV, tq + WIN, KVW), bf16),
                        pltpu.VMEM((NKV, tq + WIN, KVW), bf16),
                        pltpu.VMEM((tq, D), bf16)],
        compiler_params=_cp("arbitrary", "arbitrary"),
        name="attn_prompt",
    )(sinks, x2, gkv, wkv, gk, gqa, wq, gq, ind, spread, ones, bias, wo, gmoe, wr)


def _attn_sample_kernel(sinks_ref, x_ref, kc_ref, vc_ref, gkv_ref, wkv_ref, gk_ref, gqa_ref, wq_ref,
                        gq_ref, ind_ref, spread_ref, bias_c_ref, bias_n_ref, wo_ref, gmoe_ref, wr_ref,
                        x3_ref, h_ref, route_ref, route_t_ref, knew_ref, vnew_ref,
                        qbuf, obuf, *, bd):
    st = kc_ref.shape[0]
    s_len = x_ref.shape[0] // bd
    rows = s_len * st
    i = pl.program_id(0)

    @pl.when(i == 0)
    def _():
        q, k, v = _qkv(x_ref[...], gkv_ref, wkv_ref, gk_ref, gqa_ref, wq_ref, gq_ref, ind_ref, spread_ref)
        knew_ref[...] = k
        vnew_ref[...] = v
        qbuf[...] = q

    def tile_rows(buf):
        return jnp.concatenate(
            [buf[pl.ds(pl.multiple_of(t * bd + i * st, 8), st), :] for t in range(s_len)],
            axis=0).astype(bf16)

    kbuf, vbuf = knew_ref, vnew_ref

    qt = tile_rows(qbuf)
    lhs = jnp.concatenate([qt[:, g * KVW:(g + 1) * KVW] for g in range(G)], axis=0)
    pad = jnp.zeros((LANES - rows, KVW), bf16)
    kn = jnp.concatenate([tile_rows(kbuf), pad], axis=0)
    vn = jnp.concatenate([tile_rows(vbuf), pad], axis=0)
    kc = kc_ref[...].reshape(st * WIN, KVW).astype(bf16)
    vc = vc_ref[...].reshape(st * WIN, KVW).astype(bf16)
    pv = jnp.zeros((G * rows, KVW), f32)
    for kv in range(NKV):
        m = _kv_lane_mask(kv)
        zc = jnp.zeros_like(kc)
        zn = jnp.zeros_like(kn)
        b0 = kv * G * rows
        s_c = _dot_nt(lhs, jnp.where(m, kc, zc)) + bias_c_ref[b0:b0 + G * rows, :]
        s_n = _dot_nt(lhs, jnp.where(m, kn, zn)) + bias_n_ref[b0:b0 + G * rows, :]
        sink = _sink_col(sinks_ref, kv, rows)
        mx = jnp.maximum(jnp.maximum(jnp.max(s_c, axis=-1, keepdims=True),
                                     jnp.max(s_n, axis=-1, keepdims=True)), sink)
        p_c = jnp.exp(s_c - mx)
        p_n = jnp.exp(s_n - mx)
        den = (jnp.sum(p_c, axis=-1, keepdims=True) + jnp.sum(p_n, axis=-1, keepdims=True)
               + jnp.exp(sink - mx))
        inv = 1.0 / den
        pv = pv + _dot((p_c * inv).astype(bf16), jnp.where(m, vc, zc))
        pv = pv + _dot((p_n * inv).astype(bf16), jnp.where(m, vn, zn))
    o = jnp.concatenate([pv[g * rows:(g + 1) * rows, :] for g in range(G)], axis=-1)
    for t in range(s_len):
        obuf[pl.ds(pl.multiple_of(t * bd + i * st, 8), st), :] = o[t * st:(t + 1) * st, :]

    @pl.when(i == pl.num_programs(0) - 1)
    def _():
        x3 = x_ref[...] + _dot(obuf[...].astype(bf16), wo_ref[...])
        x3_ref[...] = x3
        _moe_prep(x3, gmoe_ref, wr_ref, h_ref, route_ref, route_t_ref)


def _attn_sample(x2, kc, vc, sinks, bias_c, bias_n, wts):
    n_tok = x2.shape[0]
    bd = kc.shape[0]
    st = SEQ_TILE
    assert bd % st == 0 and n_tok % bd == 0 and (n_tok // bd) * st <= LANES
    gkv, wkv, gk, gqa, wq, gq, ind, spread, wo, gmoe, wr = wts
    row = lambda n: _full((1, n))
    cache = pl.BlockSpec((st, WIN, KVW), lambda i: (i, 0, 0))
    return pl.pallas_call(
        functools.partial(_attn_sample_kernel, bd=bd),
        grid=(bd // st,),
        in_specs=[pl.BlockSpec(memory_space=pltpu.SMEM), _full((n_tok, D)), cache, cache,
                  row(D), _full((D, 2 * KVW)), row(KVW), row(D), _full((D, D)), row(D),
                  _full(ind.shape), _full(spread.shape), _full(bias_c.shape), _full(bias_n.shape), _full((D, D)), row(D), _full((D, LANES))],
        out_specs=[_full((n_tok, D)), _full((n_tok, D)), _full((n_tok, LANES)), _full((8, n_tok)),
                   _full((n_tok, KVW)), _full((n_tok, KVW))],
        out_shape=[jax.ShapeDtypeStruct((n_tok, D), f32),
                   jax.ShapeDtypeStruct((n_tok, D), f32),
                   jax.ShapeDtypeStruct((n_tok, LANES), f32),
                   jax.ShapeDtypeStruct((8, n_tok), f32),
                   jax.ShapeDtypeStruct((n_tok, KVW), f32),
                   jax.ShapeDtypeStruct((n_tok, KVW), f32)],
        scratch_shapes=[pltpu.VMEM((n_tok, D), f32), pltpu.VMEM((n_tok, D), f32)],
        compiler_params=_cp("arbitrary"),
        name="attn_sample",
    )(sinks, x2, kc, vc, gkv, wkv, gk, gqa, wq, gq, ind, spread, bias_c, bias_n, wo, gmoe, wr)


def _positions_kernel(rt_ref, pos_ref, off_ref, counts, running, offs):
    ph = pl.program_id(0)
    i = pl.program_id(1)
    n_sub, _, tok = pos_ref.shape
    sub = lax.broadcasted_iota(jnp.int32, (NE, tok), 0).astype(f32)

    def tile(s):
        e0 = rt_ref[0:1, s * tok:(s + 1) * tok]
        e1 = rt_ref[1:2, s * tok:(s + 1) * tok]
        sel = ((sub == e0) | (sub == e1)).astype(f32)
        return e0, e1, sel, jnp.broadcast_to(jnp.sum(sel, axis=-1, keepdims=True), (NE, LANES))

    @pl.when((ph == 0) & (i == 0))
    def _():
        counts[...] = jnp.zeros((NE, LANES), f32)

    @pl.when(ph == 0)
    def _():
        for s in range(n_sub):
            counts[...] += tile(s)[3]

    @pl.when((ph == 1) & (i == 0))
    def _():
        sub_l = lax.broadcasted_iota(jnp.int32, (NE, LANES), 0)
        acc = jnp.zeros((NE, LANES), f32)
        for e in range(NE - 1):
            acc = acc + jnp.where(sub_l > e, counts[e:e + 1, :], 0.0)
        offs[...] = acc
        running[...] = jnp.zeros((NE, LANES), f32)
        off_ref[...] = acc.astype(jnp.int32)

    @pl.when(ph == 1)
    def _():
        r_i = lax.broadcasted_iota(jnp.int32, (tok, tok), 0)
        c_i = lax.broadcasted_iota(jnp.int32, (tok, tok), 1)
        upper = (r_i < c_i).astype(bf16)
        for s in range(n_sub):
            e0, e1, sel, tile_cnt = tile(s)
            cum = _dot(sel.astype(bf16), upper)
            tot = cum + (offs[:, 0:1] + running[:, 0:1])
            p0 = jnp.sum(jnp.where(sub == e0, tot, 0.0), axis=0, keepdims=True)
            p1 = jnp.sum(jnp.where(sub == e1, tot, 0.0), axis=0, keepdims=True)
            pos_ref[s] = jnp.concatenate([p0, p1], axis=0).astype(jnp.int32)
            running[...] += tile_cnt


def _positions(route_t):
    n = route_t.shape[1]
    assert n % TOK == 0
    nt = n // TOK
    n_sub = max(d for d in range(1, 9) if nt % d == 0)
    return pl.pallas_call(
        _positions_kernel,
        grid=(2, nt // n_sub),
        in_specs=[pl.BlockSpec((8, n_sub * TOK), lambda p, i: (0, i))],
        out_specs=[pl.BlockSpec((n_sub, 2, TOK), lambda p, i: (i * p, 0, 0)), _full((NE, LANES))],
        out_shape=[jax.ShapeDtypeStruct((nt, 2, TOK), jnp.int32),
                   jax.ShapeDtypeStruct((NE, LANES), jnp.int32)],
        scratch_shapes=[pltpu.VMEM((NE, LANES), f32)] * 3,
        compiler_params=_cp("arbitrary", "arbitrary"),
        name="moe_positions",
    )(route_t)


def _row_copy(src, src_row, dst, dst_row, sem):
    return pltpu.make_async_copy(src.at[pl.ds(src_row, 1)], dst.at[pl.ds(dst_row, 1)], sem)


def _dispatch_kernel(pos_ref, hp_ref, hs_ref, xs_ref, sem, *, n_p_tiles):
    i = pl.program_id(0)
    tok = hp_ref.shape[0]

    def scatter_rows(h_ref):
        def issue(r, c):
            _row_copy(h_ref, r, xs_ref, pos_ref[r], sem).start(priority=0)
            _row_copy(h_ref, r, xs_ref, pos_ref[tok + r], sem).start(priority=1)
            return c

        lax.fori_loop(0, tok, issue, 0, unroll=ROW_DMA_UNROLL)
        for _ in range(2):
            pltpu.make_async_copy(h_ref, xs_ref.at[pl.ds(0, tok)], sem).wait()

    @pl.when(i < n_p_tiles)
    def _():
        scatter_rows(hp_ref)

    @pl.when(i >= n_p_tiles)
    def _():
        scatter_rows(hs_ref)


def _dispatch(pos, hp, hs):
    n_p, n_s = hp.shape[0], hs.shape[0]
    assert n_p % TOK == 0 and n_s % TOK == 0
    npt, nst = n_p // TOK, n_s // TOK
    return pl.pallas_call(
        functools.partial(_dispatch_kernel, n_p_tiles=npt),
        grid=(npt + nst,),
        in_specs=[pl.BlockSpec((2 * TOK,), lambda i: (i,), memory_space=pltpu.SMEM),
                  pl.BlockSpec((TOK, D), lambda i: (jnp.minimum(i, npt - 1), 0)),
                  pl.BlockSpec((TOK, D), lambda i: (jnp.maximum(i - npt, 0), 0))],
        out_specs=pl.BlockSpec(memory_space=pl.ANY),
        out_shape=jax.ShapeDtypeStruct((2 * (n_p + n_s), D), f32),
        scratch_shapes=[pltpu.SemaphoreType.DMA(())],
        compiler_params=_cp("arbitrary"),
        name="moe_dispatch",
    )(pos, hp, hs)


def _experts_kernel(tile_ref, exp_ref, lo_ref, hi_ref, xs_ref, wg_ref, wu_ref, wd_ref, o_ref):
    del tile_ref, exp_ref
    k = pl.program_id(0)
    lo = lo_ref[k]
    hi = hi_ref[k]

    @pl.when(hi > lo)
    def _():
        x = xs_ref[...].astype(bf16)
        ff = wg_ref.shape[2]
        acc = jnp.zeros(o_ref.shape, f32)
        for c0 in range(0, ff, FF_CHUNK):
            c1 = min(c0 + FF_CHUNK, ff)
            a = _dot(x, wg_ref[0, :, c0:c1])
            a = a * _sigmoid(a) * _dot(x, wu_ref[0, :, c0:c1])
            acc = acc + _dot(a.astype(bf16), wd_ref[0, c0:c1, :])
        @pl.when(lo == 0)
        def _():
            o_ref[...] = acc

        @pl.when(lo > 0)
        def _():
            row = lax.broadcasted_iota(jnp.int32, (o_ref.shape[0], 1), 0)
            o_ref[...] = jnp.where((row >= lo) & (row < hi), acc, o_ref[...])


def _experts(items, xs, wg, wu, wd):
    n_rows = xs.shape[0]
    ff = wg.shape[2]
    tm = TM_MOE
    assert n_rows % tm == 0
    n_items = items[0].shape[0]
    once = pl.Buffered(1)
    grid_spec = pltpu.PrefetchScalarGridSpec(
        num_scalar_prefetch=4,
        grid=(n_items,),
        in_specs=[pl.BlockSpec((tm, D), lambda k, ti, ex, lo, hi: (ti[k], 0)),
                  pl.BlockSpec((1, D, ff), lambda k, ti, ex, lo, hi: (ex[k], 0, 0), pipeline_mode=once),
                  pl.BlockSpec((1, D, ff), lambda k, ti, ex, lo, hi: (ex[k], 0, 0), pipeline_mode=once),
                  pl.BlockSpec((1, ff, D), lambda k, ti, ex, lo, hi: (ex[k], 0, 0), pipeline_mode=once)],
        out_specs=pl.BlockSpec((tm, D), lambda k, ti, ex, lo, hi: (ti[k], 0)))
    return pl.pallas_call(
        _experts_kernel,
        grid_spec=grid_spec,
        out_shape=jax.ShapeDtypeStruct((n_rows, D), f32),
        compiler_params=_cp("arbitrary"),
        name="moe_experts",
    )(*items, xs, wg, wu, wd)


def _work_items(off, n_rows):
    tm = TM_MOE
    n_tiles = n_rows // tm
    n_items = n_tiles + NE - 1
    start = off
    end = jnp.concatenate([off[1:], jnp.array([n_rows], jnp.int32)])
    cnt = end - start
    first = start // tm
    last = jnp.where(cnt > 0, (end - 1) // tm, first - 1)
    per = last - first + 1
    cum = jnp.cumsum(per)
    k = jnp.arange(n_items, dtype=jnp.int32)
    e = jnp.minimum(jnp.sum(k[:, None] >= cum[None, :], axis=1), NE - 1).astype(jnp.int32)
    tile = first[e] + (k - (cum[e] - per[e]))
    real = k < cum[NE - 1]
    e = jnp.where(real, e, jnp.max(jnp.where(real, e, 0)))
    tile = jnp.where(real, tile, n_tiles - 1).astype(jnp.int32)
    lo = jnp.clip(start[e] - tile * tm, 0, tm)
    hi = jnp.clip(end[e] - tile * tm, 0, tm)
    lo = jnp.where(real, lo, 0).astype(jnp.int32)
    hi = jnp.where(real, hi, 0).astype(jnp.int32)
    return tile, e, lo, hi


def _combine_kernel(pos_ref, pos_next_ref, x3_ref, route_ref, o_hbm, y_ref, buf, sem, *, n_tiles):
    i = pl.program_id(0)
    tok = x3_ref.shape[0]
    slot = i % 2

    def gather(p_ref, sl):
        def issue(r, c):
            _row_copy(o_hbm, p_ref[r], buf.at[sl, 0], r, sem.at[sl]).start(priority=0)
            _row_copy(o_hbm, p_ref[tok + r], buf.at[sl, 1], r, sem.at[sl]).start(priority=1)
            return c

        lax.fori_loop(0, tok, issue, 0, unroll=ROW_DMA_UNROLL)

    @pl.when(i == 0)
    def _():
        gather(pos_ref, slot)

    if n_tiles > 1:
        @pl.when(i + 1 < n_tiles)
        def _():
            gather(pos_next_ref, 1 - slot)

    for s in range(2):
        pltpu.make_async_copy(o_hbm.at[pl.ds(0, tok)], buf.at[slot, s], sem.at[slot]).wait()
    route = route_ref[...]
    y_ref[...] = x3_ref[...] + route[:, 2:3] * buf[slot, 0] + route[:, 3:4] * buf[slot, 1]


def _combine(pos, x3, route, o_sorted, tile0):
    n = x3.shape[0]
    assert n % TOK == 0
    nt = n // TOK
    pos_spec = lambda d: pl.BlockSpec((2 * TOK,), lambda i: (jnp.minimum(i + d, nt - 1) + tile0,),
                                      memory_space=pltpu.SMEM)
    return pl.pallas_call(
        functools.partial(_combine_kernel, n_tiles=nt),
        grid=(nt,),
        in_specs=[pos_spec(0), pos_spec(1),
                  pl.BlockSpec((TOK, D), lambda i: (i, 0)),
                  pl.BlockSpec((TOK, LANES), lambda i: (i, 0)),
                  pl.BlockSpec(memory_space=pl.ANY)],
        out_specs=pl.BlockSpec((TOK, D), lambda i: (i, 0)),
        out_shape=jax.ShapeDtypeStruct((n, D), f32),
        scratch_shapes=[pltpu.VMEM((2, 2, TOK, D), f32), pltpu.SemaphoreType.DMA((2,))],
        compiler_params=_cp("arbitrary"),
        name="moe_combine",
    )(pos, pos, x3, route, o_sorted)


def kernel(x_prompt, x_sample, state_conv, cache_k_win, cache_v_win, g_conv_norm, w_pw1, b_pw1, w_dw, b_dw, g_ln, b_ln, w_pw2, b_pw2, g_kv_norm, w_kv, g_k_norm, g_attn_norm, w_q, g_q_norm, sinks, w_o, rel_bias, g_ffn_norm, w_gate, w_up, w_down, g_moe_norm, w_router, w_e_gate, w_e_up, w_e_down):
    b, t, _ = x_prompt.shape
    bd, s_len, _ = x_sample.shape
    n_p = b * t
    n_s = bd * s_len
    row = lambda a: a.reshape(1, -1).astype(f32)

    conv_w = (row(g_conv_norm[0]), w_pw1[0].astype(bf16), row(b_pw1[0]),
              jnp.pad(w_dw[0], ((0, HALO - CONV_W), (0, 0))), row(b_dw[0]), row(g_ln[0]), row(b_ln[0]),
              w_pw2[0].astype(bf16), row(b_pw2[0]))
    ffn_w = (row(g_ffn_norm[0]), w_gate[0].astype(bf16), w_up[0].astype(bf16), w_down[0].astype(bf16))
    wq = w_q[0].reshape(D, NKV, G, DH).transpose(0, 2, 1, 3).reshape(D, NH * DH).astype(bf16)
    wo = w_o[0].reshape(NKV, G, DH, D).transpose(1, 0, 2, 3).reshape(NH * DH, D).astype(bf16)
    wr = jnp.pad(w_router[0], ((0, 0), (0, LANES - NE))).astype(bf16)
    head_of_lane = np.repeat(np.eye(NH, LANES), DH, axis=0)
    attn_w = (row(g_kv_norm), w_kv.astype(bf16), row(jnp.tile(g_k_norm, NKV)), row(g_attn_norm[0]),
              wq, row(jnp.tile(g_q_norm[0], NH)), jnp.asarray(head_of_lane, bf16),
              jnp.asarray(head_of_lane.T, bf16), wo, row(g_moe_norm[0]), wr)
    weg, weu, wed = w_e_gate[0].astype(bf16), w_e_up[0].astype(bf16), w_e_down[0].astype(bf16)
    sink = sinks[0].astype(f32)

    bias_p = _bias_table(rel_bias, _prompt_bucket_map()).reshape(2, NKV, G * WIN, 2 * WIN)
    map_c, map_n = _sample_bucket_maps(s_len, SEQ_TILE)
    rows = s_len * SEQ_TILE
    bias_c = _bias_table(rel_bias, map_c).reshape(NH * rows, SEQ_TILE * WIN)
    bias_n = _bias_table(rel_bias, map_n).reshape(NH * rows, LANES)

    x1p, ulast = _conv_prompt(x_prompt, conv_w)
    x2p = _ffn(x1p.reshape(n_p, D), *ffn_w).reshape(b, t, D)
    x3p, hp, route_p, route_tp, klast, vlast = _attn_prompt(x2p, sink, bias_p, attn_w)

    xs_tb = x_sample.transpose(1, 0, 2)
    x1s, u_s = _conv_sample(xs_tb, state_conv[0].transpose(1, 0, 2), conv_w)
    x2s = _ffn(x1s.reshape(n_s, D), *ffn_w)
    kc = cache_k_win.reshape(bd, WIN, KVW)
    vc = cache_v_win.reshape(bd, WIN, KVW)
    x3s, hs, route_s, route_ts, knew, vnew = _attn_sample(x2s, kc, vc, sink, bias_c, bias_n, attn_w)

    pos, off = _positions(jnp.concatenate([route_tp, route_ts], axis=1))
    pos = pos.reshape(-1)
    n_rows = 2 * (n_p + n_s)
    xs = _dispatch(pos, hp, hs)
    o_sorted = _experts(_work_items(off[:, 0], n_rows), xs, weg, weu, wed)
    y_p = _combine(pos, x3p.reshape(n_p, D), route_p, o_sorted, 0).reshape(b, t, D)
    y_s = _combine(pos, x3s, route_s, o_sorted, n_p // TOK).reshape(s_len, bd, D).transpose(1, 0, 2)

    n_prev = CONV_W - 1
    conv_p = ulast[:, HALO - n_prev:, :][None]
    conv_s = jnp.concatenate([state_conv[:, :, s_len:, :], u_s.transpose(1, 0, 2)[None]], axis=2)
    k_p = klast.reshape(b, WIN, NKV, DH)
    v_p = vlast.reshape(b, WIN, NKV, DH)
    k_new = knew.reshape(s_len, bd, NKV, DH).transpose(1, 0, 2, 3)
    v_new = vnew.reshape(s_len, bd, NKV, DH).transpose(1, 0, 2, 3)
    k_s = jnp.concatenate([cache_k_win[:, s_len:], k_new], axis=1)
    v_s = jnp.concatenate([cache_v_win[:, s_len:], v_new], axis=1)
    return (y_p, y_s, conv_p, conv_s, k_p, k_s, v_p, v_s)
```

```python
import functools

import numpy as np
import jax
import jax.numpy as jnp
from jax import lax
from jax.experimental import pallas as pl
from jax.experimental.pallas import tpu as pltpu

D = 1024
CONV_W = 31
HALO = 32
DH = 64
NH = 16
NKV = 4
G = NH // NKV
KVW = NKV * DH
WIN = 128
NB = 32
MAXD = 128
NE = 8
EPS = 1e-6
LANES = 128
NEG = float("-inf")

TT_CONV = 512
TM_FFN = 1024
TQ_ATT = 512
TOK = 512
TM_MOE = 512
ROW_DMA_UNROLL = 16
SEQ_TILE = 8
SEQ_TILE_CONV = 32
FF_CHUNK = 1792
FF_CHUNK_MOE = 512
VMEM_LIMIT = 56 * 1024 * 1024

bf16 = jnp.bfloat16
f32 = jnp.float32


def _cp(*sem):
    return pltpu.CompilerParams(dimension_semantics=sem, vmem_limit_bytes=VMEM_LIMIT)


def _full(shape, once=False):
    n = len(shape)
    return pl.BlockSpec(shape, lambda *_: (0,) * n, pipeline_mode=pl.Buffered(1) if once else None)


def _sigmoid(x):
    return 1.0 / (1.0 + jnp.exp(-x))


def _rms(x):
    return x * lax.rsqrt(jnp.mean(x * x, axis=-1, keepdims=True) + EPS)


def _dot(a, b):
    return jnp.dot(a, b, preferred_element_type=f32)


def _dot_nt(a, b):
    return lax.dot_general(a, b, (((1,), (1,)), ((), ())), preferred_element_type=f32)


def _split_dot(x, w):
    hi = x.astype(bf16)
    lo = (x - hi.astype(f32)).astype(bf16)
    return _dot(hi, w) + _dot(lo, w)


def _head_rms(x, gain, ind_ref, spread_ref):
    c = x.shape[1]
    ss = _dot((x * x).astype(bf16), ind_ref[0:c, :])
    r = lax.rsqrt(ss * (1.0 / DH) + EPS)
    return x * _split_dot(r, spread_ref[:, 0:c]) * gain


def _kv_lane_mask(kv):
    lane = lax.broadcasted_iota(jnp.int32, (1, KVW), 1)
    return (lane >= kv * DH) & (lane < (kv + 1) * DH)


def _bucket_of(dist):
    n = np.maximum(dist, 0)
    max_exact = NB // 2
    large = max_exact + (np.log(np.maximum(n, 1) / max_exact) / np.log(MAXD / max_exact)
                         * (NB - max_exact)).astype(np.int32)
    large = np.minimum(large, NB - 1)
    return np.where(n < max_exact, n, large).astype(np.int32)


def _prompt_bucket_map():
    dist = np.arange(WIN)[:, None] + WIN - np.arange(2 * WIN)[None, :]
    valid = (dist >= 0) & (dist < WIN)
    bk = np.where(valid, _bucket_of(dist), -1).astype(np.int32)
    first = bk.copy()
    first[:, :WIN] = -1
    return np.stack([first, bk])


def _sample_bucket_maps(s_len, st):
    r_t = np.repeat(np.arange(s_len), st)
    r_b = np.tile(np.arange(st), s_len)
    c_b = np.repeat(np.arange(st), WIN)
    c_s = np.tile(np.arange(WIN), st)
    dist = r_t[:, None] + WIN - c_s[None, :]
    valid = (r_b[:, None] == c_b[None, :]) & (dist >= 0) & (dist < WIN)
    map_c = np.where(valid, _bucket_of(dist), -1).astype(np.int32)
    dist_n = r_t[:, None] - r_t[None, :]
    valid_n = (r_b[:, None] == r_b[None, :]) & (dist_n >= 0) & (dist_n < WIN)
    map_n = np.full((s_len * st, LANES), -1, np.int32)
    map_n[:, :s_len * st] = np.where(valid_n, _bucket_of(dist_n), -1)
    return map_c[None], map_n[None]


def _bias_kernel(rb_ref, map_ref, out_ref):
    bk = map_ref[0]
    for h in range(NH):
        acc = jnp.full(bk.shape, NEG, f32)
        for b in range(NB):
            acc = jnp.where(bk == b, rb_ref[b, h], acc)
        out_ref[0, h] = acc


def _bias_table(rel_bias, bucket_map):
    v, r, c = bucket_map.shape
    return pl.pallas_call(
        _bias_kernel,
        grid=(v,),
        in_specs=[pl.BlockSpec(memory_space=pltpu.SMEM),
                  pl.BlockSpec((1, r, c), lambda i: (i, 0, 0))],
        out_specs=pl.BlockSpec((1, NH, r, c), lambda i: (i, 0, 0, 0)),
        out_shape=jax.ShapeDtypeStruct((v, NH, r, c), f32),
        compiler_params=_cp("arbitrary"),
        name="bias_table",
    )(rel_bias, jnp.asarray(bucket_map))


def _glu_rows(x, gn_ref, w1_ref, b1_ref):
    h = (_rms(x) * gn_ref[...]).astype(bf16)
    u2 = _dot(h, w1_ref[...]) + b1_ref[...]
    return u2[:, :D] * _sigmoid(u2[:, D:])


def _conv_tail(x, c, gln_ref, bln_ref, w2_ref, b2_ref):
    mu = jnp.mean(c, axis=-1, keepdims=True)
    xc = c - mu
    var = jnp.mean(xc * xc, axis=-1, keepdims=True)
    y = xc * lax.rsqrt(var + EPS) * gln_ref[...] + bln_ref[...]
    y = y * _sigmoid(y)
    return x + _dot(y.astype(bf16), w2_ref[...]) + b2_ref[...]


def _swiglu_rows(x, g_ref, wg_ref, wu_ref, wd_ref):
    h = (_rms(x) * g_ref[...]).astype(bf16)
    ff = wg_ref.shape[1]
    acc = x
    for c0 in range(0, ff, FF_CHUNK):
        c1 = min(c0 + FF_CHUNK, ff)
        a = _dot(h, wg_ref[:, c0:c1])
        a = a * _sigmoid(a) * _dot(h, wu_ref[:, c0:c1])
        acc = acc + _dot(a.astype(bf16), wd_ref[c0:c1, :])
    return acc


def _conv_prompt_kernel(x_ref, gn_ref, w1_ref, b1_ref, wdw_ref, bdw_ref, gln_ref, bln_ref,
                        w2_ref, b2_ref, x1_ref, ulast_ref, ubuf, ush):
    tt = x_ref.shape[1]
    t = pl.program_id(1)
    x = x_ref[0]
    u = _glu_rows(x, gn_ref, w1_ref, b1_ref)

    @pl.when(t == 0)
    def _():
        ubuf[0:HALO, :] = jnp.zeros((HALO, D), f32)

    @pl.when(t > 0)
    def _():
        ubuf[0:HALO, :] = ubuf[tt:tt + HALO, :]

    ubuf[HALO:HALO + tt, :] = u
    ulast_ref[0] = u[tt - HALO:, :]
    n_sh = ush.shape[1]
    for r in range(1, 8):
        ush[r - 1] = ubuf[r:r + n_sh, :]
    off = HALO - (CONV_W - 1)
    c = jnp.broadcast_to(bdw_ref[...], (tt, D))
    for k in range(CONV_W):
        a, r = divmod(off + k, 8)
        src = ubuf[8 * a:8 * a + tt, :] if r == 0 else ush[r - 1, 8 * a:8 * a + tt, :]
        c = c + wdw_ref[k:k + 1, :] * src
    x1_ref[0] = _conv_tail(x, c, gln_ref, bln_ref, w2_ref, b2_ref)


def _conv_prompt(x, wts):
    b, t, _ = x.shape
    tt = min(TT_CONV, t)
    assert t % tt == 0 and tt >= HALO
    row = lambda n: _full((1, n))
    return pl.pallas_call(
        _conv_prompt_kernel,
        grid=(b, t // tt),
        in_specs=[pl.BlockSpec((1, tt, D), lambda i, j: (i, j, 0)),
                  row(D), _full((D, 2 * D)), row(2 * D), _full((HALO, D)), row(D), row(D), row(D),
                  _full((D, D)), row(D)],
        out_specs=[pl.BlockSpec((1, tt, D), lambda i, j: (i, j, 0)),
                   pl.BlockSpec((1, HALO, D), lambda i, j: (i, 0, 0))],
        out_shape=[jax.ShapeDtypeStruct((b, t, D), f32),
                   jax.ShapeDtypeStruct((b, HALO, D), f32)],
        scratch_shapes=[pltpu.VMEM((tt + HALO, D), f32),
                        pltpu.VMEM((7, tt + HALO - 8, D), f32)],
        compiler_params=_cp("arbitrary", "arbitrary"),
        name="conv_prompt",
    )(x, *wts)


def _conv_sample_kernel(x_ref, st_ref, gn_ref, w1_ref, b1_ref, wdw_ref, bdw_ref, gln_ref, bln_ref,
                        w2_ref, b2_ref, x1_ref, u_ref):
    s_len, bs, _ = x_ref.shape
    x = x_ref[...].reshape(s_len * bs, D)
    u = _glu_rows(x, gn_ref, w1_ref, b1_ref)
    u_ref[...] = u.reshape(s_len, bs, D)
    n_prev = CONV_W - 1
    cs = []
    for t in range(s_len):
        c = jnp.broadcast_to(bdw_ref[...], (bs, D))
        for j in range(t, n_prev):
            c = c + wdw_ref[j - t:j - t + 1, :] * st_ref[j]
        for i in range(t + 1):
            k = n_prev - t + i
            c = c + wdw_ref[k:k + 1, :] * u[i * bs:(i + 1) * bs, :]
        cs.append(c)
    c = jnp.concatenate(cs, axis=0)
    x1_ref[...] = _conv_tail(x, c, gln_ref, bln_ref, w2_ref, b2_ref).reshape(s_len, bs, D)


def _conv_sample(x_tb, state_t, wts):
    s_len, bd, _ = x_tb.shape
    bs = min(SEQ_TILE_CONV, bd)
    assert bd % bs == 0 and bs % 8 == 0 and s_len <= CONV_W - 1
    row = lambda n: _full((1, n))
    blk = pl.BlockSpec((s_len, bs, D), lambda i: (0, i, 0))
    return pl.pallas_call(
        _conv_sample_kernel,
        grid=(bd // bs,),
        in_specs=[blk, pl.BlockSpec((CONV_W - 1, bs, D), lambda i: (0, i, 0)),
                  row(D), _full((D, 2 * D)), row(2 * D), _full((HALO, D)), row(D), row(D), row(D),
                  _full((D, D)), row(D)],
        out_specs=[blk, blk],
        out_shape=[jax.ShapeDtypeStruct((s_len, bd, D), f32)] * 2,
        compiler_params=_cp("arbitrary"),
        name="conv_sample",
    )(x_tb, state_t, *wts)


def _ffn_kernel(x_ref, g_ref, wg_ref, wu_ref, wd_ref, o_ref):
    o_ref[...] = _swiglu_rows(x_ref[...], g_ref, wg_ref, wu_ref, wd_ref)


def _ffn(x, g, wg, wu, wd):
    n = x.shape[0]
    tm = min(TM_FFN, n)
    assert n % tm == 0
    ff = wg.shape[1]
    return pl.pallas_call(
        _ffn_kernel,
        grid=(n // tm,),
        in_specs=[pl.BlockSpec((tm, D), lambda i: (i, 0)), _full((1, D)),
                  _full((D, ff)), _full((D, ff)), _full((ff, D))],
        out_specs=pl.BlockSpec((tm, D), lambda i: (i, 0)),
        out_shape=jax.ShapeDtypeStruct((n, D), f32),
        compiler_params=_cp("arbitrary"),
        name="dense_ffn",
    )(x, g, wg, wu, wd)


def _qkv(x, gkv_ref, wkv_ref, gk_ref, gq_attn_ref, wq_ref, gq_ref, ind_ref, spread_ref):
    xn = _rms(x)
    kv = _dot((xn * gkv_ref[...]).astype(bf16), wkv_ref[...])
    k = _head_rms(kv[:, :KVW], gk_ref[...], ind_ref, spread_ref)
    v = kv[:, KVW:]
    q = _dot((xn * gq_attn_ref[...]).astype(bf16), wq_ref[...])
    q = _head_rms(q, gq_ref[...], ind_ref, spread_ref) * (DH ** -0.5)
    return q, k, v


def _sink_col(sinks_ref, kv, rows_per_head):
    return jnp.concatenate(
        [jnp.full((rows_per_head, 1), sinks_ref[kv * G + g], f32) for g in range(G)], axis=0)


def _moe_prep(x3, gmoe_ref, wr_ref, h_ref, route_ref, route_t_ref):
    h = _rms(x3) * gmoe_ref[...]
    h_ref[...] = h
    logits = _dot(h.astype(bf16), wr_ref[...])
    lane = lax.broadcasted_iota(jnp.int32, logits.shape, 1).astype(f32)
    logits = jnp.where(lane < NE, logits, NEG)
    m1 = jnp.max(logits, axis=-1, keepdims=True)
    i1 = jnp.min(jnp.where(logits == m1, lane, float(LANES)), axis=-1, keepdims=True)
    rest = jnp.where(lane == i1, NEG, logits)
    m2 = jnp.max(rest, axis=-1, keepdims=True)
    i2 = jnp.min(jnp.where(rest == m2, lane, float(LANES)), axis=-1, keepdims=True)
    e2 = jnp.exp(m2 - m1)
    den = 1.0 + e2
    g1 = 1.0 / den
    g2 = e2 / den
    route = jnp.where(lane == 0, i1,
                      jnp.where(lane == 1, i2,
                                jnp.where(lane == 2, g1, jnp.where(lane == 3, g2, 0.0))))
    route_ref[...] = route
    route_t_ref[...] = route.T[0:8, :]


def _attn_prompt_kernel(sinks_ref, x_ref, gkv_ref, wkv_ref, gk_ref, gqa_ref, wq_ref, gq_ref,
                        ind_ref, spread_ref, ones_ref, bias_ref, wo_ref, gmoe_ref, wr_ref,
                        x3_ref, h_ref, route_ref, route_t_ref, klast_ref, vlast_ref,
                        kbuf, vbuf, obuf):
    tq = x_ref.shape[1]
    t = pl.program_id(1)
    x = x_ref[0]
    q, k, v = _qkv(x, gkv_ref, wkv_ref, gk_ref, gqa_ref, wq_ref, gq_ref, ind_ref, spread_ref)
    klast_ref[0] = k[tq - WIN:, :]
    vlast_ref[0] = v[tq - WIN:, :]

    @pl.when(t == 0)
    def _():
        kbuf[:, 0:WIN, :] = jnp.zeros((NKV, WIN, KVW), bf16)
        vbuf[:, 0:WIN, :] = jnp.zeros((NKV, WIN, KVW), bf16)

    @pl.when(t > 0)
    def _():
        kbuf[:, 0:WIN, :] = kbuf[:, tq:tq + WIN, :]
        vbuf[:, 0:WIN, :] = vbuf[:, tq:tq + WIN, :]

    kb = k.astype(bf16)
    vb = v.astype(bf16)
    for kv in range(NKV):
        m = _kv_lane_mask(kv)
        kbuf[kv, WIN:WIN + tq, :] = jnp.where(m, kb, jnp.zeros_like(kb))
        vbuf[kv, WIN:WIN + tq, :] = jnp.where(m, vb, jnp.zeros_like(vb))

    qb = q.astype(bf16)
    for n in range(tq // WIN):
        r0 = n * WIN
        lhs = jnp.concatenate([qb[r0:r0 + WIN, g * KVW:(g + 1) * KVW] for g in range(G)], axis=0)
        ps = []
        lane = lax.broadcasted_iota(jnp.int32, (1, LANES), 1)
        sink_term = jnp.zeros((G * WIN, LANES), f32)
        for kv in range(NKV):
            s = _dot_nt(lhs, kbuf[kv, r0:r0 + 2 * WIN, :])
            s = s + bias_ref[jnp.where(t == 0, 0, 1) if n == 0 else 1, kv]
            sink = _sink_col(sinks_ref, kv, WIN)
            mx = jnp.maximum(jnp.max(s, axis=-1, keepdims=True), sink)
            ps.append(jnp.exp(s - mx).astype(bf16))
            sink_term = jnp.where(lane == kv, jnp.exp(sink - mx), sink_term)
        pcat = jnp.concatenate(ps, axis=-1)
        vcat = jnp.concatenate([vbuf[kv, r0:r0 + 2 * WIN, :] for kv in range(NKV)], axis=0)
        den = _dot(pcat, ones_ref[...]) + sink_term
        inv = 1.0 / jnp.where(lane < NKV, den, 1.0)
        pv = _dot(pcat, vcat) * _split_dot(inv, spread_ref[:, 0:KVW])
        obuf[r0:r0 + WIN, :] = jnp.concatenate(
            [pv[g * WIN:(g + 1) * WIN, :] for g in range(G)], axis=-1).astype(bf16)

    x3 = x + _dot(obuf[...], wo_ref[...])
    x3_ref[0] = x3
    _moe_prep(x3, gmoe_ref, wr_ref, h_ref, route_ref, route_t_ref)


def _attn_prompt(x2, sinks, bias, wts):
    b, t, _ = x2.shape
    tq = min(TQ_ATT, t)
    assert t % tq == 0 and tq % WIN == 0
    nt = t // tq
    gkv, wkv, gk, gqa, wq, gq, ind, spread, wo, gmoe, wr = wts
    ones = jnp.asarray(np.repeat(np.eye(NKV, LANES), 2 * WIN, axis=0), bf16)
    row = lambda n: _full((1, n))
    tok = lambda w: pl.BlockSpec((tq, w), lambda i, j: (i * nt + j, 0))
    last = pl.BlockSpec((1, WIN, KVW), lambda i, j: (i, 0, 0))
    return pl.pallas_call(
        _attn_prompt_kernel,
        grid=(b, nt),
        in_specs=[pl.BlockSpec(memory_space=pltpu.SMEM),
                  pl.BlockSpec((1, tq, D), lambda i, j: (i, j, 0)),
                  row(D), _full((D, 2 * KVW)), row(KVW), row(D), _full((D, D)), row(D),
                  _full(ind.shape), _full(spread.shape), _full(ones.shape),
                  _full(bias.shape), _full((D, D)), row(D), _full((D, LANES))],
        out_specs=[pl.BlockSpec((1, tq, D), lambda i, j: (i, j, 0)), tok(D), tok(LANES),
                   pl.BlockSpec((8, tq), lambda i, j: (0, i * nt + j)), last, last],
        out_shape=[jax.ShapeDtypeStruct((b, t, D), f32),
                   jax.ShapeDtypeStruct((b * t, D), f32),
                   jax.ShapeDtypeStruct((b * t, LANES), f32),
                   jax.ShapeDtypeStruct((8, b * t), f32),
                   jax.ShapeDtypeStruct((b, WIN, KVW), f32),
                   jax.ShapeDtypeStruct((b, WIN, KVW), f32)],
        scratch_shapes=[pltpu.VMEM((NKV, tq + WIN, KVW), bf16),
                        pltpu.VMEM((NKV, tq + WIN, KVW), bf16),
                        pltpu.VMEM((tq, D), bf16)],
        compiler_params=_cp("arbitrary", "arbitrary"),
        name="attn_prompt",
    )(sinks, x2, gkv, wkv, gk, gqa, wq, gq, ind, spread, ones, bias, wo, gmoe, wr)


def _attn_sample_kernel(sinks_ref, x_ref, kc_ref, vc_ref, gkv_ref, wkv_ref, gk_ref, gqa_ref, wq_ref,
                        gq_ref, ind_ref, spread_ref, bias_c_ref, bias_n_ref, wo_ref, gmoe_ref, wr_ref,
                        x3_ref, h_ref, route_ref, route_t_ref, knew_ref, vnew_ref,
                        qbuf, obuf, *, bd):
    st = kc_ref.shape[0]
    s_len = x_ref.shape[0] // bd
    rows = s_len * st
    i = pl.program_id(0)

    @pl.when(i == 0)
    def _():
        q, k, v = _qkv(x_ref[...], gkv_ref, wkv_ref, gk_ref, gqa_ref, wq_ref, gq_ref, ind_ref, spread_ref)
        knew_ref[...] = k
        vnew_ref[...] = v
        qbuf[...] = q

    def tile_rows(buf):
        return jnp.concatenate(
            [buf[pl.ds(pl.multiple_of(t * bd + i * st, 8), st), :] for t in range(s_len)],
            axis=0).astype(bf16)

    kbuf, vbuf = knew_ref, vnew_ref

    qt = tile_rows(qbuf)
    lhs = jnp.concatenate([qt[:, g * KVW:(g + 1) * KVW] for g in range(G)], axis=0)
    pad = jnp.zeros((LANES - rows, KVW), bf16)
    kn = jnp.concatenate([tile_rows(kbuf), pad], axis=0)
    vn = jnp.concatenate([tile_rows(vbuf), pad], axis=0)
    kc = kc_ref[...].reshape(st * WIN, KVW).astype(bf16)
    vc = vc_ref[...].reshape(st * WIN, KVW).astype(bf16)
    pv = jnp.zeros((G * rows, KVW), f32)
    for kv in range(NKV):
        m = _kv_lane_mask(kv)
        zc = jnp.zeros_like(kc)
        zn = jnp.zeros_like(kn)
        b0 = kv * G * rows
        s_c = _dot_nt(lhs, jnp.where(m, kc, zc)) + bias_c_ref[b0:b0 + G * rows, :]
        s_n = _dot_nt(lhs, jnp.where(m, kn, zn)) + bias_n_ref[b0:b0 + G * rows, :]
        sink = _sink_col(sinks_ref, kv, rows)
        mx = jnp.maximum(jnp.maximum(jnp.max(s_c, axis=-1, keepdims=True),
                                     jnp.max(s_n, axis=-1, keepdims=True)), sink)
        p_c = jnp.exp(s_c - mx)
        p_n = jnp.exp(s_n - mx)
        den = (jnp.sum(p_c, axis=-1, keepdims=True) + jnp.sum(p_n, axis=-1, keepdims=True)
               + jnp.exp(sink - mx))
        inv = 1.0 / den
        pv = pv + _dot((p_c * inv).astype(bf16), jnp.where(m, vc, zc))
        pv = pv + _dot((p_n * inv).astype(bf16), jnp.where(m, vn, zn))
    o = jnp.concatenate([pv[g * rows:(g + 1) * rows, :] for g in range(G)], axis=-1)
    for t in range(s_len):
        obuf[pl.ds(pl.multiple_of(t * bd + i * st, 8), st), :] = o[t * st:(t + 1) * st, :]

    @pl.when(i == pl.num_programs(0) - 1)
    def _():
        x3 = x_ref[...] + _dot(obuf[...].astype(bf16), wo_ref[...])
        x3_ref[...] = x3
        _moe_prep(x3, gmoe_ref, wr_ref, h_ref, route_ref, route_t_ref)


def _attn_sample(x2, kc, vc, sinks, bias_c, bias_n, wts):
    n_tok = x2.shape[0]
    bd = kc.shape[0]
    st = SEQ_TILE
    assert bd % st == 0 and n_tok % bd == 0 and (n_tok // bd) * st <= LANES
    gkv, wkv, gk, gqa, wq, gq, ind, spread, wo, gmoe, wr = wts
    row = lambda n: _full((1, n))
    cache = pl.BlockSpec((st, WIN, KVW), lambda i: (i, 0, 0))
    return pl.pallas_call(
        functools.partial(_attn_sample_kernel, bd=bd),
        grid=(bd // st,),
        in_specs=[pl.BlockSpec(memory_space=pltpu.SMEM), _full((n_tok, D)), cache, cache,
                  row(D), _full((D, 2 * KVW)), row(KVW), row(D), _full((D, D)), row(D),
                  _full(ind.shape), _full(spread.shape), _full(bias_c.shape), _full(bias_n.shape), _full((D, D)), row(D), _full((D, LANES))],
        out_specs=[_full((n_tok, D)), _full((n_tok, D)), _full((n_tok, LANES)), _full((8, n_tok)),
                   _full((n_tok, KVW)), _full((n_tok, KVW))],
        out_shape=[jax.ShapeDtypeStruct((n_tok, D), f32),
                   jax.ShapeDtypeStruct((n_tok, D), f32),
                   jax.ShapeDtypeStruct((n_tok, LANES), f32),
                   jax.ShapeDtypeStruct((8, n_tok), f32),
                   jax.ShapeDtypeStruct((n_tok, KVW), f32),
                   jax.ShapeDtypeStruct((n_tok, KVW), f32)],
        scratch_shapes=[pltpu.VMEM((n_tok, D), f32), pltpu.VMEM((n_tok, D), f32)],
        compiler_params=_cp("arbitrary"),
        name="attn_sample",
    )(sinks, x2, kc, vc, gkv, wkv, gk, gqa, wq, gq, ind, spread, bias_c, bias_n, wo, gmoe, wr)


def _positions_kernel(rt_ref, pos_ref, off_ref, counts, running, offs):
    ph = pl.program_id(0)
    i = pl.program_id(1)
    n_sub, _, tok = pos_ref.shape
    sub = lax.broadcasted_iota(jnp.int32, (NE, tok), 0).astype(f32)

    def tile(s):
        e0 = rt_ref[0:1, s * tok:(s + 1) * tok]
        e1 = rt_ref[1:2, s * tok:(s + 1) * tok]
        sel = ((sub == e0) | (sub == e1)).astype(f32)
        return e0, e1, sel, jnp.broadcast_to(jnp.sum(sel, axis=-1, keepdims=True), (NE, LANES))

    @pl.when((ph == 0) & (i == 0))
    def _():
        counts[...] = jnp.zeros((NE, LANES), f32)

    @pl.when(ph == 0)
    def _():
        for s in range(n_sub):
            counts[...] += tile(s)[3]

    @pl.when((ph == 1) & (i == 0))
    def _():
        sub_l = lax.broadcasted_iota(jnp.int32, (NE, LANES), 0)
        acc = jnp.zeros((NE, LANES), f32)
        for e in range(NE - 1):
            acc = acc + jnp.where(sub_l > e, counts[e:e + 1, :], 0.0)
        offs[...] = acc
        running[...] = jnp.zeros((NE, LANES), f32)
        off_ref[...] = acc.astype(jnp.int32)

    @pl.when(ph == 1)
    def _():
        r_i = lax.broadcasted_iota(jnp.int32, (tok, tok), 0)
        c_i = lax.broadcasted_iota(jnp.int32, (tok, tok), 1)
        upper = (r_i < c_i).astype(bf16)
        for s in range(n_sub):
            e0, e1, sel, tile_cnt = tile(s)
            cum = _dot(sel.astype(bf16), upper)
            tot = cum + (offs[:, 0:1] + running[:, 0:1])
            p0 = jnp.sum(jnp.where(sub == e0, tot, 0.0), axis=0, keepdims=True)
            p1 = jnp.sum(jnp.where(sub == e1, tot, 0.0), axis=0, keepdims=True)
            pos_ref[s] = jnp.concatenate([p0, p1], axis=0).astype(jnp.int32)
            running[...] += tile_cnt


def _positions(route_t):
    n = route_t.shape[1]
    assert n % TOK == 0
    nt = n // TOK
    n_sub = max(d for d in range(1, 9) if nt % d == 0)
    return pl.pallas_call(
        _positions_kernel,
        grid=(2, nt // n_sub),
        in_specs=[pl.BlockSpec((8, n_sub * TOK), lambda p, i: (0, i))],
        out_specs=[pl.BlockSpec((n_sub, 2, TOK), lambda p, i: (i * p, 0, 0)), _full((NE, LANES))],
        out_shape=[jax.ShapeDtypeStruct((nt, 2, TOK), jnp.int32),
                   jax.ShapeDtypeStruct((NE, LANES), jnp.int32)],
        scratch_shapes=[pltpu.VMEM((NE, LANES), f32)] * 3,
        compiler_params=_cp("arbitrary", "arbitrary"),
        name="moe_positions",
    )(route_t)


def _row_copy(src, src_row, dst, dst_row, sem):
    return pltpu.make_async_copy(src.at[pl.ds(src_row, 1)], dst.at[pl.ds(dst_row, 1)], sem)


def _dispatch_kernel(pos_ref, hp_ref, hs_ref, xs_ref, sem, *, n_p_tiles):
    i = pl.program_id(0)
    tok = hp_ref.shape[0]

    def scatter_rows(h_ref):
        def issue(r, c):
            _row_copy(h_ref, r, xs_ref, pos_ref[r], sem).start(priority=0)
            _row_copy(h_ref, r, xs_ref, pos_ref[tok + r], sem).start(priority=1)
            return c

        lax.fori_loop(0, tok, issue, 0, unroll=ROW_DMA_UNROLL)
        for _ in range(2):
            pltpu.make_async_copy(h_ref, xs_ref.at[pl.ds(0, tok)], sem).wait()

    @pl.when(i < n_p_tiles)
    def _():
        scatter_rows(hp_ref)

    @pl.when(i >= n_p_tiles)
    def _():
        scatter_rows(hs_ref)


def _dispatch(pos, hp, hs):
    n_p, n_s = hp.shape[0], hs.shape[0]
    assert n_p % TOK == 0 and n_s % TOK == 0
    npt, nst = n_p // TOK, n_s // TOK
    return pl.pallas_call(
        functools.partial(_dispatch_kernel, n_p_tiles=npt),
        grid=(npt + nst,),
        in_specs=[pl.BlockSpec((2 * TOK,), lambda i: (i,), memory_space=pltpu.SMEM),
                  pl.BlockSpec((TOK, D), lambda i: (jnp.minimum(i, npt - 1), 0)),
                  pl.BlockSpec((TOK, D), lambda i: (jnp.maximum(i - npt, 0), 0))],
        out_specs=pl.BlockSpec(memory_space=pl.ANY),
        out_shape=jax.ShapeDtypeStruct((2 * (n_p + n_s), D), f32),
        scratch_shapes=[pltpu.SemaphoreType.DMA(())],
        compiler_params=_cp("arbitrary"),
        name="moe_dispatch",
    )(pos, hp, hs)


def _experts_kernel(tile_ref, exp_ref, lo_ref, hi_ref, xs_ref, wg_ref, wu_ref, wd_ref, o_ref):
    del tile_ref, exp_ref
    k = pl.program_id(0)
    lo = lo_ref[k]
    hi = hi_ref[k]

    @pl.when(hi > lo)
    def _():
        x = xs_ref[...].astype(bf16)
        ff = wg_ref.shape[2]
        acc = jnp.zeros(o_ref.shape, f32)
        for c0 in range(0, ff, FF_CHUNK_MOE):
            c1 = min(c0 + FF_CHUNK_MOE, ff)
            a = _dot(x, wg_ref[0, :, c0:c1])
            a = a * _sigmoid(a) * _dot(x, wu_ref[0, :, c0:c1])
            acc = acc + _dot(a.astype(bf16), wd_ref[0, c0:c1, :])
        @pl.when(lo == 0)
        def _():
            o_ref[...] = acc

        @pl.when(lo > 0)
        def _():
            row = lax.broadcasted_iota(jnp.int32, (o_ref.shape[0], 1), 0)
            o_ref[...] = jnp.where((row >= lo) & (row < hi), acc, o_ref[...])


def _experts(items, xs, wg, wu, wd):
    n_rows = xs.shape[0]
    ff = wg.shape[2]
    tm = TM_MOE
    assert n_rows % tm == 0
    n_items = items[0].shape[0]
    once = pl.Buffered(1)
    grid_spec = pltpu.PrefetchScalarGridSpec(
        num_scalar_prefetch=4,
        grid=(n_items,),
        in_specs=[pl.BlockSpec((tm, D), lambda k, ti, ex, lo, hi: (ti[k], 0)),
                  pl.BlockSpec((1, D, ff), lambda k, ti, ex, lo, hi: (ex[k], 0, 0), pipeline_mode=once),
                  pl.BlockSpec((1, D, ff), lambda k, ti, ex, lo, hi: (ex[k], 0, 0), pipeline_mode=once),
                  pl.BlockSpec((1, ff, D), lambda k, ti, ex, lo, hi: (ex[k], 0, 0), pipeline_mode=once)],
        out_specs=pl.BlockSpec((tm, D), lambda k, ti, ex, lo, hi: (ti[k], 0)))
    return pl.pallas_call(
        _experts_kernel,
        grid_spec=grid_spec,
        out_shape=jax.ShapeDtypeStruct((n_rows, D), f32),
        compiler_params=_cp("arbitrary"),
        name="moe_experts",
    )(*items, xs, wg, wu, wd)


def _work_items(off, n_rows):
    tm = TM_MOE
    n_tiles = n_rows // tm
    n_items = n_tiles + NE - 1
    start = off
    end = jnp.concatenate([off[1:], jnp.array([n_rows], jnp.int32)])
    cnt = end - start
    first = start // tm
    last = jnp.where(cnt > 0, (end - 1) // tm, first - 1)
    per = last - first + 1
    cum = jnp.cumsum(per)
    k = jnp.arange(n_items, dtype=jnp.int32)
    e = jnp.minimum(jnp.sum(k[:, None] >= cum[None, :], axis=1), NE - 1).astype(jnp.int32)
    tile = first[e] + (k - (cum[e] - per[e]))
    real = k < cum[NE - 1]
    e = jnp.where(real, e, jnp.max(jnp.where(real, e, 0)))
    tile = jnp.where(real, tile, n_tiles - 1).astype(jnp.int32)
    lo = jnp.clip(start[e] - tile * tm, 0, tm)
    hi = jnp.clip(end[e] - tile * tm, 0, tm)
    lo = jnp.where(real, lo, 0).astype(jnp.int32)
    hi = jnp.where(real, hi, 0).astype(jnp.int32)
    return tile, e, lo, hi


def _combine_kernel(pos_ref, pos_next_ref, x3_ref, route_ref, o_hbm, y_ref, buf, sem, *, n_tiles):
    i = pl.program_id(0)
    tok = x3_ref.shape[0]
    slot = i % 2

    def gather(p_ref, sl):
        def issue(r, c):
            _row_copy(o_hbm, p_ref[r], buf.at[sl, 0], r, sem.at[sl]).start(priority=0)
            _row_copy(o_hbm, p_ref[tok + r], buf.at[sl, 1], r, sem.at[sl]).start(priority=1)
            return c

        lax.fori_loop(0, tok, issue, 0, unroll=ROW_DMA_UNROLL)

    @pl.when(i == 0)
    def _():
        gather(pos_ref, slot)

    if n_tiles > 1:
        @pl.when(i + 1 < n_tiles)
        def _():
            gather(pos_next_ref, 1 - slot)

    for s in range(2):
        pltpu.make_async_copy(o_hbm.at[pl.ds(0, tok)], buf.at[slot, s], sem.at[slot]).wait()
    route = route_ref[...]
    y_ref[...] = x3_ref[...] + route[:, 2:3] * buf[slot, 0] + route[:, 3:4] * buf[slot, 1]


def _combine(pos, x3, route, o_sorted, tile0):
    n = x3.shape[0]
    assert n % TOK == 0
    nt = n // TOK
    pos_spec = lambda d: pl.BlockSpec((2 * TOK,), lambda i: (jnp.minimum(i + d, nt - 1) + tile0,),
                                      memory_space=pltpu.SMEM)
    return pl.pallas_call(
        functools.partial(_combine_kernel, n_tiles=nt),
        grid=(nt,),
        in_specs=[pos_spec(0), pos_spec(1),
                  pl.BlockSpec((TOK, D), lambda i: (i, 0)),
                  pl.BlockSpec((TOK, LANES), lambda i: (i, 0)),
                  pl.BlockSpec(memory_space=pl.ANY)],
        out_specs=pl.BlockSpec((TOK, D), lambda i: (i, 0)),
        out_shape=jax.ShapeDtypeStruct((n, D), f32),
        scratch_shapes=[pltpu.VMEM((2, 2, TOK, D), f32), pltpu.SemaphoreType.DMA((2,))],
        compiler_params=_cp("arbitrary"),
        name="moe_combine",
    )(pos, pos, x3, route, o_sorted)


def kernel(x_prompt, x_sample, state_conv, cache_k_win, cache_v_win, g_conv_norm, w_pw1, b_pw1, w_dw, b_dw, g_ln, b_ln, w_pw2, b_pw2, g_kv_norm, w_kv, g_k_norm, g_attn_norm, w_q, g_q_norm, sinks, w_o, rel_bias, g_ffn_norm, w_gate, w_up, w_down, g_moe_norm, w_router, w_e_gate, w_e_up, w_e_down):
    b, t, _ = x_prompt.shape
    bd, s_len, _ = x_sample.shape
    n_p = b * t
    n_s = bd * s_len
    row = lambda a: a.reshape(1, -1).astype(f32)

    conv_w = (row(g_conv_norm[0]), w_pw1[0].astype(bf16), row(b_pw1[0]),
              jnp.pad(w_dw[0], ((0, HALO - CONV_W), (0, 0))), row(b_dw[0]), row(g_ln[0]), row(b_ln[0]),
              w_pw2[0].astype(bf16), row(b_pw2[0]))
    ffn_w = (row(g_ffn_norm[0]), w_gate[0].astype(bf16), w_up[0].astype(bf16), w_down[0].astype(bf16))
    wq = w_q[0].reshape(D, NKV, G, DH).transpose(0, 2, 1, 3).reshape(D, NH * DH).astype(bf16)
    wo = w_o[0].reshape(NKV, G, DH, D).transpose(1, 0, 2, 3).reshape(NH * DH, D).astype(bf16)
    wr = jnp.pad(w_router[0], ((0, 0), (0, LANES - NE))).astype(bf16)
    head_of_lane = np.repeat(np.eye(NH, LANES), DH, axis=0)
    attn_w = (row(g_kv_norm), w_kv.astype(bf16), row(jnp.tile(g_k_norm, NKV)), row(g_attn_norm[0]),
              wq, row(jnp.tile(g_q_norm[0], NH)), jnp.asarray(head_of_lane, bf16),
              jnp.asarray(head_of_lane.T, bf16), wo, row(g_moe_norm[0]), wr)
    weg, weu, wed = w_e_gate[0].astype(bf16), w_e_up[0].astype(bf16), w_e_down[0].astype(bf16)
    sink = sinks[0].astype(f32)

    bias_p = _bias_table(rel_bias, _prompt_bucket_map()).reshape(2, NKV, G * WIN, 2 * WIN)
    map_c, map_n = _sample_bucket_maps(s_len, SEQ_TILE)
    rows = s_len * SEQ_TILE
    bias_c = _bias_table(rel_bias, map_c).reshape(NH * rows, SEQ_TILE * WIN)
    bias_n = _bias_table(rel_bias, map_n).reshape(NH * rows, LANES)

    x1p, ulast = _conv_prompt(x_prompt, conv_w)
    x2p = _ffn(x1p.reshape(n_p, D), *ffn_w).reshape(b, t, D)
    x3p, hp, route_p, route_tp, klast, vlast = _attn_prompt(x2p, sink, bias_p, attn_w)

    xs_tb = x_sample.transpose(1, 0, 2)
    x1s, u_s = _conv_sample(xs_tb, state_conv[0].transpose(1, 0, 2), conv_w)
    x2s = _ffn(x1s.reshape(n_s, D), *ffn_w)
    kc = cache_k_win.reshape(bd, WIN, KVW)
    vc = cache_v_win.reshape(bd, WIN, KVW)
    x3s, hs, route_s, route_ts, knew, vnew = _attn_sample(x2s, kc, vc, sink, bias_c, bias_n, attn_w)

    pos, off = _positions(jnp.concatenate([route_tp, route_ts], axis=1))
    pos = pos.reshape(-1)
    n_rows = 2 * (n_p + n_s)
    xs = _dispatch(pos, hp, hs)
    o_sorted = _experts(_work_items(off[:, 0], n_rows), xs, weg, weu, wed)
    y_p = _combine(pos, x3p.reshape(n_p, D), route_p, o_sorted, 0).reshape(b, t, D)
    y_s = _combine(pos, x3s, route_s, o_sorted, n_p // TOK).reshape(s_len, bd, D).transpose(1, 0, 2)

    n_prev = CONV_W - 1
    conv_p = ulast[:, HALO - n_prev:, :][None]
    conv_s = jnp.concatenate([state_conv[:, :, s_len:, :], u_s.transpose(1, 0, 2)[None]], axis=2)
    k_p = klast.reshape(b, WIN, NKV, DH)
    v_p = vlast.reshape(b, WIN, NKV, DH)
    k_new = knew.reshape(s_len, bd, NKV, DH).transpose(1, 0, 2, 3)
    v_new = vnew.reshape(s_len, bd, NKV, DH).transpose(1, 0, 2, 3)
    k_s = jnp.concatenate([cache_k_win[:, s_len:], k_new], axis=1)
    v_s = jnp.concatenate([cache_v_win[:, s_len:], v_new], axis=1)
    return (y_p, y_s, conv_p, conv_s, k_p, k_s, v_p, v_s)
```

```python
import functools

import numpy as np
import jax
import jax.numpy as jnp
from jax import lax
from jax.experimental import pallas as pl
from jax.experimental.pallas import tpu as pltpu

D = 1024
CONV_W = 31
HALO = 32
DH = 64
NH = 16
NKV = 4
G = NH // NKV
KVW = NKV * DH
WIN = 128
NB = 32
MAXD = 128
NE = 8
EPS = 1e-6
LANES = 128
NEG = float("-inf")

TT_CONV = 512
TM_FFN = 1024
TQ_ATT = 512
TOK = 512
TM_MOE = 512
ROW_DMA_UNROLL = 16
SEQ_TILE = 8
SEQ_TILE_CONV = 32
FF_CHUNK = 1792
FF_CHUNK_MOE = 512
VMEM_LIMIT = 56 * 1024 * 1024

bf16 = jnp.bfloat16
f32 = jnp.float32


def _cp(*sem):
    return pltpu.CompilerParams(dimension_semantics=sem, vmem_limit_bytes=VMEM_LIMIT)


def _full(shape, once=False):
    n = len(shape)
    return pl.BlockSpec(shape, lambda *_: (0,) * n, pipeline_mode=pl.Buffered(1) if once else None)


def _sigmoid(x):
    return 1.0 / (1.0 + jnp.exp(-x))


def _rms(x):
    return x * lax.rsqrt(jnp.mean(x * x, axis=-1, keepdims=True) + EPS)


def _dot(a, b):
    return jnp.dot(a, b, preferred_element_type=f32)


def _dot_nt(a, b):
    return lax.dot_general(a, b, (((1,), (1,)), ((), ())), preferred_element_type=f32)


def _split_dot(x, w):
    hi = x.astype(bf16)
    lo = (x - hi.astype(f32)).astype(bf16)
    return _dot(hi, w) + _dot(lo, w)


def _head_rms(x, gain, ind_ref, spread_ref):
    c = x.shape[1]
    ss = _dot((x * x).astype(bf16), ind_ref[0:c, :])
    r = lax.rsqrt(ss * (1.0 / DH) + EPS)
    return x * _split_dot(r, spread_ref[:, 0:c]) * gain


def _kv_lane_mask(kv):
    lane = lax.broadcasted_iota(jnp.int32, (1, KVW), 1)
    return (lane >= kv * DH) & (lane < (kv + 1) * DH)


def _bucket_of(dist):
    n = np.maximum(dist, 0)
    max_exact = NB // 2
    large = max_exact + (np.log(np.maximum(n, 1) / max_exact) / np.log(MAXD / max_exact)
                         * (NB - max_exact)).astype(np.int32)
    large = np.minimum(large, NB - 1)
    return np.where(n < max_exact, n, large).astype(np.int32)


def _prompt_bucket_map():
    dist = np.arange(WIN)[:, None] + WIN - np.arange(2 * WIN)[None, :]
    valid = (dist >= 0) & (dist < WIN)
    bk = np.where(valid, _bucket_of(dist), -1).astype(np.int32)
    first = bk.copy()
    first[:, :WIN] = -1
    return np.stack([first, bk])


def _sample_bucket_maps(s_len, st):
    r_t = np.repeat(np.arange(s_len), st)
    r_b = np.tile(np.arange(st), s_len)
    c_b = np.repeat(np.arange(st), WIN)
    c_s = np.tile(np.arange(WIN), st)
    dist = r_t[:, None] + WIN - c_s[None, :]
    valid = (r_b[:, None] == c_b[None, :]) & (dist >= 0) & (dist < WIN)
    map_c = np.where(valid, _bucket_of(dist), -1).astype(np.int32)
    dist_n = r_t[:, None] - r_t[None, :]
    valid_n = (r_b[:, None] == r_b[None, :]) & (dist_n >= 0) & (dist_n < WIN)
    map_n = np.full((s_len * st, LANES), -1, np.int32)
    map_n[:, :s_len * st] = np.where(valid_n, _bucket_of(dist_n), -1)
    return map_c[None], map_n[None]


def _bias_kernel(rb_ref, map_ref, out_ref):
    bk = map_ref[0]
    for h in range(NH):
        acc = jnp.full(bk.shape, NEG, f32)
        for b in range(NB):
            acc = jnp.where(bk == b, rb_ref[b, h], acc)
        out_ref[0, h] = acc


def _bias_table(rel_bias, bucket_map):
    v, r, c = bucket_map.shape
    return pl.pallas_call(
        _bias_kernel,
        grid=(v,),
        in_specs=[pl.BlockSpec(memory_space=pltpu.SMEM),
                  pl.BlockSpec((1, r, c), lambda i: (i, 0, 0))],
        out_specs=pl.BlockSpec((1, NH, r, c), lambda i: (i, 0, 0, 0)),
        out_shape=jax.ShapeDtypeStruct((v, NH, r, c), f32),
        compiler_params=_cp("arbitrary"),
        name="bias_table",
    )(rel_bias, jnp.asarray(bucket_map))


def _glu_rows(x, gn_ref, w1_ref, b1_ref):
    h = (_rms(x) * gn_ref[...]).astype(bf16)
    u2 = _dot(h, w1_ref[...]) + b1_ref[...]
    return u2[:, :D] * _sigmoid(u2[:, D:])


def _conv_tail(x, c, gln_ref, bln_ref, w2_ref, b2_ref):
    mu = jnp.mean(c, axis=-1, keepdims=True)
    xc = c - mu
    var = jnp.mean(xc * xc, axis=-1, keepdims=True)
    y = xc * lax.rsqrt(var + EPS) * gln_ref[...] + bln_ref[...]
    y = y * _sigmoid(y)
    return x + _dot(y.astype(bf16), w2_ref[...]) + b2_ref[...]


def _swiglu_rows(x, g_ref, wg_ref, wu_ref, wd_ref):
    h = (_rms(x) * g_ref[...]).astype(bf16)
    ff = wg_ref.shape[1]
    acc = x
    for c0 in range(0, ff, FF_CHUNK):
        c1 = min(c0 + FF_CHUNK, ff)
        a = _dot(h, wg_ref[:, c0:c1])
        a = a * _sigmoid(a) * _dot(h, wu_ref[:, c0:c1])
        acc = acc + _dot(a.astype(bf16), wd_ref[c0:c1, :])
    return acc


def _conv_prompt_kernel(x_ref, gn_ref, w1_ref, b1_ref, wdw_ref, bdw_ref, gln_ref, bln_ref,
                        w2_ref, b2_ref, x1_ref, ulast_ref, ubuf, ush):
    tt = x_ref.shape[1]
    t = pl.program_id(1)
    x = x_ref[0]
    u = _glu_rows(x, gn_ref, w1_ref, b1_ref)

    @pl.when(t == 0)
    def _():
        ubuf[0:HALO, :] = jnp.zeros((HALO, D), f32)

    @pl.when(t > 0)
    def _():
        ubuf[0:HALO, :] = ubuf[tt:tt + HALO, :]

    ubuf[HALO:HALO + tt, :] = u
    ulast_ref[0] = u[tt - HALO:, :]
    n_sh = ush.shape[1]
    for r in range(1, 8):
        ush[r - 1] = ubuf[r:r + n_sh, :]
    off = HALO - (CONV_W - 1)
    c = jnp.broadcast_to(bdw_ref[...], (tt, D))
    for k in range(CONV_W):
        a, r = divmod(off + k, 8)
        src = ubuf[8 * a:8 * a + tt, :] if r == 0 else ush[r - 1, 8 * a:8 * a + tt, :]
        c = c + wdw_ref[k:k + 1, :] * src
    x1_ref[0] = _conv_tail(x, c, gln_ref, bln_ref, w2_ref, b2_ref)


def _conv_prompt(x, wts):
    b, t, _ = x.shape
    tt = min(TT_CONV, t)
    assert t % tt == 0 and tt >= HALO
    row = lambda n: _full((1, n))
    return pl.pallas_call(
        _conv_prompt_kernel,
        grid=(b, t // tt),
        in_specs=[pl.BlockSpec((1, tt, D), lambda i, j: (i, j, 0)),
                  row(D), _full((D, 2 * D)), row(2 * D), _full((HALO, D)), row(D), row(D), row(D),
                  _full((D, D)), row(D)],
        out_specs=[pl.BlockSpec((1, tt, D), lambda i, j: (i, j, 0)),
                   pl.BlockSpec((1, HALO, D), lambda i, j: (i, 0, 0))],
        out_shape=[jax.ShapeDtypeStruct((b, t, D), f32),
                   jax.ShapeDtypeStruct((b, HALO, D), f32)],
        scratch_shapes=[pltpu.VMEM((tt + HALO, D), f32),
                        pltpu.VMEM((7, tt + HALO - 8, D), f32)],
        compiler_params=_cp("arbitrary", "arbitrary"),
        name="conv_prompt",
    )(x, *wts)


def _conv_sample_kernel(x_ref, st_ref, st_seq_ref, gn_ref, w1_ref, b1_ref, wdw_ref, bdw_ref, gln_ref, bln_ref,
                        w2_ref, b2_ref, x1_ref, snew_ref):
    s_len, bs, _ = x_ref.shape
    x = x_ref[...].reshape(s_len * bs, D)
    u = _glu_rows(x, gn_ref, w1_ref, b1_ref)
    n_prev = CONV_W - 1
    snew_ref[:, 0:n_prev - s_len, :] = st_seq_ref[:, s_len:, :]
    for t in range(s_len):
        for b in range(bs):
            snew_ref[b, n_prev - s_len + t:n_prev - s_len + t + 1, :] = u[t * bs + b:t * bs + b + 1, :]
    cs = []
    for t in range(s_len):
        c = jnp.broadcast_to(bdw_ref[...], (bs, D))
        for j in range(t, n_prev):
            c = c + wdw_ref[j - t:j - t + 1, :] * st_ref[j]
        for i in range(t + 1):
            k = n_prev - t + i
            c = c + wdw_ref[k:k + 1, :] * u[i * bs:(i + 1) * bs, :]
        cs.append(c)
    c = jnp.concatenate(cs, axis=0)
    x1_ref[...] = _conv_tail(x, c, gln_ref, bln_ref, w2_ref, b2_ref).reshape(s_len, bs, D)


def _conv_sample(x_tb, state, wts):
    s_len, bd, _ = x_tb.shape
    bs = min(SEQ_TILE_CONV, bd)
    assert bd % bs == 0 and bs % 8 == 0 and s_len <= CONV_W - 1
    row = lambda n: _full((1, n))
    blk = pl.BlockSpec((s_len, bs, D), lambda i: (0, i, 0))
    seq = pl.BlockSpec((bs, CONV_W - 1, D), lambda i: (i, 0, 0))
    return pl.pallas_call(
        _conv_sample_kernel,
        grid=(bd // bs,),
        in_specs=[blk, pl.BlockSpec((CONV_W - 1, bs, D), lambda i: (0, i, 0)), seq,
                  row(D), _full((D, 2 * D)), row(2 * D), _full((HALO, D)), row(D), row(D), row(D),
                  _full((D, D)), row(D)],
        out_specs=[blk, seq],
        out_shape=[jax.ShapeDtypeStruct((s_len, bd, D), f32),
                   jax.ShapeDtypeStruct((bd, CONV_W - 1, D), f32)],
        compiler_params=_cp("arbitrary"),
        name="conv_sample",
    )(x_tb, state.transpose(1, 0, 2), state, *wts)


def _ffn_kernel(x_ref, g_ref, wg_ref, wu_ref, wd_ref, o_ref):
    o_ref[...] = _swiglu_rows(x_ref[...], g_ref, wg_ref, wu_ref, wd_ref)


def _ffn(x, g, wg, wu, wd):
    n = x.shape[0]
    tm = min(TM_FFN, n)
    assert n % tm == 0
    ff = wg.shape[1]
    return pl.pallas_call(
        _ffn_kernel,
        grid=(n // tm,),
        in_specs=[pl.BlockSpec((tm, D), lambda i: (i, 0)), _full((1, D)),
                  _full((D, ff)), _full((D, ff)), _full((ff, D))],
        out_specs=pl.BlockSpec((tm, D), lambda i: (i, 0)),
        out_shape=jax.ShapeDtypeStruct((n, D), f32),
        compiler_params=_cp("arbitrary"),
        name="dense_ffn",
    )(x, g, wg, wu, wd)


def _qkv(x, gkv_ref, wkv_ref, gk_ref, gq_attn_ref, wq_ref, gq_ref, ind_ref, spread_ref):
    xn = _rms(x)
    kv = _dot((xn * gkv_ref[...]).astype(bf16), wkv_ref[...])
    k = _head_rms(kv[:, :KVW], gk_ref[...], ind_ref, spread_ref)
    v = kv[:, KVW:]
    q = _dot((xn * gq_attn_ref[...]).astype(bf16), wq_ref[...])
    q = _head_rms(q, gq_ref[...], ind_ref, spread_ref) * (DH ** -0.5)
    return q, k, v


def _sink_col(sinks_ref, kv, rows_per_head):
    return jnp.concatenate(
        [jnp.full((rows_per_head, 1), sinks_ref[kv * G + g], f32) for g in range(G)], axis=0)


def _moe_prep(x3, gmoe_ref, wr_ref, h_ref, route_ref, route_t_ref):
    h = _rms(x3) * gmoe_ref[...]
    h_ref[...] = h
    logits = _dot(h.astype(bf16), wr_ref[...])
    lane = lax.broadcasted_iota(jnp.int32, logits.shape, 1).astype(f32)
    logits = jnp.where(lane < NE, logits, NEG)
    m1 = jnp.max(logits, axis=-1, keepdims=True)
    i1 = jnp.min(jnp.where(logits == m1, lane, float(LANES)), axis=-1, keepdims=True)
    rest = jnp.where(lane == i1, NEG, logits)
    m2 = jnp.max(rest, axis=-1, keepdims=True)
    i2 = jnp.min(jnp.where(rest == m2, lane, float(LANES)), axis=-1, keepdims=True)
    e2 = jnp.exp(m2 - m1)
    den = 1.0 + e2
    g1 = 1.0 / den
    g2 = e2 / den
    route = jnp.where(lane == 0, i1,
                      jnp.where(lane == 1, i2,
                                jnp.where(lane == 2, g1, jnp.where(lane == 3, g2, 0.0))))
    route_ref[...] = route
    route_t_ref[...] = route.T[0:8, :]


def _attn_prompt_kernel(sinks_ref, x_ref, gkv_ref, wkv_ref, gk_ref, gqa_ref, wq_ref, gq_ref,
                        ind_ref, spread_ref, ones_ref, bias_ref, wo_ref, gmoe_ref, wr_ref,
                        x3_ref, h_ref, route_ref, route_t_ref, klast_ref, vlast_ref,
                        kbuf, vbuf, obuf):
    tq = x_ref.shape[1]
    t = pl.program_id(1)
    x = x_ref[0]
    q, k, v = _qkv(x, gkv_ref, wkv_ref, gk_ref, gqa_ref, wq_ref, gq_ref, ind_ref, spread_ref)
    klast_ref[0] = k[tq - WIN:, :]
    vlast_ref[0] = v[tq - WIN:, :]

    @pl.when(t == 0)
    def _():
        kbuf[:, 0:WIN, :] = jnp.zeros((NKV, WIN, KVW), bf16)
        vbuf[:, 0:WIN, :] = jnp.zeros((NKV, WIN, KVW), bf16)

    @pl.when(t > 0)
    def _():
        kbuf[:, 0:WIN, :] = kbuf[:, tq:tq + WIN, :]
        vbuf[:, 0:WIN, :] = vbuf[:, tq:tq + WIN, :]

    kb = k.astype(bf16)
    vb = v.astype(bf16)
    for kv in range(NKV):
        m = _kv_lane_mask(kv)
        kbuf[kv, WIN:WIN + tq, :] = jnp.where(m, kb, jnp.zeros_like(kb))
        vbuf[kv, WIN:WIN + tq, :] = jnp.where(m, vb, jnp.zeros_like(vb))

    qb = q.astype(bf16)
    for n in range(tq // WIN):
        r0 = n * WIN
        lhs = jnp.concatenate([qb[r0:r0 + WIN, g * KVW:(g + 1) * KVW] for g in range(G)], axis=0)
        ps = []
        lane = lax.broadcasted_iota(jnp.int32, (1, LANES), 1)
        sink_term = jnp.zeros((G * WIN, LANES), f32)
        for kv in range(NKV):
            s = _dot_nt(lhs, kbuf[kv, r0:r0 + 2 * WIN, :])
            s = s + bias_ref[jnp.where(t == 0, 0, 1) if n == 0 else 1, kv]
            sink = _sink_col(sinks_ref, kv, WIN)
            mx = jnp.maximum(jnp.max(s, axis=-1, keepdims=True), sink)
            ps.append(jnp.exp(s - mx).astype(bf16))
            sink_term = jnp.where(lane == kv, jnp.exp(sink - mx), sink_term)
        pcat = jnp.concatenate(ps, axis=-1)
        vcat = jnp.concatenate([vbuf[kv, r0:r0 + 2 * WIN, :] for kv in range(NKV)], axis=0)
        den = _dot(pcat, ones_ref[...]) + sink_term
        inv = 1.0 / jnp.where(lane < NKV, den, 1.0)
        pv = _dot(pcat, vcat) * _split_dot(inv, spread_ref[:, 0:KVW])
        obuf[r0:r0 + WIN, :] = jnp.concatenate(
            [pv[g * WIN:(g + 1) * WIN, :] for g in range(G)], axis=-1).astype(bf16)

    x3 = x + _dot(obuf[...], wo_ref[...])
    x3_ref[0] = x3
    _moe_prep(x3, gmoe_ref, wr_ref, h_ref, route_ref, route_t_ref)


def _attn_prompt(x2, sinks, bias, wts):
    b, t, _ = x2.shape
    tq = min(TQ_ATT, t)
    assert t % tq == 0 and tq % WIN == 0
    nt = t // tq
    gkv, wkv, gk, gqa, wq, gq, ind, spread, wo, gmoe, wr = wts
    ones = jnp.asarray(np.repeat(np.eye(NKV, LANES), 2 * WIN, axis=0), bf16)
    row = lambda n: _full((1, n))
    tok = lambda w: pl.BlockSpec((tq, w), lambda i, j: (i * nt + j, 0))
    last = pl.BlockSpec((1, WIN, KVW), lambda i, j: (i, 0, 0))
    return pl.pallas_call(
        _attn_prompt_kernel,
        grid=(b, nt),
        in_specs=[pl.BlockSpec(memory_space=pltpu.SMEM),
                  pl.BlockSpec((1, tq, D), lambda i, j: (i, j, 0)),
                  row(D), _full((D, 2 * KVW)), row(KVW), row(D), _full((D, D)), row(D),
                  _full(ind.shape), _full(spread.shape), _full(ones.shape),
                  _full(bias.shape), _full((D, D)), row(D), _full((D, LANES))],
        out_specs=[pl.BlockSpec((1, tq, D), lambda i, j: (i, j, 0)), tok(D), tok(LANES),
                   pl.BlockSpec((8, tq), lambda i, j: (0, i * nt + j)), last, last],
        out_shape=[jax.ShapeDtypeStruct((b, t, D), f32),
                   jax.ShapeDtypeStruct((b * t, D), f32),
                   jax.ShapeDtypeStruct((b * t, LANES), f32),
                   jax.ShapeDtypeStruct((8, b * t), f32),
                   jax.ShapeDtypeStruct((b, WIN, KVW), f32),
                   jax.ShapeDtypeStruct((b, WIN, KVW), f32)],
        scratch_shapes=[pltpu.VMEM((NKV, tq + WIN, KVW), bf16),
                        pltpu.VMEM((NKV, tq + WIN, KVW), bf16),
                        pltpu.VMEM((tq, D), bf16)],
        compiler_params=_cp("arbitrary", "arbitrary"),
        name="attn_prompt",
    )(sinks, x2, gkv, wkv, gk, gqa, wq, gq, ind, spread, ones, bias, wo, gmoe, wr)


def _attn_sample_kernel(sinks_ref, x_ref, kc_ref, vc_ref, gkv_ref, wkv_ref, gk_ref, gqa_ref, wq_ref,
                        gq_ref, ind_ref, spread_ref, bias_c_ref, bias_n_ref, wo_ref, gmoe_ref, wr_ref,
                        x3_ref, h_ref, route_ref, route_t_ref, kwin_ref, vwin_ref,
                        qbuf, kbuf, vbuf, obuf, *, bd):
    st = kc_ref.shape[0]
    s_len = x_ref.shape[0] // bd
    rows = s_len * st
    i = pl.program_id(0)

    @pl.when(i == 0)
    def _():
        q, k, v = _qkv(x_ref[...], gkv_ref, wkv_ref, gk_ref, gqa_ref, wq_ref, gq_ref, ind_ref, spread_ref)
        kbuf[...] = k
        vbuf[...] = v
        qbuf[...] = q

    def step_rows(buf, t):
        return buf[pl.ds(pl.multiple_of(t * bd + i * st, 8), st), :]

    def tile_rows(buf):
        return jnp.concatenate([step_rows(buf, t) for t in range(s_len)], axis=0).astype(bf16)

    for win_ref, c_ref, nbuf in ((kwin_ref, kc_ref, kbuf), (vwin_ref, vc_ref, vbuf)):
        win_ref[:, 0:WIN - s_len, :] = c_ref[:, s_len:, :]
        for t in range(s_len):
            new = step_rows(nbuf, t)
            for b in range(st):
                win_ref[b, WIN - s_len + t:WIN - s_len + t + 1, :] = new[b:b + 1, :]

    qt = tile_rows(qbuf)
    lhs = jnp.concatenate([qt[:, g * KVW:(g + 1) * KVW] for g in range(G)], axis=0)
    pad = jnp.zeros((LANES - rows, KVW), bf16)
    kn = jnp.concatenate([tile_rows(kbuf), pad], axis=0)
    vn = jnp.concatenate([tile_rows(vbuf), pad], axis=0)
    kc = kc_ref[...].reshape(st * WIN, KVW).astype(bf16)
    vc = vc_ref[...].reshape(st * WIN, KVW).astype(bf16)
    pv = jnp.zeros((G * rows, KVW), f32)
    for kv in range(NKV):
        m = _kv_lane_mask(kv)
        zc = jnp.zeros_like(kc)
        zn = jnp.zeros_like(kn)
        b0 = kv * G * rows
        s_c = _dot_nt(lhs, jnp.where(m, kc, zc)) + bias_c_ref[b0:b0 + G * rows, :]
        s_n = _dot_nt(lhs, jnp.where(m, kn, zn)) + bias_n_ref[b0:b0 + G * rows, :]
        sink = _sink_col(sinks_ref, kv, rows)
        mx = jnp.maximum(jnp.maximum(jnp.max(s_c, axis=-1, keepdims=True),
                                     jnp.max(s_n, axis=-1, keepdims=True)), sink)
        p_c = jnp.exp(s_c - mx)
        p_n = jnp.exp(s_n - mx)
        den = (jnp.sum(p_c, axis=-1, keepdims=True) + jnp.sum(p_n, axis=-1, keepdims=True)
               + jnp.exp(sink - mx))
        inv = 1.0 / den
        pv = pv + _dot((p_c * inv).astype(bf16), jnp.where(m, vc, zc))
        pv = pv + _dot((p_n * inv).astype(bf16), jnp.where(m, vn, zn))
    o = jnp.concatenate([pv[g * rows:(g + 1) * rows, :] for g in range(G)], axis=-1)
    for t in range(s_len):
        obuf[pl.ds(pl.multiple_of(t * bd + i * st, 8), st), :] = o[t * st:(t + 1) * st, :]

    @pl.when(i == pl.num_programs(0) - 1)
    def _():
        x3 = x_ref[...] + _dot(obuf[...].astype(bf16), wo_ref[...])
        x3_ref[...] = x3
        _moe_prep(x3, gmoe_ref, wr_ref, h_ref, route_ref, route_t_ref)


def _attn_sample(x2, kc, vc, sinks, bias_c, bias_n, wts):
    n_tok = x2.shape[0]
    bd = kc.shape[0]
    st = SEQ_TILE
    assert bd % st == 0 and n_tok % bd == 0 and (n_tok // bd) * st <= LANES
    gkv, wkv, gk, gqa, wq, gq, ind, spread, wo, gmoe, wr = wts
    row = lambda n: _full((1, n))
    cache = pl.BlockSpec((st, WIN, KVW), lambda i: (i, 0, 0))
    return pl.pallas_call(
        functools.partial(_attn_sample_kernel, bd=bd),
        grid=(bd // st,),
        in_specs=[pl.BlockSpec(memory_space=pltpu.SMEM), _full((n_tok, D)), cache, cache,
                  row(D), _full((D, 2 * KVW)), row(KVW), row(D), _full((D, D)), row(D),
                  _full(ind.shape), _full(spread.shape), _full(bias_c.shape), _full(bias_n.shape), _full((D, D)), row(D), _full((D, LANES))],
        out_specs=[_full((n_tok, D)), _full((n_tok, D)), _full((n_tok, LANES)), _full((8, n_tok)),
                   cache, cache],
        out_shape=[jax.ShapeDtypeStruct((n_tok, D), f32),
                   jax.ShapeDtypeStruct((n_tok, D), f32),
                   jax.ShapeDtypeStruct((n_tok, LANES), f32),
                   jax.ShapeDtypeStruct((8, n_tok), f32),
                   jax.ShapeDtypeStruct((bd, WIN, KVW), f32),
                   jax.ShapeDtypeStruct((bd, WIN, KVW), f32)],
        scratch_shapes=[pltpu.VMEM((n_tok, D), f32), pltpu.VMEM((n_tok, KVW), f32),
                        pltpu.VMEM((n_tok, KVW), f32), pltpu.VMEM((n_tok, D), f32)],
        compiler_params=_cp("arbitrary"),
        name="attn_sample",
    )(sinks, x2, kc, vc, gkv, wkv, gk, gqa, wq, gq, ind, spread, bias_c, bias_n, wo, gmoe, wr)


def _positions_kernel(rt_ref, pos_ref, off_ref, counts, running, offs):
    ph = pl.program_id(0)
    i = pl.program_id(1)
    n_sub, _, tok = pos_ref.shape
    sub = lax.broadcasted_iota(jnp.int32, (NE, tok), 0).astype(f32)

    def tile(s):
        e0 = rt_ref[0:1, s * tok:(s + 1) * tok]
        e1 = rt_ref[1:2, s * tok:(s + 1) * tok]
        sel = ((sub == e0) | (sub == e1)).astype(f32)
        return e0, e1, sel, jnp.broadcast_to(jnp.sum(sel, axis=-1, keepdims=True), (NE, LANES))

    @pl.when((ph == 0) & (i == 0))
    def _():
        counts[...] = jnp.zeros((NE, LANES), f32)

    @pl.when(ph == 0)
    def _():
        for s in range(n_sub):
            counts[...] += tile(s)[3]

    @pl.when((ph == 1) & (i == 0))
    def _():
        sub_l = lax.broadcasted_iota(jnp.int32, (NE, LANES), 0)
        acc = jnp.zeros((NE, LANES), f32)
        for e in range(NE - 1):
            acc = acc + jnp.where(sub_l > e, counts[e:e + 1, :], 0.0)
        offs[...] = acc
        running[...] = jnp.zeros((NE, LANES), f32)
        off_ref[...] = acc.astype(jnp.int32)

    @pl.when(ph == 1)
    def _():
        r_i = lax.broadcasted_iota(jnp.int32, (tok, tok), 0)
        c_i = lax.broadcasted_iota(jnp.int32, (tok, tok), 1)
        upper = (r_i < c_i).astype(bf16)
        for s in range(n_sub):
            e0, e1, sel, tile_cnt = tile(s)
            cum = _dot(sel.astype(bf16), upper)
            tot = cum + (offs[:, 0:1] + running[:, 0:1])
            p0 = jnp.sum(jnp.where(sub == e0, tot, 0.0), axis=0, keepdims=True)
            p1 = jnp.sum(jnp.where(sub == e1, tot, 0.0), axis=0, keepdims=True)
            pos_ref[s] = jnp.concatenate([p0, p1], axis=0).astype(jnp.int32)
            running[...] += tile_cnt


def _positions(route_t):
    n = route_t.shape[1]
    assert n % TOK == 0
    nt = n // TOK
    n_sub = max(d for d in range(1, 9) if nt % d == 0)
    return pl.pallas_call(
        _positions_kernel,
        grid=(2, nt // n_sub),
        in_specs=[pl.BlockSpec((8, n_sub * TOK), lambda p, i: (0, i))],
        out_specs=[pl.BlockSpec((n_sub, 2, TOK), lambda p, i: (i * p, 0, 0)), _full((NE, LANES))],
        out_shape=[jax.ShapeDtypeStruct((nt, 2, TOK), jnp.int32),
                   jax.ShapeDtypeStruct((NE, LANES), jnp.int32)],
        scratch_shapes=[pltpu.VMEM((NE, LANES), f32)] * 3,
        compiler_params=_cp("arbitrary", "arbitrary"),
        name="moe_positions",
    )(route_t)


def _row_copy(src, src_row, dst, dst_row, sem):
    return pltpu.make_async_copy(src.at[pl.ds(src_row, 1)], dst.at[pl.ds(dst_row, 1)], sem)


def _dispatch_kernel(pos_ref, hp_ref, hs_ref, xs_ref, sem, *, n_p_tiles):
    i = pl.program_id(0)
    tok = hp_ref.shape[0]

    def scatter_rows(h_ref):
        def issue(r, c):
            _row_copy(h_ref, r, xs_ref, pos_ref[r], sem).start(priority=0)
            _row_copy(h_ref, r, xs_ref, pos_ref[tok + r], sem).start(priority=1)
            return c

        lax.fori_loop(0, tok, issue, 0, unroll=ROW_DMA_UNROLL)
        for _ in range(2):
            pltpu.make_async_copy(h_ref, xs_ref.at[pl.ds(0, tok)], sem).wait()

    @pl.when(i < n_p_tiles)
    def _():
        scatter_rows(hp_ref)

    @pl.when(i >= n_p_tiles)
    def _():
        scatter_rows(hs_ref)


def _dispatch(pos, hp, hs):
    n_p, n_s = hp.shape[0], hs.shape[0]
    assert n_p % TOK == 0 and n_s % TOK == 0
    npt, nst = n_p // TOK, n_s // TOK
    return pl.pallas_call(
        functools.partial(_dispatch_kernel, n_p_tiles=npt),
        grid=(npt + nst,),
        in_specs=[pl.BlockSpec((2 * TOK,), lambda i: (i,), memory_space=pltpu.SMEM),
                  pl.BlockSpec((TOK, D), lambda i: (jnp.minimum(i, npt - 1), 0)),
                  pl.BlockSpec((TOK, D), lambda i: (jnp.maximum(i - npt, 0), 0))],
        out_specs=pl.BlockSpec(memory_space=pl.ANY),
        out_shape=jax.ShapeDtypeStruct((2 * (n_p + n_s), D), f32),
        scratch_shapes=[pltpu.SemaphoreType.DMA(())],
        compiler_params=_cp("arbitrary"),
        name="moe_dispatch",
    )(pos, hp, hs)


def _experts_kernel(tile_ref, exp_ref, lo_ref, hi_ref, xs_ref, wg_ref, wu_ref, wd_ref, o_ref):
    del tile_ref, exp_ref
    k = pl.program_id(0)
    lo = lo_ref[k]
    hi = hi_ref[k]

    @pl.when(hi > lo)
    def _():
        x = xs_ref[...].astype(bf16)
        ff = wg_ref.shape[2]
        acc = jnp.zeros(o_ref.shape, f32)
        for c0 in range(0, ff, FF_CHUNK_MOE):
            c1 = min(c0 + FF_CHUNK_MOE, ff)
            a = _dot(x, wg_ref[0, :, c0:c1])
            a = a * _sigmoid(a) * _dot(x, wu_ref[0, :, c0:c1])
            acc = acc + _dot(a.astype(bf16), wd_ref[0, c0:c1, :])
        @pl.when(lo == 0)
        def _():
            o_ref[...] = acc

        @pl.when(lo > 0)
        def _():
            row = lax.broadcasted_iota(jnp.int32, (o_ref.shape[0], 1), 0)
            o_ref[...] = jnp.where((row >= lo) & (row < hi), acc, o_ref[...])


def _experts(items, xs, wg, wu, wd):
    n_rows = xs.shape[0]
    ff = wg.shape[2]
    tm = TM_MOE
    assert n_rows % tm == 0
    n_items = items[0].shape[0]
    once = pl.Buffered(1)
    grid_spec = pltpu.PrefetchScalarGridSpec(
        num_scalar_prefetch=4,
        grid=(n_items,),
        in_specs=[pl.BlockSpec((tm, D), lambda k, ti, ex, lo, hi: (ti[k], 0)),
                  pl.BlockSpec((1, D, ff), lambda k, ti, ex, lo, hi: (ex[k], 0, 0), pipeline_mode=once),
                  pl.BlockSpec((1, D, ff), lambda k, ti, ex, lo, hi: (ex[k], 0, 0), pipeline_mode=once),
                  pl.BlockSpec((1, ff, D), lambda k, ti, ex, lo, hi: (ex[k], 0, 0), pipeline_mode=once)],
        out_specs=pl.BlockSpec((tm, D), lambda k, ti, ex, lo, hi: (ti[k], 0)))
    return pl.pallas_call(
        _experts_kernel,
        grid_spec=grid_spec,
        out_shape=jax.ShapeDtypeStruct((n_rows, D), f32),
        compiler_params=_cp("arbitrary"),
        name="moe_experts",
    )(*items, xs, wg, wu, wd)


def _work_items(off, n_rows):
    tm = TM_MOE
    n_tiles = n_rows // tm
    n_items = n_tiles + NE - 1
    start = off
    end = jnp.concatenate([off[1:], jnp.array([n_rows], jnp.int32)])
    cnt = end - start
    first = start // tm
    last = jnp.where(cnt > 0, (end - 1) // tm, first - 1)
    per = last - first + 1
    cum = jnp.cumsum(per)
    k = jnp.arange(n_items, dtype=jnp.int32)
    e = jnp.minimum(jnp.sum(k[:, None] >= cum[None, :], axis=1), NE - 1).astype(jnp.int32)
    tile = first[e] + (k - (cum[e] - per[e]))
    real = k < cum[NE - 1]
    e = jnp.where(real, e, jnp.max(jnp.where(real, e, 0)))
    tile = jnp.where(real, tile, n_tiles - 1).astype(jnp.int32)
    lo = jnp.clip(start[e] - tile * tm, 0, tm)
    hi = jnp.clip(end[e] - tile * tm, 0, tm)
    lo = jnp.where(real, lo, 0).astype(jnp.int32)
    hi = jnp.where(real, hi, 0).astype(jnp.int32)
    return tile, e, lo, hi


def _combine_kernel(pos_ref, pos_next_ref, x3_ref, route_ref, o_hbm, y_ref, buf, sem, *, n_tiles):
    i = pl.program_id(0)
    tok = x3_ref.shape[0]
    slot = i % 2

    def gather(p_ref, sl):
        def issue(r, c):
            _row_copy(o_hbm, p_ref[r], buf.at[sl, 0], r, sem.at[sl]).start(priority=0)
            _row_copy(o_hbm, p_ref[tok + r], buf.at[sl, 1], r, sem.at[sl]).start(priority=1)
            return c

        lax.fori_loop(0, tok, issue, 0, unroll=ROW_DMA_UNROLL)

    @pl.when(i == 0)
    def _():
        gather(pos_ref, slot)

    if n_tiles > 1:
        @pl.when(i + 1 < n_tiles)
        def _():
            gather(pos_next_ref, 1 - slot)

    for s in range(2):
        pltpu.make_async_copy(o_hbm.at[pl.ds(0, tok)], buf.at[slot, s], sem.at[slot]).wait()
    route = route_ref[...]
    y_ref[...] = x3_ref[...] + route[:, 2:3] * buf[slot, 0] + route[:, 3:4] * buf[slot, 1]


def _combine(pos, x3, route, o_sorted, tile0):
    n = x3.shape[0]
    assert n % TOK == 0
    nt = n // TOK
    pos_spec = lambda d: pl.BlockSpec((2 * TOK,), lambda i: (jnp.minimum(i + d, nt - 1) + tile0,),
                                      memory_space=pltpu.SMEM)
    return pl.pallas_call(
        functools.partial(_combine_kernel, n_tiles=nt),
        grid=(nt,),
        in_specs=[pos_spec(0), pos_spec(1),
                  pl.BlockSpec((TOK, D), lambda i: (i, 0)),
                  pl.BlockSpec((TOK, LANES), lambda i: (i, 0)),
                  pl.BlockSpec(memory_space=pl.ANY)],
        out_specs=pl.BlockSpec((TOK, D), lambda i: (i, 0)),
        out_shape=jax.ShapeDtypeStruct((n, D), f32),
        scratch_shapes=[pltpu.VMEM((2, 2, TOK, D), f32), pltpu.SemaphoreType.DMA((2,))],
        compiler_params=_cp("arbitrary"),
        name="moe_combine",
    )(pos, pos, x3, route, o_sorted)


def kernel(x_prompt, x_sample, state_conv, cache_k_win, cache_v_win, g_conv_norm, w_pw1, b_pw1, w_dw, b_dw, g_ln, b_ln, w_pw2, b_pw2, g_kv_norm, w_kv, g_k_norm, g_attn_norm, w_q, g_q_norm, sinks, w_o, rel_bias, g_ffn_norm, w_gate, w_up, w_down, g_moe_norm, w_router, w_e_gate, w_e_up, w_e_down):
    b, t, _ = x_prompt.shape
    bd, s_len, _ = x_sample.shape
    n_p = b * t
    n_s = bd * s_len
    row = lambda a: a.reshape(1, -1).astype(f32)

    conv_w = (row(g_conv_norm[0]), w_pw1[0].astype(bf16), row(b_pw1[0]),
              jnp.pad(w_dw[0], ((0, HALO - CONV_W), (0, 0))), row(b_dw[0]), row(g_ln[0]), row(b_ln[0]),
              w_pw2[0].astype(bf16), row(b_pw2[0]))
    ffn_w = (row(g_ffn_norm[0]), w_gate[0].astype(bf16), w_up[0].astype(bf16), w_down[0].astype(bf16))
    wq = w_q[0].reshape(D, NKV, G, DH).transpose(0, 2, 1, 3).reshape(D, NH * DH).astype(bf16)
    wo = w_o[0].reshape(NKV, G, DH, D).transpose(1, 0, 2, 3).reshape(NH * DH, D).astype(bf16)
    wr = jnp.pad(w_router[0], ((0, 0), (0, LANES - NE))).astype(bf16)
    head_of_lane = np.repeat(np.eye(NH, LANES), DH, axis=0)
    attn_w = (row(g_kv_norm), w_kv.astype(bf16), row(jnp.tile(g_k_norm, NKV)), row(g_attn_norm[0]),
              wq, row(jnp.tile(g_q_norm[0], NH)), jnp.asarray(head_of_lane, bf16),
              jnp.asarray(head_of_lane.T, bf16), wo, row(g_moe_norm[0]), wr)
    weg, weu, wed = w_e_gate[0].astype(bf16), w_e_up[0].astype(bf16), w_e_down[0].astype(bf16)
    sink = sinks[0].astype(f32)

    bias_p = _bias_table(rel_bias, _prompt_bucket_map()).reshape(2, NKV, G * WIN, 2 * WIN)
    map_c, map_n = _sample_bucket_maps(s_len, SEQ_TILE)
    rows = s_len * SEQ_TILE
    bias_c = _bias_table(rel_bias, map_c).reshape(NH * rows, SEQ_TILE * WIN)
    bias_n = _bias_table(rel_bias, map_n).reshape(NH * rows, LANES)

    x1p, ulast = _conv_prompt(x_prompt, conv_w)
    x2p = _ffn(x1p.reshape(n_p, D), *ffn_w).reshape(b, t, D)
    x3p, hp, route_p, route_tp, klast, vlast = _attn_prompt(x2p, sink, bias_p, attn_w)

    xs_tb = x_sample.transpose(1, 0, 2)
    x1s, snew = _conv_sample(xs_tb, state_conv[0], conv_w)
    x2s = _ffn(x1s.reshape(n_s, D), *ffn_w)
    kc = cache_k_win.reshape(bd, WIN, KVW)
    vc = cache_v_win.reshape(bd, WIN, KVW)
    x3s, hs, route_s, route_ts, kwin, vwin = _attn_sample(x2s, kc, vc, sink, bias_c, bias_n, attn_w)

    pos, off = _positions(jnp.concatenate([route_tp, route_ts], axis=1))
    pos = pos.reshape(-1)
    n_rows = 2 * (n_p + n_s)
    xs = _dispatch(pos, hp, hs)
    o_sorted = _experts(_work_items(off[:, 0], n_rows), xs, weg, weu, wed)
    y_p = _combine(pos, x3p.reshape(n_p, D), route_p, o_sorted, 0).reshape(b, t, D)
    y_s = _combine(pos, x3s, route_s, o_sorted, n_p // TOK).reshape(s_len, bd, D).transpose(1, 0, 2)

    n_prev = CONV_W - 1
    conv_p = ulast[:, HALO - n_prev:, :][None]
    conv_s = snew[None]
    k_p = klast.reshape(b, WIN, NKV, DH)
    v_p = vlast.reshape(b, WIN, NKV, DH)
    k_s = kwin.reshape(bd, WIN, NKV, DH)
    v_s = vwin.reshape(bd, WIN, NKV, DH)
    return (y_p, y_s, conv_p, conv_s, k_p, k_s, v_p, v_s)
```

```python
import functools

import numpy as np
import jax
import jax.numpy as jnp
from jax import lax
from jax.experimental import pallas as pl
from jax.experimental.pallas import tpu as pltpu

D = 1024
CONV_W = 31
HALO = 32
DH = 64
NH = 16
NKV = 4
G = NH // NKV
KVW = NKV * DH
WIN = 128
NB = 32
MAXD = 128
NE = 8
EPS = 1e-6
LANES = 128
NEG = float("-inf")

TT_CONV = 512
TM_FFN = 1024
TQ_ATT = 512
TOK = 512
TM_MOE = 512
ROW_DMA_UNROLL = 16
SEQ_TILE = 8
SEQ_TILE_CONV = 32
FF_CHUNK = 1792
FF_CHUNK_MOE = 512
VMEM_LIMIT = 56 * 1024 * 1024

bf16 = jnp.bfloat16
f32 = jnp.float32


def _cp(*sem):
    return pltpu.CompilerParams(dimension_semantics=sem, vmem_limit_bytes=VMEM_LIMIT)


def _full(shape, once=False):
    n = len(shape)
    return pl.BlockSpec(shape, lambda *_: (0,) * n, pipeline_mode=pl.Buffered(1) if once else None)


def _sigmoid(x):
    return 1.0 / (1.0 + jnp.exp(-x))


def _rms(x):
    return x * lax.rsqrt(jnp.mean(x * x, axis=-1, keepdims=True) + EPS)


def _dot(a, b):
    return jnp.dot(a, b, preferred_element_type=f32)


def _dot_nt(a, b):
    return lax.dot_general(a, b, (((1,), (1,)), ((), ())), preferred_element_type=f32)


def _split_dot(x, w):
    hi = x.astype(bf16)
    lo = (x - hi.astype(f32)).astype(bf16)
    return _dot(hi, w) + _dot(lo, w)


def _head_rms(x, gain, ind_ref, spread_ref):
    c = x.shape[1]
    ss = _dot((x * x).astype(bf16), ind_ref[0:c, :])
    r = lax.rsqrt(ss * (1.0 / DH) + EPS)
    return x * _split_dot(r, spread_ref[:, 0:c]) * gain


def _kv_lane_mask(kv):
    lane = lax.broadcasted_iota(jnp.int32, (1, KVW), 1)
    return (lane >= kv * DH) & (lane < (kv + 1) * DH)


def _bucket_of(dist):
    n = np.maximum(dist, 0)
    max_exact = NB // 2
    large = max_exact + (np.log(np.maximum(n, 1) / max_exact) / np.log(MAXD / max_exact)
                         * (NB - max_exact)).astype(np.int32)
    large = np.minimum(large, NB - 1)
    return np.where(n < max_exact, n, large).astype(np.int32)


def _prompt_bucket_map():
    dist = np.arange(WIN)[:, None] + WIN - np.arange(2 * WIN)[None, :]
    valid = (dist >= 0) & (dist < WIN)
    bk = np.where(valid, _bucket_of(dist), -1).astype(np.int32)
    first = bk.copy()
    first[:, :WIN] = -1
    return np.stack([first, bk])


def _sample_bucket_maps(s_len, st):
    r_t = np.repeat(np.arange(s_len), st)
    r_b = np.tile(np.arange(st), s_len)
    c_b = np.repeat(np.arange(st), WIN)
    c_s = np.tile(np.arange(WIN), st)
    dist = r_t[:, None] + WIN - c_s[None, :]
    valid = (r_b[:, None] == c_b[None, :]) & (dist >= 0) & (dist < WIN)
    map_c = np.where(valid, _bucket_of(dist), -1).astype(np.int32)
    dist_n = r_t[:, None] - r_t[None, :]
    valid_n = (r_b[:, None] == r_b[None, :]) & (dist_n >= 0) & (dist_n < WIN)
    map_n = np.full((s_len * st, LANES), -1, np.int32)
    map_n[:, :s_len * st] = np.where(valid_n, _bucket_of(dist_n), -1)
    return map_c[None], map_n[None]


def _bias_kernel(rb_ref, map_ref, out_ref):
    bk = map_ref[0]
    for h in range(NH):
        acc = jnp.full(bk.shape, NEG, f32)
        for b in range(NB):
            acc = jnp.where(bk == b, rb_ref[b, h], acc)
        out_ref[0, h] = acc


def _bias_table(rel_bias, bucket_map):
    v, r, c = bucket_map.shape
    return pl.pallas_call(
        _bias_kernel,
        grid=(v,),
        in_specs=[pl.BlockSpec(memory_space=pltpu.SMEM),
                  pl.BlockSpec((1, r, c), lambda i: (i, 0, 0))],
        out_specs=pl.BlockSpec((1, NH, r, c), lambda i: (i, 0, 0, 0)),
        out_shape=jax.ShapeDtypeStruct((v, NH, r, c), f32),
        compiler_params=_cp("arbitrary"),
        name="bias_table",
    )(rel_bias, jnp.asarray(bucket_map))


def _glu_rows(x, gn_ref, w1_ref, b1_ref):
    h = (_rms(x) * gn_ref[...]).astype(bf16)
    u2 = _dot(h, w1_ref[...]) + b1_ref[...]
    return u2[:, :D] * _sigmoid(u2[:, D:])


def _conv_tail(x, c, gln_ref, bln_ref, w2_ref, b2_ref):
    mu = jnp.mean(c, axis=-1, keepdims=True)
    xc = c - mu
    var = jnp.mean(xc * xc, axis=-1, keepdims=True)
    y = xc * lax.rsqrt(var + EPS) * gln_ref[...] + bln_ref[...]
    y = y * _sigmoid(y)
    return x + _dot(y.astype(bf16), w2_ref[...]) + b2_ref[...]


def _swiglu_rows(x, g_ref, wg_ref, wu_ref, wd_ref):
    h = (_rms(x) * g_ref[...]).astype(bf16)
    ff = wg_ref.shape[1]
    acc = x
    for c0 in range(0, ff, FF_CHUNK):
        c1 = min(c0 + FF_CHUNK, ff)
        a = _dot(h, wg_ref[:, c0:c1])
        a = a * _sigmoid(a) * _dot(h, wu_ref[:, c0:c1])
        acc = acc + _dot(a.astype(bf16), wd_ref[c0:c1, :])
    return acc


def _conv_prompt_kernel(x_ref, gn_ref, w1_ref, b1_ref, wdw_ref, bdw_ref, gln_ref, bln_ref,
                        w2_ref, b2_ref, x1_ref, ulast_ref, ubuf, ush):
    tt = x_ref.shape[1]
    t = pl.program_id(1)
    x = x_ref[0]
    u = _glu_rows(x, gn_ref, w1_ref, b1_ref)

    @pl.when(t == 0)
    def _():
        ubuf[0:HALO, :] = jnp.zeros((HALO, D), f32)

    @pl.when(t > 0)
    def _():
        ubuf[0:HALO, :] = ubuf[tt:tt + HALO, :]

    ubuf[HALO:HALO + tt, :] = u
    ulast_ref[0] = u[tt - HALO:, :]
    n_sh = ush.shape[1]
    for r in range(1, 8):
        ush[r - 1] = ubuf[r:r + n_sh, :]
    off = HALO - (CONV_W - 1)
    c = jnp.broadcast_to(bdw_ref[...], (tt, D))
    for k in range(CONV_W):
        a, r = divmod(off + k, 8)
        src = ubuf[8 * a:8 * a + tt, :] if r == 0 else ush[r - 1, 8 * a:8 * a + tt, :]
        c = c + wdw_ref[k:k + 1, :] * src
    x1_ref[0] = _conv_tail(x, c, gln_ref, bln_ref, w2_ref, b2_ref)


def _conv_prompt(x, wts):
    b, t, _ = x.shape
    tt = min(TT_CONV, t)
    assert t % tt == 0 and tt >= HALO
    row = lambda n: _full((1, n))
    return pl.pallas_call(
        _conv_prompt_kernel,
        grid=(b, t // tt),
        in_specs=[pl.BlockSpec((1, tt, D), lambda i, j: (i, j, 0)),
                  row(D), _full((D, 2 * D)), row(2 * D), _full((HALO, D)), row(D), row(D), row(D),
                  _full((D, D)), row(D)],
        out_specs=[pl.BlockSpec((1, tt, D), lambda i, j: (i, j, 0)),
                   pl.BlockSpec((1, HALO, D), lambda i, j: (i, 0, 0))],
        out_shape=[jax.ShapeDtypeStruct((b, t, D), f32),
                   jax.ShapeDtypeStruct((b, HALO, D), f32)],
        scratch_shapes=[pltpu.VMEM((tt + HALO, D), f32),
                        pltpu.VMEM((7, tt + HALO - 8, D), f32)],
        compiler_params=_cp("arbitrary", "arbitrary"),
        name="conv_prompt",
    )(x, *wts)


def _conv_sample_kernel(x_ref, st_ref, gn_ref, w1_ref, b1_ref, wdw_ref, bdw_ref, gln_ref, bln_ref,
                        w2_ref, b2_ref, x1_ref, snew_ref):
    s_len, bs, _ = x_ref.shape
    x = x_ref[...].reshape(s_len * bs, D)
    u = _glu_rows(x, gn_ref, w1_ref, b1_ref)
    n_prev = CONV_W - 1
    snew_ref[:, 0:n_prev - s_len, :] = st_ref[:, s_len:, :]
    for t in range(s_len):
        for b in range(bs):
            snew_ref[b, n_prev - s_len + t:n_prev - s_len + t + 1, :] = u[t * bs + b:t * bs + b + 1, :]
    cs = []
    for t in range(s_len):
        c = jnp.broadcast_to(bdw_ref[...], (bs, D))
        for j in range(t, n_prev):
            c = c + wdw_ref[j - t:j - t + 1, :] * st_ref[:, j, :]
        for i in range(t + 1):
            k = n_prev - t + i
            c = c + wdw_ref[k:k + 1, :] * u[i * bs:(i + 1) * bs, :]
        cs.append(c)
    c = jnp.concatenate(cs, axis=0)
    x1_ref[...] = _conv_tail(x, c, gln_ref, bln_ref, w2_ref, b2_ref).reshape(s_len, bs, D)


def _conv_sample(x_tb, state, wts):
    s_len, bd, _ = x_tb.shape
    bs = min(SEQ_TILE_CONV, bd)
    assert bd % bs == 0 and bs % 8 == 0 and s_len <= CONV_W - 1
    row = lambda n: _full((1, n))
    blk = pl.BlockSpec((s_len, bs, D), lambda i: (0, i, 0))
    seq = pl.BlockSpec((bs, CONV_W - 1, D), lambda i: (i, 0, 0))
    return pl.pallas_call(
        _conv_sample_kernel,
        grid=(bd // bs,),
        in_specs=[blk, seq,
                  row(D), _full((D, 2 * D)), row(2 * D), _full((HALO, D)), row(D), row(D), row(D),
                  _full((D, D)), row(D)],
        out_specs=[blk, seq],
        out_shape=[jax.ShapeDtypeStruct((s_len, bd, D), f32),
                   jax.ShapeDtypeStruct((bd, CONV_W - 1, D), f32)],
        compiler_params=_cp("arbitrary"),
        name="conv_sample",
    )(x_tb, state, *wts)


def _ffn_kernel(x_ref, g_ref, wg_ref, wu_ref, wd_ref, o_ref):
    o_ref[...] = _swiglu_rows(x_ref[...], g_ref, wg_ref, wu_ref, wd_ref)


def _ffn(x, g, wg, wu, wd):
    n = x.shape[0]
    tm = min(TM_FFN, n)
    assert n % tm == 0
    ff = wg.shape[1]
    return pl.pallas_call(
        _ffn_kernel,
        grid=(n // tm,),
        in_specs=[pl.BlockSpec((tm, D), lambda i: (i, 0)), _full((1, D)),
                  _full((D, ff)), _full((D, ff)), _full((ff, D))],
        out_specs=pl.BlockSpec((tm, D), lambda i: (i, 0)),
        out_shape=jax.ShapeDtypeStruct((n, D), f32),
        compiler_params=_cp("arbitrary"),
        name="dense_ffn",
    )(x, g, wg, wu, wd)


def _qkv(x, gkv_ref, wkv_ref, gk_ref, gq_attn_ref, wq_ref, gq_ref, ind_ref, spread_ref):
    xn = _rms(x)
    kv = _dot((xn * gkv_ref[...]).astype(bf16), wkv_ref[...])
    k = _head_rms(kv[:, :KVW], gk_ref[...], ind_ref, spread_ref)
    v = kv[:, KVW:]
    q = _dot((xn * gq_attn_ref[...]).astype(bf16), wq_ref[...])
    q = _head_rms(q, gq_ref[...], ind_ref, spread_ref) * (DH ** -0.5)
    return q, k, v


def _sink_col(sinks_ref, kv, rows_per_head):
    return jnp.concatenate(
        [jnp.full((rows_per_head, 1), sinks_ref[kv * G + g], f32) for g in range(G)], axis=0)


def _moe_prep(x3, gmoe_ref, wr_ref, h_ref, route_ref, route_t_ref):
    h = _rms(x3) * gmoe_ref[...]
    h_ref[...] = h
    logits = _dot(h.astype(bf16), wr_ref[...])
    lane = lax.broadcasted_iota(jnp.int32, logits.shape, 1).astype(f32)
    logits = jnp.where(lane < NE, logits, NEG)
    m1 = jnp.max(logits, axis=-1, keepdims=True)
    i1 = jnp.min(jnp.where(logits == m1, lane, float(LANES)), axis=-1, keepdims=True)
    rest = jnp.where(lane == i1, NEG, logits)
    m2 = jnp.max(rest, axis=-1, keepdims=True)
    i2 = jnp.min(jnp.where(rest == m2, lane, float(LANES)), axis=-1, keepdims=True)
    e2 = jnp.exp(m2 - m1)
    den = 1.0 + e2
    g1 = 1.0 / den
    g2 = e2 / den
    route = jnp.where(lane == 0, i1,
                      jnp.where(lane == 1, i2,
                                jnp.where(lane == 2, g1, jnp.where(lane == 3, g2, 0.0))))
    route_ref[...] = route
    route_t_ref[...] = route.T[0:8, :]


def _attn_prompt_kernel(sinks_ref, x_ref, gkv_ref, wkv_ref, gk_ref, gqa_ref, wq_ref, gq_ref,
                        ind_ref, spread_ref, ones_ref, bias_ref, wo_ref, gmoe_ref, wr_ref,
                        x3_ref, h_ref, route_ref, route_t_ref, klast_ref, vlast_ref,
                        kbuf, vbuf, obuf):
    tq = x_ref.shape[1]
    t = pl.program_id(1)
    x = x_ref[0]
    q, k, v = _qkv(x, gkv_ref, wkv_ref, gk_ref, gqa_ref, wq_ref, gq_ref, ind_ref, spread_ref)
    klast_ref[0] = k[tq - WIN:, :]
    vlast_ref[0] = v[tq - WIN:, :]

    @pl.when(t == 0)
    def _():
        kbuf[:, 0:WIN, :] = jnp.zeros((NKV, WIN, KVW), bf16)
        vbuf[:, 0:WIN, :] = jnp.zeros((NKV, WIN, KVW), bf16)

    @pl.when(t > 0)
    def _():
        kbuf[:, 0:WIN, :] = kbuf[:, tq:tq + WIN, :]
        vbuf[:, 0:WIN, :] = vbuf[:, tq:tq + WIN, :]

    kb = k.astype(bf16)
    vb = v.astype(bf16)
    for kv in range(NKV):
        m = _kv_lane_mask(kv)
        kbuf[kv, WIN:WIN + tq, :] = jnp.where(m, kb, jnp.zeros_like(kb))
        vbuf[kv, WIN:WIN + tq, :] = jnp.where(m, vb, jnp.zeros_like(vb))

    qb = q.astype(bf16)
    for n in range(tq // WIN):
        r0 = n * WIN
        lhs = jnp.concatenate([qb[r0:r0 + WIN, g * KVW:(g + 1) * KVW] for g in range(G)], axis=0)
        ps = []
        lane = lax.broadcasted_iota(jnp.int32, (1, LANES), 1)
        sink_term = jnp.zeros((G * WIN, LANES), f32)
        for kv in range(NKV):
            s = _dot_nt(lhs, kbuf[kv, r0:r0 + 2 * WIN, :])
            s = s + bias_ref[jnp.where(t == 0, 0, 1) if n == 0 else 1, kv]
            sink = _sink_col(sinks_ref, kv, WIN)
            mx = jnp.maximum(jnp.max(s, axis=-1, keepdims=True), sink)
            ps.append(jnp.exp(s - mx).astype(bf16))
            sink_term = jnp.where(lane == kv, jnp.exp(sink - mx), sink_term)
        pcat = jnp.concatenate(ps, axis=-1)
        vcat = jnp.concatenate([vbuf[kv, r0:r0 + 2 * WIN, :] for kv in range(NKV)], axis=0)
        den = _dot(pcat, ones_ref[...]) + sink_term
        inv = 1.0 / jnp.where(lane < NKV, den, 1.0)
        pv = _dot(pcat, vcat) * _split_dot(inv, spread_ref[:, 0:KVW])
        obuf[r0:r0 + WIN, :] = jnp.concatenate(
            [pv[g * WIN:(g + 1) * WIN, :] for g in range(G)], axis=-1).astype(bf16)

    x3 = x + _dot(obuf[...], wo_ref[...])
    x3_ref[0] = x3
    _moe_prep(x3, gmoe_ref, wr_ref, h_ref, route_ref, route_t_ref)


def _attn_prompt(x2, sinks, bias, wts):
    b, t, _ = x2.shape
    tq = min(TQ_ATT, t)
    assert t % tq == 0 and tq % WIN == 0
    nt = t // tq
    gkv, wkv, gk, gqa, wq, gq, ind, spread, wo, gmoe, wr = wts
    ones = jnp.asarray(np.repeat(np.eye(NKV, LANES), 2 * WIN, axis=0), bf16)
    row = lambda n: _full((1, n))
    tok = lambda w: pl.BlockSpec((tq, w), lambda i, j: (i * nt + j, 0))
    last = pl.BlockSpec((1, WIN, KVW), lambda i, j: (i, 0, 0))
    return pl.pallas_call(
        _attn_prompt_kernel,
        grid=(b, nt),
        in_specs=[pl.BlockSpec(memory_space=pltpu.SMEM),
                  pl.BlockSpec((1, tq, D), lambda i, j: (i, j, 0)),
                  row(D), _full((D, 2 * KVW)), row(KVW), row(D), _full((D, D)), row(D),
                  _full(ind.shape), _full(spread.shape), _full(ones.shape),
                  _full(bias.shape), _full((D, D)), row(D), _full((D, LANES))],
        out_specs=[pl.BlockSpec((1, tq, D), lambda i, j: (i, j, 0)), tok(D), tok(LANES),
                   pl.BlockSpec((8, tq), lambda i, j: (0, i * nt + j)), last, last],
        out_shape=[jax.ShapeDtypeStruct((b, t, D), f32),
                   jax.ShapeDtypeStruct((b * t, D), f32),
                   jax.ShapeDtypeStruct((b * t, LANES), f32),
                   jax.ShapeDtypeStruct((8, b * t), f32),
                   jax.ShapeDtypeStruct((b, WIN, KVW), f32),
                   jax.ShapeDtypeStruct((b, WIN, KVW), f32)],
        scratch_shapes=[pltpu.VMEM((NKV, tq + WIN, KVW), bf16),
                        pltpu.VMEM((NKV, tq + WIN, KVW), bf16),
                        pltpu.VMEM((tq, D), bf16)],
        compiler_params=_cp("arbitrary", "arbitrary"),
        name="attn_prompt",
    )(sinks, x2, gkv, wkv, gk, gqa, wq, gq, ind, spread, ones, bias, wo, gmoe, wr)


def _attn_sample_kernel(sinks_ref, x_ref, kc_ref, vc_ref, gkv_ref, wkv_ref, gk_ref, gqa_ref, wq_ref,
                        gq_ref, ind_ref, spread_ref, bias_c_ref, bias_n_ref, wo_ref, gmoe_ref, wr_ref,
                        x3_ref, h_ref, route_ref, route_t_ref, kwin_ref, vwin_ref,
                        qbuf, kbuf, vbuf, obuf, *, bd):
    st = kc_ref.shape[0]
    s_len = x_ref.shape[0] // bd
    rows = s_len * st
    i = pl.program_id(0)

    @pl.when(i == 0)
    def _():
        q, k, v = _qkv(x_ref[...], gkv_ref, wkv_ref, gk_ref, gqa_ref, wq_ref, gq_ref, ind_ref, spread_ref)
        kbuf[...] = k
        vbuf[...] = v
        qbuf[...] = q

    def step_rows(buf, t):
        return buf[pl.ds(pl.multiple_of(t * bd + i * st, 8), st), :]

    def tile_rows(buf):
        return jnp.concatenate([step_rows(buf, t) for t in range(s_len)], axis=0).astype(bf16)

    for win_ref, c_ref, nbuf in ((kwin_ref, kc_ref, kbuf), (vwin_ref, vc_ref, vbuf)):
        win_ref[:, 0:WIN - s_len, :] = c_ref[:, s_len:, :]
        for t in range(s_len):
            new = step_rows(nbuf, t)
            for b in range(st):
                win_ref[b, WIN - s_len + t:WIN - s_len + t + 1, :] = new[b:b + 1, :]

    qt = tile_rows(qbuf)
    lhs = jnp.concatenate([qt[:, g * KVW:(g + 1) * KVW] for g in range(G)], axis=0)
    pad = jnp.zeros((LANES - rows, KVW), bf16)
    kn = jnp.concatenate([tile_rows(kbuf), pad], axis=0)
    vn = jnp.concatenate([tile_rows(vbuf), pad], axis=0)
    kc = kc_ref[...].reshape(st * WIN, KVW).astype(bf16)
    vc = vc_ref[...].reshape(st * WIN, KVW).astype(bf16)
    pv = jnp.zeros((G * rows, KVW), f32)
    for kv in range(NKV):
        m = _kv_lane_mask(kv)
        zc = jnp.zeros_like(kc)
        zn = jnp.zeros_like(kn)
        b0 = kv * G * rows
        s_c = _dot_nt(lhs, jnp.where(m, kc, zc)) + bias_c_ref[b0:b0 + G * rows, :]
        s_n = _dot_nt(lhs, jnp.where(m, kn, zn)) + bias_n_ref[b0:b0 + G * rows, :]
        sink = _sink_col(sinks_ref, kv, rows)
        mx = jnp.maximum(jnp.maximum(jnp.max(s_c, axis=-1, keepdims=True),
                                     jnp.max(s_n, axis=-1, keepdims=True)), sink)
        p_c = jnp.exp(s_c - mx)
        p_n = jnp.exp(s_n - mx)
        den = (jnp.sum(p_c, axis=-1, keepdims=True) + jnp.sum(p_n, axis=-1, keepdims=True)
               + jnp.exp(sink - mx))
        inv = 1.0 / den
        pv = pv + _dot((p_c * inv).astype(bf16), jnp.where(m, vc, zc))
        pv = pv + _dot((p_n * inv).astype(bf16), jnp.where(m, vn, zn))
    o = jnp.concatenate([pv[g * rows:(g + 1) * rows, :] for g in range(G)], axis=-1)
    for t in range(s_len):
        obuf[pl.ds(pl.multiple_of(t * bd + i * st, 8), st), :] = o[t * st:(t + 1) * st, :]

    @pl.when(i == pl.num_programs(0) - 1)
    def _():
        x3 = x_ref[...] + _dot(obuf[...].astype(bf16), wo_ref[...])
        x3_ref[...] = x3
        _moe_prep(x3, gmoe_ref, wr_ref, h_ref, route_ref, route_t_ref)


def _attn_sample(x2, kc, vc, sinks, bias_c, bias_n, wts):
    n_tok = x2.shape[0]
    bd = kc.shape[0]
    st = SEQ_TILE
    assert bd % st == 0 and n_tok % bd == 0 and (n_tok // bd) * st <= LANES
    gkv, wkv, gk, gqa, wq, gq, ind, spread, wo, gmoe, wr = wts
    row = lambda n: _full((1, n))
    cache = pl.BlockSpec((st, WIN, KVW), lambda i: (i, 0, 0))
    return pl.pallas_call(
        functools.partial(_attn_sample_kernel, bd=bd),
        grid=(bd // st,),
        in_specs=[pl.BlockSpec(memory_space=pltpu.SMEM), _full((n_tok, D)), cache, cache,
                  row(D), _full((D, 2 * KVW)), row(KVW), row(D), _full((D, D)), row(D),
                  _full(ind.shape), _full(spread.shape), _full(bias_c.shape), _full(bias_n.shape), _full((D, D)), row(D), _full((D, LANES))],
        out_specs=[_full((n_tok, D)), _full((n_tok, D)), _full((n_tok, LANES)), _full((8, n_tok)),
                   cache, cache],
        out_shape=[jax.ShapeDtypeStruct((n_tok, D), f32),
                   jax.ShapeDtypeStruct((n_tok, D), f32),
                   jax.ShapeDtypeStruct((n_tok, LANES), f32),
                   jax.ShapeDtypeStruct((8, n_tok), f32),
                   jax.ShapeDtypeStruct((bd, WIN, KVW), f32),
                   jax.ShapeDtypeStruct((bd, WIN, KVW), f32)],
        scratch_shapes=[pltpu.VMEM((n_tok, D), f32), pltpu.VMEM((n_tok, KVW), f32),
                        pltpu.VMEM((n_tok, KVW), f32), pltpu.VMEM((n_tok, D), f32)],
        compiler_params=_cp("arbitrary"),
        name="attn_sample",
    )(sinks, x2, kc, vc, gkv, wkv, gk, gqa, wq, gq, ind, spread, bias_c, bias_n, wo, gmoe, wr)


def _positions_kernel(rt_ref, pos_ref, off_ref, counts, running, offs):
    ph = pl.program_id(0)
    i = pl.program_id(1)
    n_sub, _, tok = pos_ref.shape
    sub = lax.broadcasted_iota(jnp.int32, (NE, tok), 0).astype(f32)

    def tile(s):
        e0 = rt_ref[0:1, s * tok:(s + 1) * tok]
        e1 = rt_ref[1:2, s * tok:(s + 1) * tok]
        sel = ((sub == e0) | (sub == e1)).astype(f32)
        return e0, e1, sel, jnp.broadcast_to(jnp.sum(sel, axis=-1, keepdims=True), (NE, LANES))

    @pl.when((ph == 0) & (i == 0))
    def _():
        counts[...] = jnp.zeros((NE, LANES), f32)

    @pl.when(ph == 0)
    def _():
        for s in range(n_sub):
            counts[...] += tile(s)[3]

    @pl.when((ph == 1) & (i == 0))
    def _():
        sub_l = lax.broadcasted_iota(jnp.int32, (NE, LANES), 0)
        acc = jnp.zeros((NE, LANES), f32)
        for e in range(NE - 1):
            acc = acc + jnp.where(sub_l > e, counts[e:e + 1, :], 0.0)
        offs[...] = acc
        running[...] = jnp.zeros((NE, LANES), f32)
        off_ref[...] = acc.astype(jnp.int32)

    @pl.when(ph == 1)
    def _():
        r_i = lax.broadcasted_iota(jnp.int32, (tok, tok), 0)
        c_i = lax.broadcasted_iota(jnp.int32, (tok, tok), 1)
        upper = (r_i < c_i).astype(bf16)
        for s in range(n_sub):
            e0, e1, sel, tile_cnt = tile(s)
            cum = _dot(sel.astype(bf16), upper)
            tot = cum + (offs[:, 0:1] + running[:, 0:1])
            p0 = jnp.sum(jnp.where(sub == e0, tot, 0.0), axis=0, keepdims=True)
            p1 = jnp.sum(jnp.where(sub == e1, tot, 0.0), axis=0, keepdims=True)
            pos_ref[s] = jnp.concatenate([p0, p1], axis=0).astype(jnp.int32)
            running[...] += tile_cnt


def _positions(route_t):
    n = route_t.shape[1]
    assert n % TOK == 0
    nt = n // TOK
    n_sub = max(d for d in range(1, 9) if nt % d == 0)
    return pl.pallas_call(
        _positions_kernel,
        grid=(2, nt // n_sub),
        in_specs=[pl.BlockSpec((8, n_sub * TOK), lambda p, i: (0, i))],
        out_specs=[pl.BlockSpec((n_sub, 2, TOK), lambda p, i: (i * p, 0, 0)), _full((NE, LANES))],
        out_shape=[jax.ShapeDtypeStruct((nt, 2, TOK), jnp.int32),
                   jax.ShapeDtypeStruct((NE, LANES), jnp.int32)],
        scratch_shapes=[pltpu.VMEM((NE, LANES), f32)] * 3,
        compiler_params=_cp("arbitrary", "arbitrary"),
        name="moe_positions",
    )(route_t)


def _row_copy(src, src_row, dst, dst_row, sem):
    return pltpu.make_async_copy(src.at[pl.ds(src_row, 1)], dst.at[pl.ds(dst_row, 1)], sem)


def _dispatch_kernel(pos_ref, hp_ref, hs_ref, xs_ref, sem, *, n_p_tiles):
    i = pl.program_id(0)
    tok = hp_ref.shape[0]

    def scatter_rows(h_ref):
        def issue(r, c):
            _row_copy(h_ref, r, xs_ref, pos_ref[r], sem).start(priority=0)
            _row_copy(h_ref, r, xs_ref, pos_ref[tok + r], sem).start(priority=1)
            return c

        lax.fori_loop(0, tok, issue, 0, unroll=ROW_DMA_UNROLL)
        for _ in range(2):
            pltpu.make_async_copy(h_ref, xs_ref.at[pl.ds(0, tok)], sem).wait()

    @pl.when(i < n_p_tiles)
    def _():
        scatter_rows(hp_ref)

    @pl.when(i >= n_p_tiles)
    def _():
        scatter_rows(hs_ref)


def _dispatch(pos, hp, hs):
    n_p, n_s = hp.shape[0], hs.shape[0]
    assert n_p % TOK == 0 and n_s % TOK == 0
    npt, nst = n_p // TOK, n_s // TOK
    return pl.pallas_call(
        functools.partial(_dispatch_kernel, n_p_tiles=npt),
        grid=(npt + nst,),
        in_specs=[pl.BlockSpec((2 * TOK,), lambda i: (i,), memory_space=pltpu.SMEM),
                  pl.BlockSpec((TOK, D), lambda i: (jnp.minimum(i, npt - 1), 0)),
                  pl.BlockSpec((TOK, D), lambda i: (jnp.maximum(i - npt, 0), 0))],
        out_specs=pl.BlockSpec(memory_space=pl.ANY),
        out_shape=jax.ShapeDtypeStruct((2 * (n_p + n_s), D), f32),
        scratch_shapes=[pltpu.SemaphoreType.DMA(())],
        compiler_params=_cp("arbitrary"),
        name="moe_dispatch",
    )(pos, hp, hs)


def _experts_kernel(tile_ref, exp_ref, lo_ref, hi_ref, xs_ref, wg_ref, wu_ref, wd_ref, o_ref):
    del tile_ref, exp_ref
    k = pl.program_id(0)
    lo = lo_ref[k]
    hi = hi_ref[k]

    @pl.when(hi > lo)
    def _():
        x = xs_ref[...].astype(bf16)
        ff = wg_ref.shape[2]
        acc = jnp.zeros(o_ref.shape, f32)
        for c0 in range(0, ff, FF_CHUNK_MOE):
            c1 = min(c0 + FF_CHUNK_MOE, ff)
            a = _dot(x, wg_ref[0, :, c0:c1])
            a = a * _sigmoid(a) * _dot(x, wu_ref[0, :, c0:c1])
            acc = acc + _dot(a.astype(bf16), wd_ref[0, c0:c1, :])
        @pl.when(lo == 0)
        def _():
            o_ref[...] = acc

        @pl.when(lo > 0)
        def _():
            row = lax.broadcasted_iota(jnp.int32, (o_ref.shape[0], 1), 0)
            o_ref[...] = jnp.where((row >= lo) & (row < hi), acc, o_ref[...])


def _experts(items, xs, wg, wu, wd):
    n_rows = xs.shape[0]
    ff = wg.shape[2]
    tm = TM_MOE
    assert n_rows % tm == 0
    n_items = items[0].shape[0]
    once = pl.Buffered(1)
    grid_spec = pltpu.PrefetchScalarGridSpec(
        num_scalar_prefetch=4,
        grid=(n_items,),
        in_specs=[pl.BlockSpec((tm, D), lambda k, ti, ex, lo, hi: (ti[k], 0)),
                  pl.BlockSpec((1, D, ff), lambda k, ti, ex, lo, hi: (ex[k], 0, 0), pipeline_mode=once),
                  pl.BlockSpec((1, D, ff), lambda k, ti, ex, lo, hi: (ex[k], 0, 0), pipeline_mode=once),
                  pl.BlockSpec((1, ff, D), lambda k, ti, ex, lo, hi: (ex[k], 0, 0), pipeline_mode=once)],
        out_specs=pl.BlockSpec((tm, D), lambda k, ti, ex, lo, hi: (ti[k], 0)))
    return pl.pallas_call(
        _experts_kernel,
        grid_spec=grid_spec,
        out_shape=jax.ShapeDtypeStruct((n_rows, D), f32),
        compiler_params=_cp("arbitrary"),
        name="moe_experts",
    )(*items, xs, wg, wu, wd)


def _work_items(off, n_rows):
    tm = TM_MOE
    n_tiles = n_rows // tm
    n_items = n_tiles + NE - 1
    start = off
    end = jnp.concatenate([off[1:], jnp.array([n_rows], jnp.int32)])
    cnt = end - start
    first = start // tm
    last = jnp.where(cnt > 0, (end - 1) // tm, first - 1)
    per = last - first + 1
    cum = jnp.cumsum(per)
    k = jnp.arange(n_items, dtype=jnp.int32)
    e = jnp.minimum(jnp.sum(k[:, None] >= cum[None, :], axis=1), NE - 1).astype(jnp.int32)
    tile = first[e] + (k - (cum[e] - per[e]))
    real = k < cum[NE - 1]
    e = jnp.where(real, e, jnp.max(jnp.where(real, e, 0)))
    tile = jnp.where(real, tile, n_tiles - 1).astype(jnp.int32)
    lo = jnp.clip(start[e] - tile * tm, 0, tm)
    hi = jnp.clip(end[e] - tile * tm, 0, tm)
    lo = jnp.where(real, lo, 0).astype(jnp.int32)
    hi = jnp.where(real, hi, 0).astype(jnp.int32)
    return tile, e, lo, hi


def _combine_kernel(pos_ref, pos_next_ref, x3_ref, route_ref, o_hbm, y_ref, buf, sem, *, n_tiles):
    i = pl.program_id(0)
    tok = x3_ref.shape[0]
    slot = i % 2

    def gather(p_ref, sl):
        def issue(r, c):
            _row_copy(o_hbm, p_ref[r], buf.at[sl, 0], r, sem.at[sl]).start(priority=0)
            _row_copy(o_hbm, p_ref[tok + r], buf.at[sl, 1], r, sem.at[sl]).start(priority=1)
            return c

        lax.fori_loop(0, tok, issue, 0, unroll=ROW_DMA_UNROLL)

    @pl.when(i == 0)
    def _():
        gather(pos_ref, slot)

    if n_tiles > 1:
        @pl.when(i + 1 < n_tiles)
        def _():
            gather(pos_next_ref, 1 - slot)

    for s in range(2):
        pltpu.make_async_copy(o_hbm.at[pl.ds(0, tok)], buf.at[slot, s], sem.at[slot]).wait()
    route = route_ref[...]
    y_ref[...] = x3_ref[...] + route[:, 2:3] * buf[slot, 0] + route[:, 3:4] * buf[slot, 1]


def _combine(pos, x3, route, o_sorted, tile0):
    n = x3.shape[0]
    assert n % TOK == 0
    nt = n // TOK
    pos_spec = lambda d: pl.BlockSpec((2 * TOK,), lambda i: (jnp.minimum(i + d, nt - 1) + tile0,),
                                      memory_space=pltpu.SMEM)
    return pl.pallas_call(
        functools.partial(_combine_kernel, n_tiles=nt),
        grid=(nt,),
        in_specs=[pos_spec(0), pos_spec(1),
                  pl.BlockSpec((TOK, D), lambda i: (i, 0)),
                  pl.BlockSpec((TOK, LANES), lambda i: (i, 0)),
                  pl.BlockSpec(memory_space=pl.ANY)],
        out_specs=pl.BlockSpec((TOK, D), lambda i: (i, 0)),
        out_shape=jax.ShapeDtypeStruct((n, D), f32),
        scratch_shapes=[pltpu.VMEM((2, 2, TOK, D), f32), pltpu.SemaphoreType.DMA((2,))],
        compiler_params=_cp("arbitrary"),
        name="moe_combine",
    )(pos, pos, x3, route, o_sorted)


def kernel(x_prompt, x_sample, state_conv, cache_k_win, cache_v_win, g_conv_norm, w_pw1, b_pw1, w_dw, b_dw, g_ln, b_ln, w_pw2, b_pw2, g_kv_norm, w_kv, g_k_norm, g_attn_norm, w_q, g_q_norm, sinks, w_o, rel_bias, g_ffn_norm, w_gate, w_up, w_down, g_moe_norm, w_router, w_e_gate, w_e_up, w_e_down):
    b, t, _ = x_prompt.shape
    bd, s_len, _ = x_sample.shape
    n_p = b * t
    n_s = bd * s_len
    row = lambda a: a.reshape(1, -1).astype(f32)

    conv_w = (row(g_conv_norm[0]), w_pw1[0].astype(bf16), row(b_pw1[0]),
              jnp.pad(w_dw[0], ((0, HALO - CONV_W), (0, 0))), row(b_dw[0]), row(g_ln[0]), row(b_ln[0]),
              w_pw2[0].astype(bf16), row(b_pw2[0]))
    ffn_w = (row(g_ffn_norm[0]), w_gate[0].astype(bf16), w_up[0].astype(bf16), w_down[0].astype(bf16))
    wq = w_q[0].reshape(D, NKV, G, DH).transpose(0, 2, 1, 3).reshape(D, NH * DH).astype(bf16)
    wo = w_o[0].reshape(NKV, G, DH, D).transpose(1, 0, 2, 3).reshape(NH * DH, D).astype(bf16)
    wr = jnp.pad(w_router[0], ((0, 0), (0, LANES - NE))).astype(bf16)
    head_of_lane = np.repeat(np.eye(NH, LANES), DH, axis=0)
    attn_w = (row(g_kv_norm), w_kv.astype(bf16), row(jnp.tile(g_k_norm, NKV)), row(g_attn_norm[0]),
              wq, row(jnp.tile(g_q_norm[0], NH)), jnp.asarray(head_of_lane, bf16),
              jnp.asarray(head_of_lane.T, bf16), wo, row(g_moe_norm[0]), wr)
    weg, weu, wed = w_e_gate[0].astype(bf16), w_e_up[0].astype(bf16), w_e_down[0].astype(bf16)
    sink = sinks[0].astype(f32)

    bias_p = _bias_table(rel_bias, _prompt_bucket_map()).reshape(2, NKV, G * WIN, 2 * WIN)
    map_c, map_n = _sample_bucket_maps(s_len, SEQ_TILE)
    rows = s_len * SEQ_TILE
    bias_c = _bias_table(rel_bias, map_c).reshape(NH * rows, SEQ_TILE * WIN)
    bias_n = _bias_table(rel_bias, map_n).reshape(NH * rows, LANES)

    x1p, ulast = _conv_prompt(x_prompt, conv_w)
    x2p = _ffn(x1p.reshape(n_p, D), *ffn_w).reshape(b, t, D)
    x3p, hp, route_p, route_tp, klast, vlast = _attn_prompt(x2p, sink, bias_p, attn_w)

    xs_tb = x_sample.transpose(1, 0, 2)
    x1s, snew = _conv_sample(xs_tb, state_conv[0], conv_w)
    x2s = _ffn(x1s.reshape(n_s, D), *ffn_w)
    kc = cache_k_win.reshape(bd, WIN, KVW)
    vc = cache_v_win.reshape(bd, WIN, KVW)
    x3s, hs, route_s, route_ts, kwin, vwin = _attn_sample(x2s, kc, vc, sink, bias_c, bias_n, attn_w)

    pos, off = _positions(jnp.concatenate([route_tp, route_ts], axis=1))
    pos = pos.reshape(-1)
    n_rows = 2 * (n_p + n_s)
    xs = _dispatch(pos, hp, hs)
    o_sorted = _experts(_work_items(off[:, 0], n_rows), xs, weg, weu, wed)
    y_p = _combine(pos, x3p.reshape(n_p, D), route_p, o_sorted, 0).reshape(b, t, D)
    y_s = _combine(pos, x3s, route_s, o_sorted, n_p // TOK).reshape(s_len, bd, D).transpose(1, 0, 2)

    n_prev = CONV_W - 1
    conv_p = ulast[:, HALO - n_prev:, :][None]
    conv_s = snew[None]
    k_p = klast.reshape(b, WIN, NKV, DH)
    v_p = vlast.reshape(b, WIN, NKV, DH)
    k_s = kwin.reshape(bd, WIN, NKV, DH)
    v_s = vwin.reshape(bd, WIN, NKV, DH)
    return (y_p, y_s, conv_p, conv_s, k_p, k_s, v_p, v_s)
```

```python
import functools

import numpy as np
import jax
import jax.numpy as jnp
from jax import lax
from jax.experimental import pallas as pl
from jax.experimental.pallas import tpu as pltpu

D = 1024
CONV_W = 31
HALO = 32
DH = 64
NH = 16
NKV = 4
G = NH // NKV
KVW = NKV * DH
WIN = 128
NB = 32
MAXD = 128
NE = 8
EPS = 1e-6
LANES = 128
NEG = float("-inf")

TT_CONV = 512
TM_FFN = 1024
TQ_ATT = 512
TOK = 512
TM_MOE = 512
ROW_DMA_UNROLL = 32
SEQ_TILE = 8
SEQ_TILE_CONV = 32
FF_CHUNK = 1792
FF_CHUNK_MOE = 512
VMEM_LIMIT = 56 * 1024 * 1024

bf16 = jnp.bfloat16
f32 = jnp.float32


def _cp(*sem):
    return pltpu.CompilerParams(dimension_semantics=sem, vmem_limit_bytes=VMEM_LIMIT)


def _full(shape, once=False):
    n = len(shape)
    return pl.BlockSpec(shape, lambda *_: (0,) * n, pipeline_mode=pl.Buffered(1) if once else None)


def _sigmoid(x):
    return 0.5 * jnp.tanh(0.5 * x) + 0.5


def _rms(x):
    return x * lax.rsqrt(jnp.mean(x * x, axis=-1, keepdims=True) + EPS)


def _dot(a, b):
    return jnp.dot(a, b, preferred_element_type=f32)


def _dot_nt(a, b):
    return lax.dot_general(a, b, (((1,), (1,)), ((), ())), preferred_element_type=f32)


def _split_dot(x, w):
    hi = x.astype(bf16)
    lo = (x - hi.astype(f32)).astype(bf16)
    return _dot(hi, w) + _dot(lo, w)


def _head_rms(x, gain, ind_ref, spread_ref):
    c = x.shape[1]
    ss = _dot((x * x).astype(bf16), ind_ref[0:c, :])
    r = lax.rsqrt(ss * (1.0 / DH) + EPS)
    return x * _split_dot(r, spread_ref[:, 0:c]) * gain


def _kv_lane_mask(kv):
    lane = lax.broadcasted_iota(jnp.int32, (1, KVW), 1)
    return (lane >= kv * DH) & (lane < (kv + 1) * DH)


def _bucket_of(dist):
    n = np.maximum(dist, 0)
    max_exact = NB // 2
    large = max_exact + (np.log(np.maximum(n, 1) / max_exact) / np.log(MAXD / max_exact)
                         * (NB - max_exact)).astype(np.int32)
    large = np.minimum(large, NB - 1)
    return np.where(n < max_exact, n, large).astype(np.int32)


def _prompt_bucket_map():
    dist = np.arange(WIN)[:, None] + WIN - np.arange(2 * WIN)[None, :]
    valid = (dist >= 0) & (dist < WIN)
    bk = np.where(valid, _bucket_of(dist), -1).astype(np.int32)
    first = bk.copy()
    first[:, :WIN] = -1
    return np.stack([first, bk])


def _sample_bucket_maps(s_len, st):
    r_t = np.repeat(np.arange(s_len), st)
    r_b = np.tile(np.arange(st), s_len)
    c_b = np.repeat(np.arange(st), WIN)
    c_s = np.tile(np.arange(WIN), st)
    dist = r_t[:, None] + WIN - c_s[None, :]
    valid = (r_b[:, None] == c_b[None, :]) & (dist >= 0) & (dist < WIN)
    map_c = np.where(valid, _bucket_of(dist), -1).astype(np.int32)
    dist_n = r_t[:, None] - r_t[None, :]
    valid_n = (r_b[:, None] == r_b[None, :]) & (dist_n >= 0) & (dist_n < WIN)
    map_n = np.full((s_len * st, LANES), -1, np.int32)
    map_n[:, :s_len * st] = np.where(valid_n, _bucket_of(dist_n), -1)
    return map_c[None], map_n[None]


def _bias_kernel(rb_ref, map_ref, out_ref):
    bk = map_ref[0]
    for h in range(NH):
        acc = jnp.full(bk.shape, NEG, f32)
        for b in range(NB):
            acc = jnp.where(bk == b, rb_ref[b, h], acc)
        out_ref[0, h] = acc


def _bias_table(rel_bias, bucket_map):
    v, r, c = bucket_map.shape
    return pl.pallas_call(
        _bias_kernel,
        grid=(v,),
        in_specs=[pl.BlockSpec(memory_space=pltpu.SMEM),
                  pl.BlockSpec((1, r, c), lambda i: (i, 0, 0))],
        out_specs=pl.BlockSpec((1, NH, r, c), lambda i: (i, 0, 0, 0)),
        out_shape=jax.ShapeDtypeStruct((v, NH, r, c), f32),
        compiler_params=_cp("arbitrary"),
        name="bias_table",
    )(rel_bias, jnp.asarray(bucket_map))


def _glu_rows(x, gn_ref, w1_ref, b1_ref):
    h = (_rms(x) * gn_ref[...]).astype(bf16)
    u2 = _dot(h, w1_ref[...]) + b1_ref[...]
    return u2[:, :D] * _sigmoid(u2[:, D:])


def _conv_tail(x, c, gln_ref, bln_ref, w2_ref, b2_ref):
    mu = jnp.mean(c, axis=-1, keepdims=True)
    xc = c - mu
    var = jnp.mean(xc * xc, axis=-1, keepdims=True)
    y = xc * lax.rsqrt(var + EPS) * gln_ref[...] + bln_ref[...]
    y = y * _sigmoid(y)
    return x + _dot(y.astype(bf16), w2_ref[...]) + b2_ref[...]


def _swiglu_rows(x, g_ref, wg_ref, wu_ref, wd_ref):
    h = (_rms(x) * g_ref[...]).astype(bf16)
    ff = wg_ref.shape[1]
    acc = x
    for c0 in range(0, ff, FF_CHUNK):
        c1 = min(c0 + FF_CHUNK, ff)
        a = _dot(h, wg_ref[:, c0:c1])
        a = a * _sigmoid(a) * _dot(h, wu_ref[:, c0:c1])
        acc = acc + _dot(a.astype(bf16), wd_ref[c0:c1, :])
    return acc


def _conv_prompt_kernel(x_ref, gn_ref, w1_ref, b1_ref, wdw_ref, bdw_ref, gln_ref, bln_ref,
                        w2_ref, b2_ref, x1_ref, ulast_ref, ubuf, ush):
    tt = x_ref.shape[1]
    t = pl.program_id(1)
    x = x_ref[0]
    u = _glu_rows(x, gn_ref, w1_ref, b1_ref)

    @pl.when(t == 0)
    def _():
        ubuf[0:HALO, :] = jnp.zeros((HALO, D), f32)

    @pl.when(t > 0)
    def _():
        ubuf[0:HALO, :] = ubuf[tt:tt + HALO, :]

    ubuf[HALO:HALO + tt, :] = u
    ulast_ref[0] = u[tt - HALO:, :]
    n_sh = ush.shape[1]
    for r in range(1, 8):
        ush[r - 1] = ubuf[r:r + n_sh, :]
    off = HALO - (CONV_W - 1)
    c = jnp.broadcast_to(bdw_ref[...], (tt, D))
    for k in range(CONV_W):
        a, r = divmod(off + k, 8)
        src = ubuf[8 * a:8 * a + tt, :] if r == 0 else ush[r - 1, 8 * a:8 * a + tt, :]
        c = c + wdw_ref[k:k + 1, :] * src
    x1_ref[0] = _conv_tail(x, c, gln_ref, bln_ref, w2_ref, b2_ref)


def _conv_prompt(x, wts):
    b, t, _ = x.shape
    tt = min(TT_CONV, t)
    assert t % tt == 0 and tt >= HALO
    row = lambda n: _full((1, n))
    return pl.pallas_call(
        _conv_prompt_kernel,
        grid=(b, t // tt),
        in_specs=[pl.BlockSpec((1, tt, D), lambda i, j: (i, j, 0)),
                  row(D), _full((D, 2 * D)), row(2 * D), _full((HALO, D)), row(D), row(D), row(D),
                  _full((D, D)), row(D)],
        out_specs=[pl.BlockSpec((1, tt, D), lambda i, j: (i, j, 0)),
                   pl.BlockSpec((1, HALO, D), lambda i, j: (i, 0, 0))],
        out_shape=[jax.ShapeDtypeStruct((b, t, D), f32),
                   jax.ShapeDtypeStruct((b, HALO, D), f32)],
        scratch_shapes=[pltpu.VMEM((tt + HALO, D), f32),
                        pltpu.VMEM((7, tt + HALO - 8, D), f32)],
        compiler_params=_cp("arbitrary", "arbitrary"),
        name="conv_prompt",
    )(x, *wts)


def _conv_sample_kernel(x_ref, st_ref, gn_ref, w1_ref, b1_ref, wdw_ref, bdw_ref, gln_ref, bln_ref,
                        w2_ref, b2_ref, x1_ref, snew_ref):
    s_len, bs, _ = x_ref.shape
    x = x_ref[...].reshape(s_len * bs, D)
    u = _glu_rows(x, gn_ref, w1_ref, b1_ref)
    n_prev = CONV_W - 1
    snew_ref[:, 0:n_prev - s_len, :] = st_ref[:, s_len:, :]
    for t in range(s_len):
        for b in range(bs):
            snew_ref[b, n_prev - s_len + t:n_prev - s_len + t + 1, :] = u[t * bs + b:t * bs + b + 1, :]
    cs = []
    for t in range(s_len):
        c = jnp.broadcast_to(bdw_ref[...], (bs, D))
        for j in range(t, n_prev):
            c = c + wdw_ref[j - t:j - t + 1, :] * st_ref[:, j, :]
        for i in range(t + 1):
            k = n_prev - t + i
            c = c + wdw_ref[k:k + 1, :] * u[i * bs:(i + 1) * bs, :]
        cs.append(c)
    c = jnp.concatenate(cs, axis=0)
    x1_ref[...] = _conv_tail(x, c, gln_ref, bln_ref, w2_ref, b2_ref).reshape(s_len, bs, D)


def _conv_sample(x_tb, state, wts):
    s_len, bd, _ = x_tb.shape
    bs = min(SEQ_TILE_CONV, bd)
    assert bd % bs == 0 and bs % 8 == 0 and s_len <= CONV_W - 1
    row = lambda n: _full((1, n))
    blk = pl.BlockSpec((s_len, bs, D), lambda i: (0, i, 0))
    seq = pl.BlockSpec((bs, CONV_W - 1, D), lambda i: (i, 0, 0))
    return pl.pallas_call(
        _conv_sample_kernel,
        grid=(bd // bs,),
        in_specs=[blk, seq,
                  row(D), _full((D, 2 * D)), row(2 * D), _full((HALO, D)), row(D), row(D), row(D),
                  _full((D, D)), row(D)],
        out_specs=[blk, seq],
        out_shape=[jax.ShapeDtypeStruct((s_len, bd, D), f32),
                   jax.ShapeDtypeStruct((bd, CONV_W - 1, D), f32)],
        compiler_params=_cp("arbitrary"),
        name="conv_sample",
    )(x_tb, state, *wts)


def _ffn_kernel(x_ref, g_ref, wg_ref, wu_ref, wd_ref, o_ref):
    o_ref[...] = _swiglu_rows(x_ref[...], g_ref, wg_ref, wu_ref, wd_ref)


def _ffn(x, g, wg, wu, wd):
    n = x.shape[0]
    tm = min(TM_FFN, n)
    assert n % tm == 0
    ff = wg.shape[1]
    return pl.pallas_call(
        _ffn_kernel,
        grid=(n // tm,),
        in_specs=[pl.BlockSpec((tm, D), lambda i: (i, 0)), _full((1, D)),
                  _full((D, ff)), _full((D, ff)), _full((ff, D))],
        out_specs=pl.BlockSpec((tm, D), lambda i: (i, 0)),
        out_shape=jax.ShapeDtypeStruct((n, D), f32),
        compiler_params=_cp("arbitrary"),
        name="dense_ffn",
    )(x, g, wg, wu, wd)


def _qkv(x, gkv_ref, wkv_ref, gk_ref, gq_attn_ref, wq_ref, gq_ref, ind_ref, spread_ref):
    xn = _rms(x)
    kv = _dot((xn * gkv_ref[...]).astype(bf16), wkv_ref[...])
    k = _head_rms(kv[:, :KVW], gk_ref[...], ind_ref, spread_ref)
    v = kv[:, KVW:]
    q = _dot((xn * gq_attn_ref[...]).astype(bf16), wq_ref[...])
    q = _head_rms(q, gq_ref[...], ind_ref, spread_ref) * (DH ** -0.5)
    return q, k, v


def _sink_col(sinks_ref, kv, rows_per_head):
    return jnp.concatenate(
        [jnp.full((rows_per_head, 1), sinks_ref[kv * G + g], f32) for g in range(G)], axis=0)


def _moe_prep(x3, gmoe_ref, wr_ref, h_ref, route_ref, route_t_ref):
    h = _rms(x3) * gmoe_ref[...]
    h_ref[...] = h
    logits = _dot(h.astype(bf16), wr_ref[...])
    lane = lax.broadcasted_iota(jnp.int32, logits.shape, 1).astype(f32)
    logits = jnp.where(lane < NE, logits, NEG)
    m1 = jnp.max(logits, axis=-1, keepdims=True)
    i1 = jnp.min(jnp.where(logits == m1, lane, float(LANES)), axis=-1, keepdims=True)
    rest = jnp.where(lane == i1, NEG, logits)
    m2 = jnp.max(rest, axis=-1, keepdims=True)
    i2 = jnp.min(jnp.where(rest == m2, lane, float(LANES)), axis=-1, keepdims=True)
    e2 = jnp.exp(m2 - m1)
    den = 1.0 + e2
    g1 = 1.0 / den
    g2 = e2 / den
    route = jnp.where(lane == 0, i1,
                      jnp.where(lane == 1, i2,
                                jnp.where(lane == 2, g1, jnp.where(lane == 3, g2, 0.0))))
    route_ref[...] = route
    route_t_ref[...] = route.T[0:8, :]


def _attn_prompt_kernel(sinks_ref, x_ref, gkv_ref, wkv_ref, gk_ref, gqa_ref, wq_ref, gq_ref,
                        ind_ref, spread_ref, ones_ref, bias_ref, wo_ref, gmoe_ref, wr_ref,
                        x3_ref, h_ref, route_ref, route_t_ref, klast_ref, vlast_ref,
                        kbuf, vbuf, obuf):
    tq = x_ref.shape[1]
    t = pl.program_id(1)
    x = x_ref[0]
    q, k, v = _qkv(x, gkv_ref, wkv_ref, gk_ref, gqa_ref, wq_ref, gq_ref, ind_ref, spread_ref)
    klast_ref[0] = k[tq - WIN:, :]
    vlast_ref[0] = v[tq - WIN:, :]

    @pl.when(t == 0)
    def _():
        kbuf[:, 0:WIN, :] = jnp.zeros((NKV, WIN, KVW), bf16)
        vbuf[:, 0:WIN, :] = jnp.zeros((NKV, WIN, KVW), bf16)

    @pl.when(t > 0)
    def _():
        kbuf[:, 0:WIN, :] = kbuf[:, tq:tq + WIN, :]
        vbuf[:, 0:WIN, :] = vbuf[:, tq:tq + WIN, :]

    kb = k.astype(bf16)
    vb = v.astype(bf16)
    for kv in range(NKV):
        m = _kv_lane_mask(kv)
        kbuf[kv, WIN:WIN + tq, :] = jnp.where(m, kb, jnp.zeros_like(kb))
        vbuf[kv, WIN:WIN + tq, :] = jnp.where(m, vb, jnp.zeros_like(vb))

    qb = q.astype(bf16)
    for n in range(tq // WIN):
        r0 = n * WIN
        lhs = jnp.concatenate([qb[r0:r0 + WIN, g * KVW:(g + 1) * KVW] for g in range(G)], axis=0)
        ps = []
        lane = lax.broadcasted_iota(jnp.int32, (1, LANES), 1)
        sink_term = jnp.zeros((G * WIN, LANES), f32)
        for kv in range(NKV):
            s = _dot_nt(lhs, kbuf[kv, r0:r0 + 2 * WIN, :])
            s = s + bias_ref[jnp.where(t == 0, 0, 1) if n == 0 else 1, kv]
            sink = _sink_col(sinks_ref, kv, WIN)
            mx = jnp.maximum(jnp.max(s, axis=-1, keepdims=True), sink)
            ps.append(jnp.exp(s - mx).astype(bf16))
            sink_term = jnp.where(lane == kv, jnp.exp(sink - mx), sink_term)
        pcat = jnp.concatenate(ps, axis=-1)
        vcat = jnp.concatenate([vbuf[kv, r0:r0 + 2 * WIN, :] for kv in range(NKV)], axis=0)
        den = _dot(pcat, ones_ref[...]) + sink_term
        inv = 1.0 / jnp.where(lane < NKV, den, 1.0)
        pv = _dot(pcat, vcat) * _split_dot(inv, spread_ref[:, 0:KVW])
        obuf[r0:r0 + WIN, :] = jnp.concatenate(
            [pv[g * WIN:(g + 1) * WIN, :] for g in range(G)], axis=-1).astype(bf16)

    x3 = x + _dot(obuf[...], wo_ref[...])
    x3_ref[0] = x3
    _moe_prep(x3, gmoe_ref, wr_ref, h_ref, route_ref, route_t_ref)


def _attn_prompt(x2, sinks, bias, wts):
    b, t, _ = x2.shape
    tq = min(TQ_ATT, t)
    assert t % tq == 0 and tq % WIN == 0
    nt = t // tq
    gkv, wkv, gk, gqa, wq, gq, ind, spread, wo, gmoe, wr = wts
    ones = jnp.asarray(np.repeat(np.eye(NKV, LANES), 2 * WIN, axis=0), bf16)
    row = lambda n: _full((1, n))
    tok = lambda w: pl.BlockSpec((tq, w), lambda i, j: (i * nt + j, 0))
    last = pl.BlockSpec((1, WIN, KVW), lambda i, j: (i, 0, 0))
    return pl.pallas_call(
        _attn_prompt_kernel,
        grid=(b, nt),
        in_specs=[pl.BlockSpec(memory_space=pltpu.SMEM),
                  pl.BlockSpec((1, tq, D), lambda i, j: (i, j, 0)),
                  row(D), _full((D, 2 * KVW)), row(KVW), row(D), _full((D, D)), row(D),
                  _full(ind.shape), _full(spread.shape), _full(ones.shape),
                  _full(bias.shape), _full((D, D)), row(D), _full((D, LANES))],
        out_specs=[pl.BlockSpec((1, tq, D), lambda i, j: (i, j, 0)), tok(D), tok(LANES),
                   pl.BlockSpec((8, tq), lambda i, j: (0, i * nt + j)), last, last],
        out_shape=[jax.ShapeDtypeStruct((b, t, D), f32),
                   jax.ShapeDtypeStruct((b * t, D), f32),
                   jax.ShapeDtypeStruct((b * t, LANES), f32),
                   jax.ShapeDtypeStruct((8, b * t), f32),
                   jax.ShapeDtypeStruct((b, WIN, KVW), f32),
                   jax.ShapeDtypeStruct((b, WIN, KVW), f32)],
        scratch_shapes=[pltpu.VMEM((NKV, tq + WIN, KVW), bf16),
                        pltpu.VMEM((NKV, tq + WIN, KVW), bf16),
                        pltpu.VMEM((tq, D), bf16)],
        compiler_params=_cp("arbitrary", "arbitrary"),
        name="attn_prompt",
    )(sinks, x2, gkv, wkv, gk, gqa, wq, gq, ind, spread, ones, bias, wo, gmoe, wr)


def _attn_sample_kernel(sinks_ref, x_ref, kc_ref, vc_ref, gkv_ref, wkv_ref, gk_ref, gqa_ref, wq_ref,
                        gq_ref, ind_ref, spread_ref, bias_c_ref, bias_n_ref, wo_ref, gmoe_ref, wr_ref,
                        x3_ref, h_ref, route_ref, route_t_ref, kwin_ref, vwin_ref,
                        qbuf, kbuf, vbuf, obuf, *, bd):
    st = kc_ref.shape[0]
    s_len = x_ref.shape[0] // bd
    rows = s_len * st
    i = pl.program_id(0)

    @pl.when(i == 0)
    def _():
        q, k, v = _qkv(x_ref[...], gkv_ref, wkv_ref, gk_ref, gqa_ref, wq_ref, gq_ref, ind_ref, spread_ref)
        kbuf[...] = k
        vbuf[...] = v
        qbuf[...] = q

    def step_rows(buf, t):
        return buf[pl.ds(pl.multiple_of(t * bd + i * st, 8), st), :]

    def tile_rows(buf):
        return jnp.concatenate([step_rows(buf, t) for t in range(s_len)], axis=0).astype(bf16)

    for win_ref, c_ref, nbuf in ((kwin_ref, kc_ref, kbuf), (vwin_ref, vc_ref, vbuf)):
        win_ref[:, 0:WIN - s_len, :] = c_ref[:, s_len:, :]
        for t in range(s_len):
            new = step_rows(nbuf, t)
            for b in range(st):
                win_ref[b, WIN - s_len + t:WIN - s_len + t + 1, :] = new[b:b + 1, :]

    qt = tile_rows(qbuf)
    lhs = jnp.concatenate([qt[:, g * KVW:(g + 1) * KVW] for g in range(G)], axis=0)
    pad = jnp.zeros((LANES - rows, KVW), bf16)
    kn = jnp.concatenate([tile_rows(kbuf), pad], axis=0)
    vn = jnp.concatenate([tile_rows(vbuf), pad], axis=0)
    kc = kc_ref[...].reshape(st * WIN, KVW).astype(bf16)
    vc = vc_ref[...].reshape(st * WIN, KVW).astype(bf16)
    pv = jnp.zeros((G * rows, KVW), f32)
    for kv in range(NKV):
        m = _kv_lane_mask(kv)
        zc = jnp.zeros_like(kc)
        zn = jnp.zeros_like(kn)
        b0 = kv * G * rows
        s_c = _dot_nt(lhs, jnp.where(m, kc, zc)) + bias_c_ref[b0:b0 + G * rows, :]
        s_n = _dot_nt(lhs, jnp.where(m, kn, zn)) + bias_n_ref[b0:b0 + G * rows, :]
        sink = _sink_col(sinks_ref, kv, rows)
        mx = jnp.maximum(jnp.maximum(jnp.max(s_c, axis=-1, keepdims=True),
                                     jnp.max(s_n, axis=-1, keepdims=True)), sink)
        p_c = jnp.exp(s_c - mx)
        p_n = jnp.exp(s_n - mx)
        den = (jnp.sum(p_c, axis=-1, keepdims=True) + jnp.sum(p_n, axis=-1, keepdims=True)
               + jnp.exp(sink - mx))
        inv = 1.0 / den
        pv = pv + _dot((p_c * inv).astype(bf16), jnp.where(m, vc, zc))
        pv = pv + _dot((p_n * inv).astype(bf16), jnp.where(m, vn, zn))
    o = jnp.concatenate([pv[g * rows:(g + 1) * rows, :] for g in range(G)], axis=-1)
    for t in range(s_len):
        obuf[pl.ds(pl.multiple_of(t * bd + i * st, 8), st), :] = o[t * st:(t + 1) * st, :]

    @pl.when(i == pl.num_programs(0) - 1)
    def _():
        x3 = x_ref[...] + _dot(obuf[...].astype(bf16), wo_ref[...])
        x3_ref[...] = x3
        _moe_prep(x3, gmoe_ref, wr_ref, h_ref, route_ref, route_t_ref)


def _attn_sample(x2, kc, vc, sinks, bias_c, bias_n, wts):
    n_tok = x2.shape[0]
    bd = kc.shape[0]
    st = SEQ_TILE
    assert bd % st == 0 and n_tok % bd == 0 and (n_tok // bd) * st <= LANES
    gkv, wkv, gk, gqa, wq, gq, ind, spread, wo, gmoe, wr = wts
    row = lambda n: _full((1, n))
    cache = pl.BlockSpec((st, WIN, KVW), lambda i: (i, 0, 0))
    return pl.pallas_call(
        functools.partial(_attn_sample_kernel, bd=bd),
        grid=(bd // st,),
        in_specs=[pl.BlockSpec(memory_space=pltpu.SMEM), _full((n_tok, D)), cache, cache,
                  row(D), _full((D, 2 * KVW)), row(KVW), row(D), _full((D, D)), row(D),
                  _full(ind.shape), _full(spread.shape), _full(bias_c.shape), _full(bias_n.shape), _full((D, D)), row(D), _full((D, LANES))],
        out_specs=[_full((n_tok, D)), _full((n_tok, D)), _full((n_tok, LANES)), _full((8, n_tok)),
                   cache, cache],
        out_shape=[jax.ShapeDtypeStruct((n_tok, D), f32),
                   jax.ShapeDtypeStruct((n_tok, D), f32),
                   jax.ShapeDtypeStruct((n_tok, LANES), f32),
                   jax.ShapeDtypeStruct((8, n_tok), f32),
                   jax.ShapeDtypeStruct((bd, WIN, KVW), f32),
                   jax.ShapeDtypeStruct((bd, WIN, KVW), f32)],
        scratch_shapes=[pltpu.VMEM((n_tok, D), f32), pltpu.VMEM((n_tok, KVW), f32),
                        pltpu.VMEM((n_tok, KVW), f32), pltpu.VMEM((n_tok, D), f32)],
        compiler_params=_cp("arbitrary"),
        name="attn_sample",
    )(sinks, x2, kc, vc, gkv, wkv, gk, gqa, wq, gq, ind, spread, bias_c, bias_n, wo, gmoe, wr)


def _positions_kernel(rt_ref, pos_ref, off_ref, counts, running, offs):
    ph = pl.program_id(0)
    i = pl.program_id(1)
    n_sub, _, tok = pos_ref.shape
    sub = lax.broadcasted_iota(jnp.int32, (NE, tok), 0).astype(f32)

    def tile(s):
        e0 = rt_ref[0:1, s * tok:(s + 1) * tok]
        e1 = rt_ref[1:2, s * tok:(s + 1) * tok]
        sel = ((sub == e0) | (sub == e1)).astype(f32)
        return e0, e1, sel, jnp.broadcast_to(jnp.sum(sel, axis=-1, keepdims=True), (NE, LANES))

    @pl.when((ph == 0) & (i == 0))
    def _():
        counts[...] = jnp.zeros((NE, LANES), f32)

    @pl.when(ph == 0)
    def _():
        for s in range(n_sub):
            counts[...] += tile(s)[3]

    @pl.when((ph == 1) & (i == 0))
    def _():
        sub_l = lax.broadcasted_iota(jnp.int32, (NE, LANES), 0)
        acc = jnp.zeros((NE, LANES), f32)
        for e in range(NE - 1):
            acc = acc + jnp.where(sub_l > e, counts[e:e + 1, :], 0.0)
        offs[...] = acc
        running[...] = jnp.zeros((NE, LANES), f32)
        off_ref[...] = acc.astype(jnp.int32)

    @pl.when(ph == 1)
    def _():
        r_i = lax.broadcasted_iota(jnp.int32, (tok, tok), 0)
        c_i = lax.broadcasted_iota(jnp.int32, (tok, tok), 1)
        upper = (r_i < c_i).astype(bf16)
        for s in range(n_sub):
            e0, e1, sel, tile_cnt = tile(s)
            cum = _dot(sel.astype(bf16), upper)
            tot = cum + (offs[:, 0:1] + running[:, 0:1])
            p0 = jnp.sum(jnp.where(sub == e0, tot, 0.0), axis=0, keepdims=True)
            p1 = jnp.sum(jnp.where(sub == e1, tot, 0.0), axis=0, keepdims=True)
            pos_ref[s] = jnp.concatenate([p0, p1], axis=0).astype(jnp.int32)
            running[...] += tile_cnt


def _positions(route_t):
    n = route_t.shape[1]
    assert n % TOK == 0
    nt = n // TOK
    n_sub = max(d for d in range(1, 9) if nt % d == 0)
    return pl.pallas_call(
        _positions_kernel,
        grid=(2, nt // n_sub),
        in_specs=[pl.BlockSpec((8, n_sub * TOK), lambda p, i: (0, i))],
        out_specs=[pl.BlockSpec((n_sub, 2, TOK), lambda p, i: (i * p, 0, 0)), _full((NE, LANES))],
        out_shape=[jax.ShapeDtypeStruct((nt, 2, TOK), jnp.int32),
                   jax.ShapeDtypeStruct((NE, LANES), jnp.int32)],
        scratch_shapes=[pltpu.VMEM((NE, LANES), f32)] * 3,
        compiler_params=_cp("arbitrary", "arbitrary"),
        name="moe_positions",
    )(route_t)


def _row_copy(src, src_row, dst, dst_row, sem):
    return pltpu.make_async_copy(src.at[pl.ds(src_row, 1)], dst.at[pl.ds(dst_row, 1)], sem)


def _dispatch_kernel(pos_ref, hp_ref, hs_ref, xs_ref, sem, *, n_p_tiles):
    i = pl.program_id(0)
    tok = hp_ref.shape[0]

    def scatter_rows(h_ref):
        def issue(r, c):
            _row_copy(h_ref, r, xs_ref, pos_ref[r], sem).start(priority=0)
            _row_copy(h_ref, r, xs_ref, pos_ref[tok + r], sem).start(priority=1)
            return c

        lax.fori_loop(0, tok, issue, 0, unroll=ROW_DMA_UNROLL)
        for _ in range(2):
            pltpu.make_async_copy(h_ref, xs_ref.at[pl.ds(0, tok)], sem).wait()

    @pl.when(i < n_p_tiles)
    def _():
        scatter_rows(hp_ref)

    @pl.when(i >= n_p_tiles)
    def _():
        scatter_rows(hs_ref)


def _dispatch(pos, hp, hs):
    n_p, n_s = hp.shape[0], hs.shape[0]
    assert n_p % TOK == 0 and n_s % TOK == 0
    npt, nst = n_p // TOK, n_s // TOK
    return pl.pallas_call(
        functools.partial(_dispatch_kernel, n_p_tiles=npt),
        grid=(npt + nst,),
        in_specs=[pl.BlockSpec((2 * TOK,), lambda i: (i,), memory_space=pltpu.SMEM),
                  pl.BlockSpec((TOK, D), lambda i: (jnp.minimum(i, npt - 1), 0)),
                  pl.BlockSpec((TOK, D), lambda i: (jnp.maximum(i - npt, 0), 0))],
        out_specs=pl.BlockSpec(memory_space=pl.ANY),
        out_shape=jax.ShapeDtypeStruct((2 * (n_p + n_s), D), f32),
        scratch_shapes=[pltpu.SemaphoreType.DMA(())],
        compiler_params=_cp("arbitrary"),
        name="moe_dispatch",
    )(pos, hp, hs)


def _experts_kernel(tile_ref, exp_ref, lo_ref, hi_ref, xs_ref, wg_ref, wu_ref, wd_ref, o_ref):
    del tile_ref, exp_ref
    k = pl.program_id(0)
    lo = lo_ref[k]
    hi = hi_ref[k]

    @pl.when(hi > lo)
    def _():
        x = xs_ref[...].astype(bf16)
        ff = wg_ref.shape[2]
        acc = jnp.zeros(o_ref.shape, f32)
        for c0 in range(0, ff, FF_CHUNK_MOE):
            c1 = min(c0 + FF_CHUNK_MOE, ff)
            a = _dot(x, wg_ref[0, :, c0:c1])
            a = a * _sigmoid(a) * _dot(x, wu_ref[0, :, c0:c1])
            acc = acc + _dot(a.astype(bf16), wd_ref[0, c0:c1, :])
        @pl.when(lo == 0)
        def _():
            o_ref[...] = acc

        @pl.when(lo > 0)
        def _():
            row = lax.broadcasted_iota(jnp.int32, (o_ref.shape[0], 1), 0)
            o_ref[...] = jnp.where((row >= lo) & (row < hi), acc, o_ref[...])


def _experts(items, xs, wg, wu, wd):
    n_rows = xs.shape[0]
    ff = wg.shape[2]
    tm = TM_MOE
    assert n_rows % tm == 0
    n_items = items[0].shape[0]
    once = pl.Buffered(1)
    grid_spec = pltpu.PrefetchScalarGridSpec(
        num_scalar_prefetch=4,
        grid=(n_items,),
        in_specs=[pl.BlockSpec((tm, D), lambda k, ti, ex, lo, hi: (ti[k], 0)),
                  pl.BlockSpec((1, D, ff), lambda k, ti, ex, lo, hi: (ex[k], 0, 0), pipeline_mode=once),
                  pl.BlockSpec((1, D, ff), lambda k, ti, ex, lo, hi: (ex[k], 0, 0), pipeline_mode=once),
                  pl.BlockSpec((1, ff, D), lambda k, ti, ex, lo, hi: (ex[k], 0, 0), pipeline_mode=once)],
        out_specs=pl.BlockSpec((tm, D), lambda k, ti, ex, lo, hi: (ti[k], 0)))
    return pl.pallas_call(
        _experts_kernel,
        grid_spec=grid_spec,
        out_shape=jax.ShapeDtypeStruct((n_rows, D), f32),
        compiler_params=_cp("arbitrary"),
        name="moe_experts",
    )(*items, xs, wg, wu, wd)


def _work_items(off, n_rows):
    tm = TM_MOE
    n_tiles = n_rows // tm
    n_items = n_tiles + NE - 1
    start = off
    end = jnp.concatenate([off[1:], jnp.array([n_rows], jnp.int32)])
    cnt = end - start
    first = start // tm
    last = jnp.where(cnt > 0, (end - 1) // tm, first - 1)
    per = last - first + 1
    cum = jnp.cumsum(per)
    k = jnp.arange(n_items, dtype=jnp.int32)
    e = jnp.minimum(jnp.sum(k[:, None] >= cum[None, :], axis=1), NE - 1).astype(jnp.int32)
    tile = first[e] + (k - (cum[e] - per[e]))
    real = k < cum[NE - 1]
    e = jnp.where(real, e, jnp.max(jnp.where(real, e, 0)))
    tile = jnp.where(real, tile, n_tiles - 1).astype(jnp.int32)
    lo = jnp.clip(start[e] - tile * tm, 0, tm)
    hi = jnp.clip(end[e] - tile * tm, 0, tm)
    lo = jnp.where(real, lo, 0).astype(jnp.int32)
    hi = jnp.where(real, hi, 0).astype(jnp.int32)
    return tile, e, lo, hi


def _combine_kernel(pos_ref, pos_next_ref, x3_ref, route_ref, o_hbm, y_ref, buf, sem, *, n_tiles):
    i = pl.program_id(0)
    tok = x3_ref.shape[0]
    slot = i % 2

    def gather(p_ref, sl):
        def issue(r, c):
            _row_copy(o_hbm, p_ref[r], buf.at[sl, 0], r, sem.at[sl]).start(priority=0)
            _row_copy(o_hbm, p_ref[tok + r], buf.at[sl, 1], r, sem.at[sl]).start(priority=1)
            return c

        lax.fori_loop(0, tok, issue, 0, unroll=ROW_DMA_UNROLL)

    @pl.when(i == 0)
    def _():
        gather(pos_ref, slot)

    if n_tiles > 1:
        @pl.when(i + 1 < n_tiles)
        def _():
            gather(pos_next_ref, 1 - slot)

    for s in range(2):
        pltpu.make_async_copy(o_hbm.at[pl.ds(0, tok)], buf.at[slot, s], sem.at[slot]).wait()
    route = route_ref[...]
    y_ref[...] = x3_ref[...] + route[:, 2:3] * buf[slot, 0] + route[:, 3:4] * buf[slot, 1]


def _combine(pos, x3, route, o_sorted, tile0):
    n = x3.shape[0]
    assert n % TOK == 0
    nt = n // TOK
    pos_spec = lambda d: pl.BlockSpec((2 * TOK,), lambda i: (jnp.minimum(i + d, nt - 1) + tile0,),
                                      memory_space=pltpu.SMEM)
    return pl.pallas_call(
        functools.partial(_combine_kernel, n_tiles=nt),
        grid=(nt,),
        in_specs=[pos_spec(0), pos_spec(1),
                  pl.BlockSpec((TOK, D), lambda i: (i, 0)),
                  pl.BlockSpec((TOK, LANES), lambda i: (i, 0)),
                  pl.BlockSpec(memory_space=pl.ANY)],
        out_specs=pl.BlockSpec((TOK, D), lambda i: (i, 0)),
        out_shape=jax.ShapeDtypeStruct((n, D), f32),
        scratch_shapes=[pltpu.VMEM((2, 2, TOK, D), f32), pltpu.SemaphoreType.DMA((2,))],
        compiler_params=_cp("arbitrary"),
        name="moe_combine",
    )(pos, pos, x3, route, o_sorted)


def kernel(x_prompt, x_sample, state_conv, cache_k_win, cache_v_win, g_conv_norm, w_pw1, b_pw1, w_dw, b_dw, g_ln, b_ln, w_pw2, b_pw2, g_kv_norm, w_kv, g_k_norm, g_attn_norm, w_q, g_q_norm, sinks, w_o, rel_bias, g_ffn_norm, w_gate, w_up, w_down, g_moe_norm, w_router, w_e_gate, w_e_up, w_e_down):
    b, t, _ = x_prompt.shape
    bd, s_len, _ = x_sample.shape
    n_p = b * t
    n_s = bd * s_len
    row = lambda a: a.reshape(1, -1).astype(f32)

    conv_w = (row(g_conv_norm[0]), w_pw1[0].astype(bf16), row(b_pw1[0]),
              jnp.pad(w_dw[0], ((0, HALO - CONV_W), (0, 0))), row(b_dw[0]), row(g_ln[0]), row(b_ln[0]),
              w_pw2[0].astype(bf16), row(b_pw2[0]))
    ffn_w = (row(g_ffn_norm[0]), w_gate[0].astype(bf16), w_up[0].astype(bf16), w_down[0].astype(bf16))
    wq = w_q[0].reshape(D, NKV, G, DH).transpose(0, 2, 1, 3).reshape(D, NH * DH).astype(bf16)
    wo = w_o[0].reshape(NKV, G, DH, D).transpose(1, 0, 2, 3).reshape(NH * DH, D).astype(bf16)
    wr = jnp.pad(w_router[0], ((0, 0), (0, LANES - NE))).astype(bf16)
    head_of_lane = np.repeat(np.eye(NH, LANES), DH, axis=0)
    attn_w = (row(g_kv_norm), w_kv.astype(bf16), row(jnp.tile(g_k_norm, NKV)), row(g_attn_norm[0]),
              wq, row(jnp.tile(g_q_norm[0], NH)), jnp.asarray(head_of_lane, bf16),
              jnp.asarray(head_of_lane.T, bf16), wo, row(g_moe_norm[0]), wr)
    weg, weu, wed = w_e_gate[0].astype(bf16), w_e_up[0].astype(bf16), w_e_down[0].astype(bf16)
    sink = sinks[0].astype(f32)

    bias_p = _bias_table(rel_bias, _prompt_bucket_map()).reshape(2, NKV, G * WIN, 2 * WIN)
    map_c, map_n = _sample_bucket_maps(s_len, SEQ_TILE)
    rows = s_len * SEQ_TILE
    bias_c = _bias_table(rel_bias, map_c).reshape(NH * rows, SEQ_TILE * WIN)
    bias_n = _bias_table(rel_bias, map_n).reshape(NH * rows, LANES)

    x1p, ulast = _conv_prompt(x_prompt, conv_w)
    x2p = _ffn(x1p.reshape(n_p, D), *ffn_w).reshape(b, t, D)
    x3p, hp, route_p, route_tp, klast, vlast = _attn_prompt(x2p, sink, bias_p, attn_w)

    xs_tb = x_sample.transpose(1, 0, 2)
    x1s, snew = _conv_sample(xs_tb, state_conv[0], conv_w)
    x2s = _ffn(x1s.reshape(n_s, D), *ffn_w)
    kc = cache_k_win.reshape(bd, WIN, KVW)
    vc = cache_v_win.reshape(bd, WIN, KVW)
    x3s, hs, route_s, route_ts, kwin, vwin = _attn_sample(x2s, kc, vc, sink, bias_c, bias_n, attn_w)

    pos, off = _positions(jnp.concatenate([route_tp, route_ts], axis=1))
    pos = pos.reshape(-1)
    n_rows = 2 * (n_p + n_s)
    xs = _dispatch(pos, hp, hs)
    o_sorted = _experts(_work_items(off[:, 0], n_rows), xs, weg, weu, wed)
    y_p = _combine(pos, x3p.reshape(n_p, D), route_p, o_sorted, 0).reshape(b, t, D)
    y_s = _combine(pos, x3s, route_s, o_sorted, n_p // TOK).reshape(s_len, bd, D).transpose(1, 0, 2)

    n_prev = CONV_W - 1
    conv_p = ulast[:, HALO - n_prev:, :][None]
    conv_s = snew[None]
    k_p = klast.reshape(b, WIN, NKV, DH)
    v_p = vlast.reshape(b, WIN, NKV, DH)
    k_s = kwin.reshape(bd, WIN, NKV, DH)
    v_s = vwin.reshape(bd, WIN, NKV, DH)
    return (y_p, y_s, conv_p, conv_s, k_p, k_s, v_p, v_s)
```

```python
import functools

import numpy as np
import jax
import jax.numpy as jnp
from jax import lax
from jax.experimental import pallas as pl
from jax.experimental.pallas import tpu as pltpu

D = 1024
CONV_W = 31
HALO = 32
DH = 64
NH = 16
NKV = 4
G = NH // NKV
KVW = NKV * DH
WIN = 128
NB = 32
MAXD = 128
NE = 8
EPS = 1e-6
LANES = 128
NEG = float("-inf")

TT_CONV = 512
TM_FFN = 1024
TQ_ATT = 512
TOK = 512
TM_MOE = 512
ROW_DMA_UNROLL = 32
SEQ_TILE = 8
SEQ_TILE_CONV = 32
FF_CHUNK = 1792
FF_CHUNK_MOE = 512
VMEM_LIMIT = 56 * 1024 * 1024

bf16 = jnp.bfloat16
f32 = jnp.float32


def _cp(*sem):
    return pltpu.CompilerParams(dimension_semantics=sem, vmem_limit_bytes=VMEM_LIMIT)


def _full(shape, once=False):
    n = len(shape)
    return pl.BlockSpec(shape, lambda *_: (0,) * n, pipeline_mode=pl.Buffered(1) if once else None)


def _sigmoid(x):
    return 0.5 * jnp.tanh(0.5 * x) + 0.5


def _rms(x):
    return x * lax.rsqrt(jnp.mean(x * x, axis=-1, keepdims=True) + EPS)


def _dot(a, b):
    return jnp.dot(a, b, preferred_element_type=f32)


def _dot_nt(a, b):
    return lax.dot_general(a, b, (((1,), (1,)), ((), ())), preferred_element_type=f32)


def _split_dot(x, w):
    hi = x.astype(bf16)
    lo = (x - hi.astype(f32)).astype(bf16)
    return _dot(hi, w) + _dot(lo, w)


def _head_rms(x, gain, ind_ref, spread_ref):
    c = x.shape[1]
    ss = _dot((x * x).astype(bf16), ind_ref[0:c, :])
    r = lax.rsqrt(ss * (1.0 / DH) + EPS)
    return x * _split_dot(r, spread_ref[:, 0:c]) * gain


def _kv_lane_mask(kv):
    lane = lax.broadcasted_iota(jnp.int32, (1, KVW), 1)
    return (lane >= kv * DH) & (lane < (kv + 1) * DH)


def _bucket_of(dist):
    n = np.maximum(dist, 0)
    max_exact = NB // 2
    large = max_exact + (np.log(np.maximum(n, 1) / max_exact) / np.log(MAXD / max_exact)
                         * (NB - max_exact)).astype(np.int32)
    large = np.minimum(large, NB - 1)
    return np.where(n < max_exact, n, large).astype(np.int32)


def _prompt_bucket_map():
    dist = np.arange(WIN)[:, None] + WIN - np.arange(2 * WIN)[None, :]
    valid = (dist >= 0) & (dist < WIN)
    bk = np.where(valid, _bucket_of(dist), -1).astype(np.int32)
    first = bk.copy()
    first[:, :WIN] = -1
    return np.stack([first, bk])


def _sample_bucket_maps(s_len, st):
    r_t = np.repeat(np.arange(s_len), st)
    r_b = np.tile(np.arange(st), s_len)
    c_b = np.repeat(np.arange(st), WIN)
    c_s = np.tile(np.arange(WIN), st)
    dist = r_t[:, None] + WIN - c_s[None, :]
    valid = (r_b[:, None] == c_b[None, :]) & (dist >= 0) & (dist < WIN)
    map_c = np.where(valid, _bucket_of(dist), -1).astype(np.int32)
    dist_n = r_t[:, None] - r_t[None, :]
    valid_n = (r_b[:, None] == r_b[None, :]) & (dist_n >= 0) & (dist_n < WIN)
    map_n = np.full((s_len * st, LANES), -1, np.int32)
    map_n[:, :s_len * st] = np.where(valid_n, _bucket_of(dist_n), -1)
    return map_c[None], map_n[None]


def _bias_kernel(rb_ref, map_ref, out_ref):
    bk = map_ref[0]
    for h in range(NH):
        acc = jnp.full(bk.shape, NEG, f32)
        for b in range(NB):
            acc = jnp.where(bk == b, rb_ref[b, h], acc)
        out_ref[0, h] = acc


def _bias_table(rel_bias, bucket_map):
    v, r, c = bucket_map.shape
    return pl.pallas_call(
        _bias_kernel,
        grid=(v,),
        in_specs=[pl.BlockSpec(memory_space=pltpu.SMEM),
                  pl.BlockSpec((1, r, c), lambda i: (i, 0, 0))],
        out_specs=pl.BlockSpec((1, NH, r, c), lambda i: (i, 0, 0, 0)),
        out_shape=jax.ShapeDtypeStruct((v, NH, r, c), f32),
        compiler_params=_cp("arbitrary"),
        name="bias_table",
    )(rel_bias, jnp.asarray(bucket_map))


def _glu_rows(x, gn_ref, w1_ref, b1_ref):
    h = (_rms(x) * gn_ref[...]).astype(bf16)
    u2 = _dot(h, w1_ref[...]) + b1_ref[...]
    return u2[:, :D] * _sigmoid(u2[:, D:])


def _conv_tail(x, c, gln_ref, bln_ref, w2_ref, b2_ref):
    mu = jnp.mean(c, axis=-1, keepdims=True)
    xc = c - mu
    var = jnp.mean(xc * xc, axis=-1, keepdims=True)
    y = xc * lax.rsqrt(var + EPS) * gln_ref[...] + bln_ref[...]
    y = y * _sigmoid(y)
    return x + _dot(y.astype(bf16), w2_ref[...]) + b2_ref[...]


def _swiglu_rows(x, g_ref, wg_ref, wu_ref, wd_ref):
    h = (_rms(x) * g_ref[...]).astype(bf16)
    ff = wg_ref.shape[1]
    acc = x
    for c0 in range(0, ff, FF_CHUNK):
        c1 = min(c0 + FF_CHUNK, ff)
        a = _dot(h, wg_ref[:, c0:c1])
        a = a * _sigmoid(a) * _dot(h, wu_ref[:, c0:c1])
        acc = acc + _dot(a.astype(bf16), wd_ref[c0:c1, :])
    return acc


def _conv_prompt_kernel(x_ref, gn_ref, w1_ref, b1_ref, wdw_ref, bdw_ref, gln_ref, bln_ref,
                        w2_ref, b2_ref, x1_ref, ulast_ref, ubuf, ush):
    tt = x_ref.shape[1]
    t = pl.program_id(1)
    x = x_ref[0]
    u = _glu_rows(x, gn_ref, w1_ref, b1_ref)

    @pl.when(t == 0)
    def _():
        ubuf[0:HALO, :] = jnp.zeros((HALO, D), f32)

    @pl.when(t > 0)
    def _():
        ubuf[0:HALO, :] = ubuf[tt:tt + HALO, :]

    ubuf[HALO:HALO + tt, :] = u
    ulast_ref[0] = u[tt - HALO:, :]
    n_sh = ush.shape[1]
    for r in range(1, 8):
        ush[r - 1] = ubuf[r:r + n_sh, :]
    off = HALO - (CONV_W - 1)
    c = jnp.broadcast_to(bdw_ref[...], (tt, D))
    for k in range(CONV_W):
        a, r = divmod(off + k, 8)
        src = ubuf[8 * a:8 * a + tt, :] if r == 0 else ush[r - 1, 8 * a:8 * a + tt, :]
        c = c + wdw_ref[k:k + 1, :] * src
    x1_ref[0] = _conv_tail(x, c, gln_ref, bln_ref, w2_ref, b2_ref)


def _conv_prompt(x, wts):
    b, t, _ = x.shape
    tt = min(TT_CONV, t)
    assert t % tt == 0 and tt >= HALO
    row = lambda n: _full((1, n))
    return pl.pallas_call(
        _conv_prompt_kernel,
        grid=(b, t // tt),
        in_specs=[pl.BlockSpec((1, tt, D), lambda i, j: (i, j, 0)),
                  row(D), _full((D, 2 * D)), row(2 * D), _full((HALO, D)), row(D), row(D), row(D),
                  _full((D, D)), row(D)],
        out_specs=[pl.BlockSpec((1, tt, D), lambda i, j: (i, j, 0)),
                   pl.BlockSpec((1, HALO, D), lambda i, j: (i, 0, 0))],
        out_shape=[jax.ShapeDtypeStruct((b, t, D), f32),
                   jax.ShapeDtypeStruct((b, HALO, D), f32)],
        scratch_shapes=[pltpu.VMEM((tt + HALO, D), f32),
                        pltpu.VMEM((7, tt + HALO - 8, D), f32)],
        compiler_params=_cp("arbitrary", "arbitrary"),
        name="conv_prompt",
    )(x, *wts)


def _conv_sample_kernel(x_ref, st_ref, gn_ref, w1_ref, b1_ref, wdw_ref, bdw_ref, gln_ref, bln_ref,
                        w2_ref, b2_ref, x1_ref, snew_ref):
    s_len, bs, _ = x_ref.shape
    x = x_ref[...].reshape(s_len * bs, D)
    u = _glu_rows(x, gn_ref, w1_ref, b1_ref)
    n_prev = CONV_W - 1
    snew_ref[:, 0:n_prev - s_len, :] = st_ref[:, s_len:, :]
    for t in range(s_len):
        for b in range(bs):
            snew_ref[b, n_prev - s_len + t:n_prev - s_len + t + 1, :] = u[t * bs + b:t * bs + b + 1, :]
    cs = []
    for t in range(s_len):
        c = jnp.broadcast_to(bdw_ref[...], (bs, D))
        for j in range(t, n_prev):
            c = c + wdw_ref[j - t:j - t + 1, :] * st_ref[:, j, :]
        for i in range(t + 1):
            k = n_prev - t + i
            c = c + wdw_ref[k:k + 1, :] * u[i * bs:(i + 1) * bs, :]
        cs.append(c)
    c = jnp.concatenate(cs, axis=0)
    x1_ref[...] = _conv_tail(x, c, gln_ref, bln_ref, w2_ref, b2_ref).reshape(s_len, bs, D)


def _conv_sample(x_tb, state, wts):
    s_len, bd, _ = x_tb.shape
    bs = min(SEQ_TILE_CONV, bd)
    assert bd % bs == 0 and bs % 8 == 0 and s_len <= CONV_W - 1
    row = lambda n: _full((1, n))
    blk = pl.BlockSpec((s_len, bs, D), lambda i: (0, i, 0))
    seq = pl.BlockSpec((bs, CONV_W - 1, D), lambda i: (i, 0, 0))
    return pl.pallas_call(
        _conv_sample_kernel,
        grid=(bd // bs,),
        in_specs=[blk, seq,
                  row(D), _full((D, 2 * D)), row(2 * D), _full((HALO, D)), row(D), row(D), row(D),
                  _full((D, D)), row(D)],
        out_specs=[blk, seq],
        out_shape=[jax.ShapeDtypeStruct((s_len, bd, D), f32),
                   jax.ShapeDtypeStruct((bd, CONV_W - 1, D), f32)],
        compiler_params=_cp("arbitrary"),
        name="conv_sample",
    )(x_tb, state, *wts)


def _ffn_kernel(x_ref, g_ref, wg_ref, wu_ref, wd_ref, o_ref):
    o_ref[...] = _swiglu_rows(x_ref[...], g_ref, wg_ref, wu_ref, wd_ref)


def _ffn(x, g, wg, wu, wd):
    n = x.shape[0]
    tm = min(TM_FFN, n)
    assert n % tm == 0
    ff = wg.shape[1]
    return pl.pallas_call(
        _ffn_kernel,
        grid=(n // tm,),
        in_specs=[pl.BlockSpec((tm, D), lambda i: (i, 0)), _full((1, D)),
                  _full((D, ff)), _full((D, ff)), _full((ff, D))],
        out_specs=pl.BlockSpec((tm, D), lambda i: (i, 0)),
        out_shape=jax.ShapeDtypeStruct((n, D), f32),
        compiler_params=_cp("arbitrary"),
        name="dense_ffn",
    )(x, g, wg, wu, wd)


def _qkv(x, gkv_ref, wkv_ref, gk_ref, gq_attn_ref, wq_ref, gq_ref, ind_ref, spread_ref):
    xn = _rms(x)
    kv = _dot((xn * gkv_ref[...]).astype(bf16), wkv_ref[...])
    k = _head_rms(kv[:, :KVW], gk_ref[...], ind_ref, spread_ref)
    v = kv[:, KVW:]
    q = _dot((xn * gq_attn_ref[...]).astype(bf16), wq_ref[...])
    q = _head_rms(q, gq_ref[...], ind_ref, spread_ref) * (DH ** -0.5)
    return q, k, v


def _sink_col(sinks_ref, kv, rows_per_head):
    return jnp.concatenate(
        [jnp.full((rows_per_head, 1), sinks_ref[kv * G + g], f32) for g in range(G)], axis=0)


def _moe_prep(x3, gmoe_ref, wr_ref, h_ref, route_ref, route_t_ref):
    h = _rms(x3) * gmoe_ref[...]
    h_ref[...] = h
    logits = _dot(h.astype(bf16), wr_ref[...])
    lane = lax.broadcasted_iota(jnp.int32, logits.shape, 1).astype(f32)
    logits = jnp.where(lane < NE, logits, NEG)
    m1 = jnp.max(logits, axis=-1, keepdims=True)
    i1 = jnp.min(jnp.where(logits == m1, lane, float(LANES)), axis=-1, keepdims=True)
    rest = jnp.where(lane == i1, NEG, logits)
    m2 = jnp.max(rest, axis=-1, keepdims=True)
    i2 = jnp.min(jnp.where(rest == m2, lane, float(LANES)), axis=-1, keepdims=True)
    e2 = jnp.exp(m2 - m1)
    den = 1.0 + e2
    g1 = 1.0 / den
    g2 = e2 / den
    route = jnp.where(lane == 0, i1,
                      jnp.where(lane == 1, i2,
                                jnp.where(lane == 2, g1, jnp.where(lane == 3, g2, 0.0))))
    route_ref[...] = route
    route_t_ref[...] = route.T[0:8, :]


def _attn_prompt_kernel(sinks_ref, x_ref, gkv_ref, wkv_ref, gk_ref, gqa_ref, wq_ref, gq_ref,
                        ind_ref, spread_ref, ones_ref, bias_ref, wo_ref, gmoe_ref, wr_ref,
                        x3_ref, h_ref, route_ref, route_t_ref, klast_ref, vlast_ref,
                        kbuf, vbuf, obuf):
    tq = x_ref.shape[1]
    t = pl.program_id(1)
    x = x_ref[0]
    q, k, v = _qkv(x, gkv_ref, wkv_ref, gk_ref, gqa_ref, wq_ref, gq_ref, ind_ref, spread_ref)
    klast_ref[0] = k[tq - WIN:, :]
    vlast_ref[0] = v[tq - WIN:, :]

    @pl.when(t == 0)
    def _():
        kbuf[:, 0:WIN, :] = jnp.zeros((NKV, WIN, KVW), bf16)
        vbuf[:, 0:WIN, :] = jnp.zeros((NKV, WIN, KVW), bf16)

    @pl.when(t > 0)
    def _():
        kbuf[:, 0:WIN, :] = kbuf[:, tq:tq + WIN, :]
        vbuf[:, 0:WIN, :] = vbuf[:, tq:tq + WIN, :]

    kb = k.astype(bf16)
    vb = v.astype(bf16)
    for kv in range(NKV):
        m = _kv_lane_mask(kv)
        kbuf[kv, WIN:WIN + tq, :] = jnp.where(m, kb, jnp.zeros_like(kb))
        vbuf[kv, WIN:WIN + tq, :] = jnp.where(m, vb, jnp.zeros_like(vb))

    qb = q.astype(bf16)
    for n in range(tq // WIN):
        r0 = n * WIN
        lhs = jnp.concatenate([qb[r0:r0 + WIN, g * KVW:(g + 1) * KVW] for g in range(G)], axis=0)
        ps = []
        lane = lax.broadcasted_iota(jnp.int32, (1, LANES), 1)
        sink_term = jnp.zeros((G * WIN, LANES), f32)
        for kv in range(NKV):
            s = _dot_nt(lhs, kbuf[kv, r0:r0 + 2 * WIN, :])
            s = s + bias_ref[jnp.where(t == 0, 0, 1) if n == 0 else 1, kv]
            sink = _sink_col(sinks_ref, kv, WIN)
            mx = jnp.maximum(jnp.max(s, axis=-1, keepdims=True), sink)
            ps.append(jnp.exp(s - mx).astype(bf16))
            sink_term = jnp.where(lane == kv, jnp.exp(sink - mx), sink_term)
        pcat = jnp.concatenate(ps, axis=-1)
        vcat = jnp.concatenate([vbuf[kv, r0:r0 + 2 * WIN, :] for kv in range(NKV)], axis=0)
        den = _dot(pcat, ones_ref[...]) + sink_term
        inv = 1.0 / jnp.where(lane < NKV, den, 1.0)
        pv = _dot(pcat, vcat) * _split_dot(inv, spread_ref[:, 0:KVW])
        obuf[r0:r0 + WIN, :] = jnp.concatenate(
            [pv[g * WIN:(g + 1) * WIN, :] for g in range(G)], axis=-1).astype(bf16)

    x3 = x + _dot(obuf[...], wo_ref[...])
    x3_ref[0] = x3
    _moe_prep(x3, gmoe_ref, wr_ref, h_ref, route_ref, route_t_ref)


def _attn_prompt(x2, sinks, bias, wts):
    b, t, _ = x2.shape
    tq = min(TQ_ATT, t)
    assert t % tq == 0 and tq % WIN == 0
    nt = t // tq
    gkv, wkv, gk, gqa, wq, gq, ind, spread, wo, gmoe, wr = wts
    ones = jnp.asarray(np.repeat(np.eye(NKV, LANES), 2 * WIN, axis=0), bf16)
    row = lambda n: _full((1, n))
    tok = lambda w: pl.BlockSpec((tq, w), lambda i, j: (i * nt + j, 0))
    last = pl.BlockSpec((1, WIN, KVW), lambda i, j: (i, 0, 0))
    return pl.pallas_call(
        _attn_prompt_kernel,
        grid=(b, nt),
        in_specs=[pl.BlockSpec(memory_space=pltpu.SMEM),
                  pl.BlockSpec((1, tq, D), lambda i, j: (i, j, 0)),
                  row(D), _full((D, 2 * KVW)), row(KVW), row(D), _full((D, D)), row(D),
                  _full(ind.shape), _full(spread.shape), _full(ones.shape),
                  _full(bias.shape), _full((D, D)), row(D), _full((D, LANES))],
        out_specs=[pl.BlockSpec((1, tq, D), lambda i, j: (i, j, 0)), tok(D), tok(LANES),
                   pl.BlockSpec((8, tq), lambda i, j: (0, i * nt + j)), last, last],
        out_shape=[jax.ShapeDtypeStruct((b, t, D), f32),
                   jax.ShapeDtypeStruct((b * t, D), f32),
                   jax.ShapeDtypeStruct((b * t, LANES), f32),
                   jax.ShapeDtypeStruct((8, b * t), f32),
                   jax.ShapeDtypeStruct((b, WIN, KVW), f32),
                   jax.ShapeDtypeStruct((b, WIN, KVW), f32)],
        scratch_shapes=[pltpu.VMEM((NKV, tq + WIN, KVW), bf16),
                        pltpu.VMEM((NKV, tq + WIN, KVW), bf16),
                        pltpu.VMEM((tq, D), bf16)],
        compiler_params=_cp("arbitrary", "arbitrary"),
        name="attn_prompt",
    )(sinks, x2, gkv, wkv, gk, gqa, wq, gq, ind, spread, ones, bias, wo, gmoe, wr)


def _attn_sample_kernel(sinks_ref, x_ref, kc_ref, vc_ref, gkv_ref, wkv_ref, gk_ref, gqa_ref, wq_ref,
                        gq_ref, ind_ref, spread_ref, bias_c_ref, bias_n_ref, wo_ref, gmoe_ref, wr_ref,
                        x3_ref, h_ref, route_ref, route_t_ref, kwin_ref, vwin_ref,
                        qbuf, kbuf, vbuf, obuf, *, bd):
    st = kc_ref.shape[0]
    s_len = x_ref.shape[0] // bd
    rows = s_len * st
    i = pl.program_id(0)

    @pl.when(i == 0)
    def _():
        q, k, v = _qkv(x_ref[...], gkv_ref, wkv_ref, gk_ref, gqa_ref, wq_ref, gq_ref, ind_ref, spread_ref)
        kbuf[...] = k
        vbuf[...] = v
        qbuf[...] = q

    def step_rows(buf, t):
        return buf[pl.ds(pl.multiple_of(t * bd + i * st, 8), st), :]

    def tile_rows(buf):
        return jnp.concatenate([step_rows(buf, t) for t in range(s_len)], axis=0).astype(bf16)

    for win_ref, c_ref, nbuf in ((kwin_ref, kc_ref, kbuf), (vwin_ref, vc_ref, vbuf)):
        win_ref[:, 0:WIN - s_len, :] = c_ref[:, s_len:, :]
        for t in range(s_len):
            new = step_rows(nbuf, t)
            for b in range(st):
                win_ref[b, WIN - s_len + t:WIN - s_len + t + 1, :] = new[b:b + 1, :]

    qt = tile_rows(qbuf)
    lhs = jnp.concatenate([qt[:, g * KVW:(g + 1) * KVW] for g in range(G)], axis=0)
    pad = jnp.zeros((LANES - rows, KVW), bf16)
    kn = jnp.concatenate([tile_rows(kbuf), pad], axis=0)
    vn = jnp.concatenate([tile_rows(vbuf), pad], axis=0)
    kc = kc_ref[...].reshape(st * WIN, KVW).astype(bf16)
    vc = vc_ref[...].reshape(st * WIN, KVW).astype(bf16)
    pv = jnp.zeros((G * rows, KVW), f32)
    for kv in range(NKV):
        m = _kv_lane_mask(kv)
        zc = jnp.zeros_like(kc)
        zn = jnp.zeros_like(kn)
        b0 = kv * G * rows
        s_c = _dot_nt(lhs, jnp.where(m, kc, zc)) + bias_c_ref[b0:b0 + G * rows, :]
        s_n = _dot_nt(lhs, jnp.where(m, kn, zn)) + bias_n_ref[b0:b0 + G * rows, :]
        sink = _sink_col(sinks_ref, kv, rows)
        mx = jnp.maximum(jnp.maximum(jnp.max(s_c, axis=-1, keepdims=True),
                                     jnp.max(s_n, axis=-1, keepdims=True)), sink)
        p_c = jnp.exp(s_c - mx)
        p_n = jnp.exp(s_n - mx)
        den = (jnp.sum(p_c, axis=-1, keepdims=True) + jnp.sum(p_n, axis=-1, keepdims=True)
               + jnp.exp(sink - mx))
        inv = 1.0 / den
        pv = pv + _dot((p_c * inv).astype(bf16), jnp.where(m, vc, zc))
        pv = pv + _dot((p_n * inv).astype(bf16), jnp.where(m, vn, zn))
    o = jnp.concatenate([pv[g * rows:(g + 1) * rows, :] for g in range(G)], axis=-1)
    for t in range(s_len):
        obuf[pl.ds(pl.multiple_of(t * bd + i * st, 8), st), :] = o[t * st:(t + 1) * st, :]

    @pl.when(i == pl.num_programs(0) - 1)
    def _():
        x3 = x_ref[...] + _dot(obuf[...].astype(bf16), wo_ref[...])
        x3_ref[...] = x3
        _moe_prep(x3, gmoe_ref, wr_ref, h_ref, route_ref, route_t_ref)


def _attn_sample(x2, kc, vc, sinks, bias_c, bias_n, wts):
    n_tok = x2.shape[0]
    bd = kc.shape[0]
    st = SEQ_TILE
    assert bd % st == 0 and n_tok % bd == 0 and (n_tok // bd) * st <= LANES
    gkv, wkv, gk, gqa, wq, gq, ind, spread, wo, gmoe, wr = wts
    row = lambda n: _full((1, n))
    cache = pl.BlockSpec((st, WIN, KVW), lambda i: (i, 0, 0))
    return pl.pallas_call(
        functools.partial(_attn_sample_kernel, bd=bd),
        grid=(bd // st,),
        in_specs=[pl.BlockSpec(memory_space=pltpu.SMEM), _full((n_tok, D)), cache, cache,
                  row(D), _full((D, 2 * KVW)), row(KVW), row(D), _full((D, D)), row(D),
                  _full(ind.shape), _full(spread.shape), _full(bias_c.shape), _full(bias_n.shape), _full((D, D)), row(D), _full((D, LANES))],
        out_specs=[_full((n_tok, D)), _full((n_tok, D)), _full((n_tok, LANES)), _full((8, n_tok)),
                   cache, cache],
        out_shape=[jax.ShapeDtypeStruct((n_tok, D), f32),
                   jax.ShapeDtypeStruct((n_tok, D), f32),
                   jax.ShapeDtypeStruct((n_tok, LANES), f32),
                   jax.ShapeDtypeStruct((8, n_tok), f32),
                   jax.ShapeDtypeStruct((bd, WIN, KVW), f32),
                   jax.ShapeDtypeStruct((bd, WIN, KVW), f32)],
        scratch_shapes=[pltpu.VMEM((n_tok, D), f32), pltpu.VMEM((n_tok, KVW), f32),
                        pltpu.VMEM((n_tok, KVW), f32), pltpu.VMEM((n_tok, D), f32)],
        compiler_params=_cp("arbitrary"),
        name="attn_sample",
    )(sinks, x2, kc, vc, gkv, wkv, gk, gqa, wq, gq, ind, spread, bias_c, bias_n, wo, gmoe, wr)


def _positions_kernel(rt_ref, pos_ref, off_ref, counts, running, offs):
    ph = pl.program_id(0)
    i = pl.program_id(1)
    n_sub, _, tok = pos_ref.shape
    sub = lax.broadcasted_iota(jnp.int32, (NE, tok), 0).astype(f32)

    def tile(s):
        e0 = rt_ref[0:1, s * tok:(s + 1) * tok]
        e1 = rt_ref[1:2, s * tok:(s + 1) * tok]
        sel = ((sub == e0) | (sub == e1)).astype(f32)
        return e0, e1, sel, jnp.broadcast_to(jnp.sum(sel, axis=-1, keepdims=True), (NE, LANES))

    @pl.when((ph == 0) & (i == 0))
    def _():
        counts[...] = jnp.zeros((NE, LANES), f32)

    @pl.when(ph == 0)
    def _():
        for s in range(n_sub):
            counts[...] += tile(s)[3]

    @pl.when((ph == 1) & (i == 0))
    def _():
        sub_l = lax.broadcasted_iota(jnp.int32, (NE, LANES), 0)
        acc = jnp.zeros((NE, LANES), f32)
        for e in range(NE - 1):
            acc = acc + jnp.where(sub_l > e, counts[e:e + 1, :], 0.0)
        offs[...] = acc
        running[...] = jnp.zeros((NE, LANES), f32)
        off_ref[...] = acc.astype(jnp.int32)

    @pl.when(ph == 1)
    def _():
        r_i = lax.broadcasted_iota(jnp.int32, (tok, tok), 0)
        c_i = lax.broadcasted_iota(jnp.int32, (tok, tok), 1)
        upper = (r_i < c_i).astype(bf16)
        for s in range(n_sub):
            e0, e1, sel, tile_cnt = tile(s)
            cum = _dot(sel.astype(bf16), upper)
            tot = cum + (offs[:, 0:1] + running[:, 0:1])
            p0 = jnp.sum(jnp.where(sub == e0, tot, 0.0), axis=0, keepdims=True)
            p1 = jnp.sum(jnp.where(sub == e1, tot, 0.0), axis=0, keepdims=True)
            pos_ref[s] = jnp.concatenate([p0, p1], axis=0).astype(jnp.int32)
            running[...] += tile_cnt


def _positions(route_t):
    n = route_t.shape[1]
    assert n % TOK == 0
    nt = n // TOK
    n_sub = max(d for d in range(1, 9) if nt % d == 0)
    return pl.pallas_call(
        _positions_kernel,
        grid=(2, nt // n_sub),
        in_specs=[pl.BlockSpec((8, n_sub * TOK), lambda p, i: (0, i))],
        out_specs=[pl.BlockSpec((n_sub, 2, TOK), lambda p, i: (i * p, 0, 0)), _full((NE, LANES))],
        out_shape=[jax.ShapeDtypeStruct((nt, 2, TOK), jnp.int32),
                   jax.ShapeDtypeStruct((NE, LANES), jnp.int32)],
        scratch_shapes=[pltpu.VMEM((NE, LANES), f32)] * 3,
        compiler_params=_cp("arbitrary", "arbitrary"),
        name="moe_positions",
    )(route_t)


def _row_copy(src, src_row, dst, dst_row, sem):
    return pltpu.make_async_copy(src.at[pl.ds(src_row, 1)], dst.at[pl.ds(dst_row, 1)], sem)


def _dispatch_kernel(pos_ref, hp_ref, hs_ref, xs_ref, sem, *, n_p_tiles, n_tiles):
    i = pl.program_id(0)
    tok = TOK
    slot = i % 2

    def scatter_rows(h_ref, row0):
        def issue(r, c):
            _row_copy(h_ref, row0 + r, xs_ref, pos_ref[r], sem.at[slot]).start(priority=0)
            _row_copy(h_ref, row0 + r, xs_ref, pos_ref[tok + r], sem.at[slot]).start(priority=1)
            return c

        lax.fori_loop(0, tok, issue, 0, unroll=ROW_DMA_UNROLL)

    def drain(s):
        for _ in range(2):
            pltpu.make_async_copy(hp_ref.at[pl.ds(0, tok)], xs_ref.at[pl.ds(0, tok)], sem.at[s]).wait()

    @pl.when(i < n_p_tiles)
    def _():
        scatter_rows(hp_ref, i * tok)

    @pl.when(i >= n_p_tiles)
    def _():
        scatter_rows(hs_ref, (i - n_p_tiles) * tok)

    @pl.when(i > 0)
    def _():
        drain(1 - slot)

    @pl.when(i == n_tiles - 1)
    def _():
        drain(slot)


def _dispatch(pos, hp, hs):
    n_p, n_s = hp.shape[0], hs.shape[0]
    assert n_p % TOK == 0 and n_s % TOK == 0
    npt, nst = n_p // TOK, n_s // TOK
    return pl.pallas_call(
        functools.partial(_dispatch_kernel, n_p_tiles=npt, n_tiles=npt + nst),
        grid=(npt + nst,),
        in_specs=[pl.BlockSpec((2 * TOK,), lambda i: (i,), memory_space=pltpu.SMEM),
                  pl.BlockSpec(memory_space=pl.ANY),
                  pl.BlockSpec(memory_space=pl.ANY)],
        out_specs=pl.BlockSpec(memory_space=pl.ANY),
        out_shape=jax.ShapeDtypeStruct((2 * (n_p + n_s), D), f32),
        scratch_shapes=[pltpu.SemaphoreType.DMA((2,))],
        compiler_params=_cp("arbitrary"),
        name="moe_dispatch",
    )(pos, hp, hs)


def _experts_kernel(tile_ref, exp_ref, lo_ref, hi_ref, xs_ref, wg_ref, wu_ref, wd_ref, o_ref):
    del tile_ref, exp_ref
    k = pl.program_id(0)
    lo = lo_ref[k]
    hi = hi_ref[k]

    @pl.when(hi > lo)
    def _():
        x = xs_ref[...].astype(bf16)
        ff = wg_ref.shape[2]
        acc = jnp.zeros(o_ref.shape, f32)
        for c0 in range(0, ff, FF_CHUNK_MOE):
            c1 = min(c0 + FF_CHUNK_MOE, ff)
            a = _dot(x, wg_ref[0, :, c0:c1])
            a = a * _sigmoid(a) * _dot(x, wu_ref[0, :, c0:c1])
            acc = acc + _dot(a.astype(bf16), wd_ref[0, c0:c1, :])
        @pl.when(lo == 0)
        def _():
            o_ref[...] = acc

        @pl.when(lo > 0)
        def _():
            row = lax.broadcasted_iota(jnp.int32, (o_ref.shape[0], 1), 0)
            o_ref[...] = jnp.where((row >= lo) & (row < hi), acc, o_ref[...])


def _experts(items, xs, wg, wu, wd):
    n_rows = xs.shape[0]
    ff = wg.shape[2]
    tm = TM_MOE
    assert n_rows % tm == 0
    n_items = items[0].shape[0]
    once = pl.Buffered(1)
    grid_spec = pltpu.PrefetchScalarGridSpec(
        num_scalar_prefetch=4,
        grid=(n_items,),
        in_specs=[pl.BlockSpec((tm, D), lambda k, ti, ex, lo, hi: (ti[k], 0)),
                  pl.BlockSpec((1, D, ff), lambda k, ti, ex, lo, hi: (ex[k], 0, 0), pipeline_mode=once),
                  pl.BlockSpec((1, D, ff), lambda k, ti, ex, lo, hi: (ex[k], 0, 0), pipeline_mode=once),
                  pl.BlockSpec((1, ff, D), lambda k, ti, ex, lo, hi: (ex[k], 0, 0), pipeline_mode=once)],
        out_specs=pl.BlockSpec((tm, D), lambda k, ti, ex, lo, hi: (ti[k], 0)))
    return pl.pallas_call(
        _experts_kernel,
        grid_spec=grid_spec,
        out_shape=jax.ShapeDtypeStruct((n_rows, D), f32),
        compiler_params=_cp("arbitrary"),
        name="moe_experts",
    )(*items, xs, wg, wu, wd)


def _work_items(off, n_rows):
    tm = TM_MOE
    n_tiles = n_rows // tm
    n_items = n_tiles + NE - 1
    start = off
    end = jnp.concatenate([off[1:], jnp.array([n_rows], jnp.int32)])
    cnt = end - start
    first = start // tm
    last = jnp.where(cnt > 0, (end - 1) // tm, first - 1)
    per = last - first + 1
    cum = jnp.cumsum(per)
    k = jnp.arange(n_items, dtype=jnp.int32)
    e = jnp.minimum(jnp.sum(k[:, None] >= cum[None, :], axis=1), NE - 1).astype(jnp.int32)
    tile = first[e] + (k - (cum[e] - per[e]))
    real = k < cum[NE - 1]
    e = jnp.where(real, e, jnp.max(jnp.where(real, e, 0)))
    tile = jnp.where(real, tile, n_tiles - 1).astype(jnp.int32)
    lo = jnp.clip(start[e] - tile * tm, 0, tm)
    hi = jnp.clip(end[e] - tile * tm, 0, tm)
    lo = jnp.where(real, lo, 0).astype(jnp.int32)
    hi = jnp.where(real, hi, 0).astype(jnp.int32)
    return tile, e, lo, hi


def _combine_kernel(pos_ref, pos_next_ref, x3_ref, route_ref, o_hbm, y_ref, buf, sem, *, n_tiles):
    i = pl.program_id(0)
    tok = x3_ref.shape[0]
    slot = i % 2

    def gather(p_ref, sl):
        def issue(r, c):
            _row_copy(o_hbm, p_ref[r], buf.at[sl, 0], r, sem.at[sl]).start(priority=0)
            _row_copy(o_hbm, p_ref[tok + r], buf.at[sl, 1], r, sem.at[sl]).start(priority=1)
            return c

        lax.fori_loop(0, tok, issue, 0, unroll=ROW_DMA_UNROLL)

    @pl.when(i == 0)
    def _():
        gather(pos_ref, slot)

    if n_tiles > 1:
        @pl.when(i + 1 < n_tiles)
        def _():
            gather(pos_next_ref, 1 - slot)

    for s in range(2):
        pltpu.make_async_copy(o_hbm.at[pl.ds(0, tok)], buf.at[slot, s], sem.at[slot]).wait()
    route = route_ref[...]
    y_ref[...] = x3_ref[...] + route[:, 2:3] * buf[slot, 0] + route[:, 3:4] * buf[slot, 1]


def _combine(pos, x3, route, o_sorted, tile0):
    n = x3.shape[0]
    assert n % TOK == 0
    nt = n // TOK
    pos_spec = lambda d: pl.BlockSpec((2 * TOK,), lambda i: (jnp.minimum(i + d, nt - 1) + tile0,),
                                      memory_space=pltpu.SMEM)
    return pl.pallas_call(
        functools.partial(_combine_kernel, n_tiles=nt),
        grid=(nt,),
        in_specs=[pos_spec(0), pos_spec(1),
                  pl.BlockSpec((TOK, D), lambda i: (i, 0)),
                  pl.BlockSpec((TOK, LANES), lambda i: (i, 0)),
                  pl.BlockSpec(memory_space=pl.ANY)],
        out_specs=pl.BlockSpec((TOK, D), lambda i: (i, 0)),
        out_shape=jax.ShapeDtypeStruct((n, D), f32),
        scratch_shapes=[pltpu.VMEM((2, 2, TOK, D), f32), pltpu.SemaphoreType.DMA((2,))],
        compiler_params=_cp("arbitrary"),
        name="moe_combine",
    )(pos, pos, x3, route, o_sorted)


def kernel(x_prompt, x_sample, state_conv, cache_k_win, cache_v_win, g_conv_norm, w_pw1, b_pw1, w_dw, b_dw, g_ln, b_ln, w_pw2, b_pw2, g_kv_norm, w_kv, g_k_norm, g_attn_norm, w_q, g_q_norm, sinks, w_o, rel_bias, g_ffn_norm, w_gate, w_up, w_down, g_moe_norm, w_router, w_e_gate, w_e_up, w_e_down):
    b, t, _ = x_prompt.shape
    bd, s_len, _ = x_sample.shape
    n_p = b * t
    n_s = bd * s_len
    row = lambda a: a.reshape(1, -1).astype(f32)

    conv_w = (row(g_conv_norm[0]), w_pw1[0].astype(bf16), row(b_pw1[0]),
              jnp.pad(w_dw[0], ((0, HALO - CONV_W), (0, 0))), row(b_dw[0]), row(g_ln[0]), row(b_ln[0]),
              w_pw2[0].astype(bf16), row(b_pw2[0]))
    ffn_w = (row(g_ffn_norm[0]), w_gate[0].astype(bf16), w_up[0].astype(bf16), w_down[0].astype(bf16))
    wq = w_q[0].reshape(D, NKV, G, DH).transpose(0, 2, 1, 3).reshape(D, NH * DH).astype(bf16)
    wo = w_o[0].reshape(NKV, G, DH, D).transpose(1, 0, 2, 3).reshape(NH * DH, D).astype(bf16)
    wr = jnp.pad(w_router[0], ((0, 0), (0, LANES - NE))).astype(bf16)
    head_of_lane = np.repeat(np.eye(NH, LANES), DH, axis=0)
    attn_w = (row(g_kv_norm), w_kv.astype(bf16), row(jnp.tile(g_k_norm, NKV)), row(g_attn_norm[0]),
              wq, row(jnp.tile(g_q_norm[0], NH)), jnp.asarray(head_of_lane, bf16),
              jnp.asarray(head_of_lane.T, bf16), wo, row(g_moe_norm[0]), wr)
    weg, weu, wed = w_e_gate[0].astype(bf16), w_e_up[0].astype(bf16), w_e_down[0].astype(bf16)
    sink = sinks[0].astype(f32)

    bias_p = _bias_table(rel_bias, _prompt_bucket_map()).reshape(2, NKV, G * WIN, 2 * WIN)
    map_c, map_n = _sample_bucket_maps(s_len, SEQ_TILE)
    rows = s_len * SEQ_TILE
    bias_c = _bias_table(rel_bias, map_c).reshape(NH * rows, SEQ_TILE * WIN)
    bias_n = _bias_table(rel_bias, map_n).reshape(NH * rows, LANES)

    x1p, ulast = _conv_prompt(x_prompt, conv_w)
    x2p = _ffn(x1p.reshape(n_p, D), *ffn_w).reshape(b, t, D)
    x3p, hp, route_p, route_tp, klast, vlast = _attn_prompt(x2p, sink, bias_p, attn_w)

    xs_tb = x_sample.transpose(1, 0, 2)
    x1s, snew = _conv_sample(xs_tb, state_conv[0], conv_w)
    x2s = _ffn(x1s.reshape(n_s, D), *ffn_w)
    kc = cache_k_win.reshape(bd, WIN, KVW)
    vc = cache_v_win.reshape(bd, WIN, KVW)
    x3s, hs, route_s, route_ts, kwin, vwin = _attn_sample(x2s, kc, vc, sink, bias_c, bias_n, attn_w)

    pos, off = _positions(jnp.concatenate([route_tp, route_ts], axis=1))
    pos = pos.reshape(-1)
    n_rows = 2 * (n_p + n_s)
    xs = _dispatch(pos, hp, hs)
    o_sorted = _experts(_work_items(off[:, 0], n_rows), xs, weg, weu, wed)
    y_p = _combine(pos, x3p.reshape(n_p, D), route_p, o_sorted, 0).reshape(b, t, D)
    y_s = _combine(pos, x3s, route_s, o_sorted, n_p // TOK).reshape(s_len, bd, D).transpose(1, 0, 2)

    n_prev = CONV_W - 1
    conv_p = ulast[:, HALO - n_prev:, :][None]
    conv_s = snew[None]
    k_p = klast.reshape(b, WIN, NKV, DH)
    v_p = vlast.reshape(b, WIN, NKV, DH)
    k_s = kwin.reshape(bd, WIN, NKV, DH)
    v_s = vwin.reshape(bd, WIN, NKV, DH)
    return (y_p, y_s, conv_p, conv_s, k_p, k_s, v_p, v_s)
```
